```python
import math
import jax, jax.numpy as jnp
from jax import lax
import numpy as np

D_MODEL = 1024
BATCH = 2
SEQ = 16384
DEPTH = 2

HEAD_DIM = 64
GRID_W = 64
NA_HEADS = 4
SW_HEADS = 6
SW_KV_HEADS = 2
AX_HEADS = 6
AX_KV_HEADS = 2
MIX_WIDTH = (NA_HEADS + SW_HEADS + AX_HEADS) * HEAD_DIM
NA_WIN_ROWS = 8
NA_WIN_COLS = 16
SW_RADIUS = 128
BLOCK = 128
T5_BUCKETS = 32
T5_MAX_DIST = 128
ROPE_THETA = 10000.0
FFN_HIDDEN = ((8 * D_MODEL + 3 * 256 - 1) // (3 * 256)) * 256
IN_SPLITS = (NA_HEADS * HEAD_DIM, NA_HEADS * HEAD_DIM, NA_HEADS * HEAD_DIM,
             SW_HEADS * HEAD_DIM, SW_KV_HEADS * HEAD_DIM, SW_KV_HEADS * HEAD_DIM,
             AX_HEADS * HEAD_DIM, AX_KV_HEADS * HEAD_DIM, AX_KV_HEADS * HEAD_DIM)
IN_WIDTH = sum(IN_SPLITS)
GROUP_WIDTHS = (NA_HEADS * HEAD_DIM, SW_HEADS * HEAD_DIM, AX_HEADS * HEAD_DIM)
EPS = 1e-6
NEG_INF = -1e30

kernel_name = "hybrid_parallel_heads_encoder"


def rms_norm(x, g):
    xf = x.astype(jnp.float32)
    y = xf * lax.rsqrt(jnp.mean(xf * xf, axis=-1, keepdims=True) + EPS)
    return (y * g.astype(jnp.float32)).astype(x.dtype)


def split_cols(t, sizes):
    out = []
    start = 0
    for s in sizes:
        out.append(t[..., start:start + s])
        start += s
    return out


def neighborhood_attention(q, k, v, rpb):
    B, S, H, d = q.shape
    rows = S // GRID_W
    kh = min(NA_WIN_ROWS, rows)
    kw = NA_WIN_COLS
    qg = q.reshape(B, rows, GRID_W, H, d)
    kg = k.reshape(B, rows, GRID_W, H, d)
    vg = v.reshape(B, rows, GRID_W, H, d)
    cols = jnp.arange(GRID_W)
    col_start = jnp.clip(cols - kw // 2, 0, GRID_W - kw)
    col_idx = col_start[:, None] + jnp.arange(kw)[None, :]
    col_off = col_idx - cols[:, None] + (NA_WIN_COLS - 1)
    scale = d ** -0.5

    def one_row(r):
        rs = jnp.clip(r - kh // 2, 0, rows - kh)
        k_band = lax.dynamic_slice_in_dim(kg, rs, kh, axis=1)
        v_band = lax.dynamic_slice_in_dim(vg, rs, kh, axis=1)
        k_nb = k_band[:, :, col_idx]
        v_nb = v_band[:, :, col_idx]
        q_row = lax.dynamic_index_in_dim(qg, r, axis=1, keepdims=False)
        s = jnp.einsum('bqhd,brqwhd->bhqrw', q_row, k_nb).astype(jnp.float32) * scale
        row_off = rs + jnp.arange(kh) - r + (NA_WIN_ROWS - 1)
        bias = rpb[:, row_off[None, :, None], col_off[:, None, :]]
        s = s + bias[None].astype(jnp.float32)
        p = jax.nn.softmax(s.reshape(B, H, GRID_W, kh * kw), axis=-1)
        p = p.reshape(B, H, GRID_W, kh, kw).astype(v.dtype)
        return jnp.einsum('bhqrw,brqwhd->bqhd', p, v_nb)

    out = lax.map(one_row, jnp.arange(rows))
    return out.transpose(1, 0, 2, 3, 4).reshape(B, S, H * d)


def t5_bucket(rel):
    nb = T5_BUCKETS // 2
    ret = (rel > 0).astype(jnp.int32) * nb
    n = jnp.abs(rel)
    max_exact = nb // 2
    nf = jnp.maximum(n, max_exact).astype(jnp.float32)
    large = max_exact + (jnp.log(nf / max_exact) / math.log(T5_MAX_DIST / max_exact)
                         * (nb - max_exact)).astype(jnp.int32)
    large = jnp.minimum(large, nb - 1)
    return ret + jnp.where(n < max_exact, n, large)


def sliding_window_attention(q, k, v, sink, t5_table):
    B, S, H, d = q.shape
    G = k.shape[2]
    R = H // G
    nb = S // BLOCK
    scale = d ** -0.5
    qb = q.reshape(B, nb, BLOCK, G, R, d)

    def band(t):
        tb = t.reshape(B, nb, BLOCK, G, d)
        pad = jnp.zeros_like(tb[:, :1])
        tp = jnp.concatenate([pad, tb, pad], axis=1)
        return jnp.concatenate([tp[:, :-2], tp[:, 1:-1], tp[:, 2:]], axis=2)

    kb = band(k)
    vb = band(v)
    qpos = jnp.arange(BLOCK)
    kpos = jnp.arange(3 * BLOCK) - BLOCK
    rel = kpos[None, :] - qpos[:, None]
    bias = t5_table[t5_bucket(rel)].astype(jnp.float32)
    bias = jnp.transpose(bias, (2, 0, 1)).reshape(G, R, BLOCK, 3 * BLOCK)
    kabs = jnp.arange(nb)[:, None] * BLOCK + kpos[None, :]
    valid = (jnp.abs(rel) <= SW_RADIUS)[None] & ((kabs >= 0) & (kabs < S))[:, None, :]
    s = jnp.einsum('bnqgrd,bnkgd->bngrqk', qb, kb).astype(jnp.float32) * scale + bias
    s = jnp.where(valid[None, :, None, None], s, NEG_INF)
    sink_logits = jnp.broadcast_to(sink.reshape(G, R, 1, 1).astype(jnp.float32), s.shape[:-1] + (1,))
    p = jax.nn.softmax(jnp.concatenate([s, sink_logits], axis=-1), axis=-1)[..., :-1]
    o = jnp.einsum('bngrqk,bnkgd->bnqgrd', p.astype(v.dtype), vb)
    return o.reshape(B, S, H * d)


def axial_rope_tables(S):
    t = jnp.arange(S)
    row = (t // GRID_W).astype(jnp.float32)
    col = (t % GRID_W).astype(jnp.float32)
    axis_dim = HEAD_DIM // 2
    freqs = ROPE_THETA ** (-jnp.arange(0, axis_dim, 2, dtype=jnp.float32) / axis_dim)
    ang = jnp.stack([row[:, None] * freqs, col[:, None] * freqs], axis=1)
    return jnp.cos(ang), jnp.sin(ang)


def apply_axial_rope(x, cos, sin):
    B, S, H, d = x.shape
    xr = x.reshape(B, S, H, 2, 2, d // 4).astype(jnp.float32)
    x1 = xr[..., 0, :]
    x2 = xr[..., 1, :]
    c = cos[None, :, None]
    s = sin[None, :, None]
    out = jnp.stack([x1 * c - x2 * s, x2 * c + x1 * s], axis=-2)
    return out.reshape(B, S, H, d).astype(x.dtype)


def axial_attention(q, k, v, gq, gk):
    B, S, H, d = q.shape
    G = k.shape[2]
    R = H // G
    nb = S // BLOCK
    scale = d ** -0.5
    cos, sin = axial_rope_tables(S)
    q = apply_axial_rope(rms_norm(q, gq), cos, sin)
    k = apply_axial_rope(rms_norm(k, gk), cos, sin)
    qb = q.reshape(B, nb, BLOCK, G, R, d).transpose(1, 0, 2, 3, 4, 5)

    def one_block(qblk):
        s = jnp.einsum('bqgrd,bkgd->bgrqk', qblk, k).astype(jnp.float32) * scale
        p = jax.nn.softmax(s, axis=-1).astype(v.dtype)
        return jnp.einsum('bgrqk,bkgd->bqgrd', p, v)

    o = lax.map(one_block, qb)
    return o.transpose(1, 0, 2, 3, 4, 5).reshape(B, S, H * d)


def setup_inputs(seed: int = 0) -> dict:
    key = jax.random.key(seed)
    ks = jax.random.split(key, 18)
    D = D_MODEL
    L = DEPTH
    F = FFN_HIDDEN

    def nrm(k, shape, scale):
        return jax.random.normal(k, shape, jnp.float32) * scale

    return {
        "x": nrm(ks[0], (BATCH, SEQ, D), 1.0),
        "c": nrm(ks[1], (BATCH, D), 1.0),
        "w_mod": nrm(ks[2], (L, D, 6 * D), 0.5 * D ** -0.5),
        "b_mod": nrm(ks[3], (L, 6 * D), 0.01),
        "g_attn": 1.0 + nrm(ks[4], (L, D), 0.01),
        "w_in": nrm(ks[5], (L, D, IN_WIDTH), D ** -0.5),
        "rpb_na": nrm(ks[6], (L, NA_HEADS, 2 * NA_WIN_ROWS - 1, 2 * NA_WIN_COLS - 1), 0.1),
        "sink_sw": nrm(ks[7], (L, SW_HEADS), 0.5),
        "t5_table": nrm(ks[8], (T5_BUCKETS, SW_HEADS), 0.1),
        "gq_ax": 1.0 + nrm(ks[9], (L, HEAD_DIM), 0.01),
        "gk_ax": 1.0 + nrm(ks[10], (L, HEAD_DIM), 0.01),
        "g_group": 1.0 + nrm(ks[11], (L, MIX_WIDTH), 0.01),
        "w_o": nrm(ks[12], (L, MIX_WIDTH, D), MIX_WIDTH ** -0.5),
        "g_ffn": 1.0 + nrm(ks[13], (L, D), 0.01),
        "w_gu": nrm(ks[14], (L, D, 2 * F), D ** -0.5),
        "w_down": nrm(ks[15], (L, F, D), F ** -0.5),
        "g_final": 1.0 + nrm(ks[16], (D,), 0.01),
    }


def reference(x, c, w_mod, b_mod, g_attn, w_in, rpb_na, sink_sw, t5_table, gq_ax, gk_ax,
              g_group, w_o, g_ffn, w_gu, w_down, g_final):
    B, S, D = x.shape
    cond = jax.nn.silu(c)
    for l in range(DEPTH):
        mod = cond @ w_mod[l] + b_mod[l]
        sh_a, sc_a, gt_a, sh_f, sc_f, gt_f = [m[:, None, :] for m in split_cols(mod, (D,) * 6)]

        h = rms_norm(x, g_attn[l]) * (1 + sc_a) + sh_a
        proj = h @ w_in[l]
        qa, ka, va, qb, kb, vb, qc, kc, vc = split_cols(proj, IN_SPLITS)
        ya = neighborhood_attention(qa.reshape(B, S, NA_HEADS, HEAD_DIM),
                                    ka.reshape(B, S, NA_HEADS, HEAD_DIM),
                                    va.reshape(B, S, NA_HEADS, HEAD_DIM), rpb_na[l])
        yb = sliding_window_attention(qb.reshape(B, S, SW_HEADS, HEAD_DIM),
                                      kb.reshape(B, S, SW_KV_HEADS, HEAD_DIM),
                                      vb.reshape(B, S, SW_KV_HEADS, HEAD_DIM),
                                      sink_sw[l], t5_table)
        yc = axial_attention(qc.reshape(B, S, AX_HEADS, HEAD_DIM),
                             kc.reshape(B, S, AX_KV_HEADS, HEAD_DIM),
                             vc.reshape(B, S, AX_KV_HEADS, HEAD_DIM), gq_ax[l], gk_ax[l])
        ga, gb, gc = split_cols(g_group[l], GROUP_WIDTHS)
        y = jnp.concatenate([rms_norm(ya, ga), rms_norm(yb, gb), rms_norm(yc, gc)], axis=-1)
        x = x + gt_a * (y @ w_o[l])

        h = rms_norm(x, g_ffn[l]) * (1 + sc_f) + sh_f
        gate, up = split_cols(h @ w_gu[l], (FFN_HIDDEN, FFN_HIDDEN))
        x = x + gt_f * ((jax.nn.silu(gate) * up) @ w_down[l])
    return rms_norm(x, g_final)
```

```python
import functools
import math

import jax
import jax.numpy as jnp
import numpy as np
from jax import lax
from jax.experimental import pallas as pl
from jax.experimental.pallas import tpu as pltpu

D_MODEL = 1024
HEAD_DIM = 64
GRID_W = 64
NA_HEADS = 4
SW_HEADS = 6
SW_KV_HEADS = 2
AX_HEADS = 6
AX_KV_HEADS = 2
NA_WIN_ROWS = 8
NA_WIN_COLS = 16
SW_RADIUS = 128
SW_BLOCK = 128
T5_BUCKETS = 32
T5_MAX_DIST = 128
ROPE_THETA = 10000.0
FFN_HIDDEN = 2816
EPS = 1e-6
NEG_INF = -1e30

NA_W = NA_HEADS * HEAD_DIM
SW_QW = SW_HEADS * HEAD_DIM
SW_KW = SW_KV_HEADS * HEAD_DIM
AX_QW = AX_HEADS * HEAD_DIM
AX_KW = AX_KV_HEADS * HEAD_DIM
IN_WIDTH = 3 * NA_W + SW_QW + 2 * SW_KW + AX_QW + 2 * AX_KW
OFF_QA, OFF_KA, OFF_VA = 0, NA_W, 2 * NA_W
OFF_QB = 3 * NA_W
OFF_KB = OFF_QB + SW_QW
OFF_VB = OFF_KB + SW_KW
OFF_QC = OFF_VB + SW_KW
OFF_KC = OFF_QC + AX_QW
OFF_VC = OFF_KC + AX_KW

LANES = 128
VMEM_LIMIT = 56 * 1024 * 1024

TOK_TILE = 512
AX_TQ = 256
AX_TK = TOK_TILE
NA_QROWS = 4
NA_BAND = NA_QROWS + NA_WIN_ROWS
SW_TQ = 512
FFN_CHUNK = 512

QK_SCALE = HEAD_DIM ** -0.5


def _cparams(*sem):
    return pltpu.CompilerParams(dimension_semantics=sem, vmem_limit_bytes=VMEM_LIMIT)


def _const_spec(shape):
    n = len(shape)
    return pl.BlockSpec(shape, lambda *_: (0,) * n, pipeline_mode=pl.Buffered(1))


def _rms(x, g):
    return x * lax.rsqrt(jnp.mean(x * x, axis=-1, keepdims=True) + EPS) * g


def _mod_kernel(c_ref, w_ref, b_ref, o_ref):
    c = c_ref[...]
    cond = c * (1.0 / (1.0 + jnp.exp(-c)))
    o_ref[0] = jnp.dot(cond, w_ref[0], preferred_element_type=jnp.float32,
                       precision=lax.Precision.HIGHEST) + b_ref[0]


def _modulation(c_pad, w_mod, b_mod):
    L, D, N = w_mod.shape
    tn = 1536
    return pl.pallas_call(
        _mod_kernel,
        out_shape=jax.ShapeDtypeStruct((L, c_pad.shape[0], N), jnp.float32),
        grid=(L, N // tn),
        in_specs=[pl.BlockSpec(c_pad.shape, lambda l, j: (0, 0)),
                  pl.BlockSpec((1, D, tn), lambda l, j: (l, 0, j)),
                  pl.BlockSpec((1, 1, tn), lambda l, j: (l, 0, j))],
        out_specs=pl.BlockSpec((1, c_pad.shape[0], tn), lambda l, j: (l, 0, j)),
        compiler_params=_cparams("arbitrary", "arbitrary"),
        name="adaln_mod",
    )(c_pad, w_mod, b_mod.reshape(L, 1, N))


def _head_sumsq(t, bd):
    t2 = t * t
    hi = t2.astype(jnp.bfloat16)
    lo = (t2 - hi.astype(jnp.float32)).astype(jnp.bfloat16)
    return (jnp.dot(hi, bd, preferred_element_type=jnp.float32)
            + jnp.dot(lo, bd, preferred_element_type=jnp.float32))


def _rope_chunk(t, cos, sin_signed, first_half):
    swapped = jnp.where(first_half, pltpu.roll(t, LANES - 16, 1), pltpu.roll(t, 16, 1))
    return t * cos + swapped * sin_signed


def _in_kernel(x_ref, sc_ref, sh_ref, g_ref, w_ref, cos_ref, sin_ref, gq_ref, gk_ref,
               bdq_ref, bdk_ref,
               qa_ref, ka_ref, va_ref, qb_ref, kb_ref, vb_ref, qct_ref, kc_ref, vct_ref):
    x = x_ref[0]
    h = _rms(x, g_ref[...]) * (1.0 + sc_ref[0]) + sh_ref[0]
    proj = jnp.dot(h.astype(jnp.bfloat16), w_ref[...], preferred_element_type=jnp.float32)

    bf = jnp.bfloat16
    qa_ref[0] = (proj[:, OFF_QA:OFF_QA + NA_W] * QK_SCALE).astype(bf)
    ka_ref[0] = proj[:, OFF_KA:OFF_KA + NA_W].astype(bf)
    va_ref[0] = proj[:, OFF_VA:OFF_VA + NA_W].astype(bf)
    qb_ref[0] = (proj[:, OFF_QB:OFF_QB + SW_QW] * QK_SCALE).astype(bf)
    kb_ref[0] = proj[:, OFF_KB:OFF_KB + SW_KW].astype(bf)
    vb_ref[0] = proj[:, OFF_VB:OFF_VB + SW_KW].astype(bf)

    cos = cos_ref[...]
    sin = sin_ref[...]
    lane = lax.broadcasted_iota(jnp.int32, cos.shape, 1)
    first_half = (lane % 32) < 16

    qc = proj[:, OFF_QC:OFF_QC + AX_QW]
    qn = qc * lax.rsqrt(_head_sumsq(qc, bdq_ref[...]) * (1.0 / HEAD_DIM) + EPS)
    chunks = []
    for j in range(AX_QW // LANES):
        t = qn[:, j * LANES:(j + 1) * LANES] * gq_ref[...]
        chunks.append(_rope_chunk(t, cos, sin, first_half) * QK_SCALE)
    qct_ref[0] = jnp.concatenate(chunks, axis=1).T.astype(bf)

    kc = proj[:, OFF_KC:OFF_KC + AX_KW]
    kn = kc * lax.rsqrt(_head_sumsq(kc, bdk_ref[...]) * (1.0 / HEAD_DIM) + EPS) * gk_ref[...]
    kc_ref[0] = _rope_chunk(kn, cos, sin, first_half).astype(bf)

    vct_ref[0, 0] = proj[:, OFF_VC:OFF_VC + AX_KW].T.astype(bf)


def _in_proj(x, sc, sh, g, w, cos, sin, gq, gk, bdq, bdk):
    B, S, D = x.shape
    tm = TOK_TILE
    bf = jnp.bfloat16
    tok = lambda width: pl.BlockSpec((1, tm, width), lambda b, i: (b, i, 0))
    vec = pl.BlockSpec((1, 1, D), lambda b, i: (b, 0, 0))
    out_shape = (
        jax.ShapeDtypeStruct((B, S, NA_W), bf), jax.ShapeDtypeStruct((B, S, NA_W), bf),
        jax.ShapeDtypeStruct((B, S, NA_W), bf),
        jax.ShapeDtypeStruct((B, S, SW_QW), bf), jax.ShapeDtypeStruct((B, S, SW_KW), bf),
        jax.ShapeDtypeStruct((B, S, SW_KW), bf),
        jax.ShapeDtypeStruct((B, AX_QW, S), bf), jax.ShapeDtypeStruct((B, S, AX_KW), bf),
        jax.ShapeDtypeStruct((B, S // tm, AX_KW, tm), bf),
    )
    out_specs = (
        tok(NA_W), tok(NA_W), tok(NA_W), tok(SW_QW), tok(SW_KW), tok(SW_KW),
        pl.BlockSpec((1, AX_QW, tm), lambda b, i: (b, 0, i)),
        tok(AX_KW),
        pl.BlockSpec((1, 1, AX_KW, tm), lambda b, i: (b, i, 0, 0)),
    )
    return pl.pallas_call(
        _in_kernel,
        out_shape=out_shape,
        grid=(B, S // tm),
        in_specs=[tok(D), vec, vec, _const_spec((1, D)), _const_spec(w.shape),
                  pl.BlockSpec((tm, LANES), lambda b, i: (i, 0)),
                  pl.BlockSpec((tm, LANES), lambda b, i: (i, 0)),
                  _const_spec((1, LANES)), _const_spec((1, LANES)),
                  _const_spec(bdq.shape), _const_spec(bdk.shape)],
        out_specs=out_specs,
        compiler_params=_cparams("arbitrary", "arbitrary"),
        name="in_proj",
    )(x, sc, sh, g, w, cos, sin, gq, gk, bdq, bdk)


def _na_kernel(q_ref, k_ref, v_ref, bias_ref, o_ref, *, rows):
    j = pl.program_id(1)
    band_row = jnp.clip(j * NA_QROWS - NA_WIN_ROWS // 2, 0, rows - NA_BAND)
    start = pl.multiple_of(band_row * GRID_W, GRID_W)
    kb = k_ref[0, pl.ds(start, NA_BAND * GRID_W), :]
    vb = v_ref[0, pl.ds(start, NA_BAND * GRID_W), :]
    q = q_ref[0]
    head_of_lane = lax.broadcasted_iota(jnp.int32, q.shape, 1) // HEAD_DIM
    out = jnp.zeros(q.shape, jnp.float32)
    for h in range(NA_HEADS):
        mine = head_of_lane == h
        qh = jnp.where(mine, q, jnp.zeros_like(q))
        s = lax.dot_general(qh, kb, (((1,), (1,)), ((), ())),
                            preferred_element_type=jnp.float32) + bias_ref[0, h]
        m = jnp.max(s, axis=-1, keepdims=True)
        p = jnp.exp(s - m)
        l = jnp.sum(p, axis=-1, keepdims=True)
        o = jnp.dot(p.astype(jnp.bfloat16), vb, preferred_element_type=jnp.float32)
        out = jnp.where(mine, o / l, out)
    o_ref[0] = out


def _na_attention(q, k, v, bias):
    B, S, W = q.shape
    rows = S // GRID_W
    tq = NA_QROWS * GRID_W
    nblk = S // tq

    def bias_idx(b, j):
        return (jnp.where(j == 0, 0, jnp.where(j == nblk - 1, 2, 1)), 0, 0, 0)

    return pl.pallas_call(
        functools.partial(_na_kernel, rows=rows),
        out_shape=jax.ShapeDtypeStruct((B, S, W), jnp.float32),
        grid=(B, nblk),
        in_specs=[pl.BlockSpec((1, tq, W), lambda b, j: (b, j, 0)),
                  pl.BlockSpec((1, S, W), lambda b, j: (b, 0, 0), pipeline_mode=pl.Buffered(1)),
                  pl.BlockSpec((1, S, W), lambda b, j: (b, 0, 0), pipeline_mode=pl.Buffered(1)),
                  pl.BlockSpec((1, NA_HEADS, tq, NA_BAND * GRID_W), bias_idx)],
        out_specs=pl.BlockSpec((1, tq, W), lambda b, j: (b, j, 0)),
        compiler_params=_cparams("arbitrary", "arbitrary"),
        name="na_attn",
    )(q, k, v, bias)


def _na_bias_tiles(rpb, rows):
    tq = NA_QROWS * GRID_W
    tk = NA_BAND * GRID_W
    qi = np.arange(tq)
    kj = np.arange(tk)
    ri, c = qi // GRID_W, qi % GRID_W
    kr_rel, kc = kj // GRID_W, kj % GRID_W
    cs = np.clip(c - NA_WIN_COLS // 2, 0, GRID_W - NA_WIN_COLS)
    col_ok = (kc[None, :] >= cs[:, None]) & (kc[None, :] < cs[:, None] + NA_WIN_COLS)
    col_off = np.clip(kc[None, :] - c[:, None] + NA_WIN_COLS - 1, 0, 2 * NA_WIN_COLS - 2)
    tiles = []
    last_q = rows - NA_QROWS
    for qr0 in (0, 2 * NA_QROWS, last_q):
        bs = int(np.clip(qr0 - NA_WIN_ROWS // 2, 0, rows - NA_BAND))
        r = qr0 + ri
        rs = np.clip(r - NA_WIN_ROWS // 2, 0, rows - NA_WIN_ROWS)
        kr = bs + kr_rel
        row_ok = (kr[None, :] >= rs[:, None]) & (kr[None, :] < rs[:, None] + NA_WIN_ROWS)
        row_off = np.clip(kr[None, :] - r[:, None] + NA_WIN_ROWS - 1, 0, 2 * NA_WIN_ROWS - 2)
        vals = rpb[:, row_off, col_off].astype(jnp.float32)
        tiles.append(jnp.where((row_ok & col_ok)[None], vals, NEG_INF))
    return jnp.stack(tiles)


def _sw_kernel(sink_ref, q_ref, k_ref, v_ref, bias_ref, o_ref, *, seq):
    i = pl.program_id(1)
    nblk_total = seq // SW_BLOCK
    col = lax.broadcasted_iota(jnp.int32, (SW_BLOCK, 3 * SW_BLOCK), 1)
    lane_group = lax.broadcasted_iota(jnp.int32, (SW_BLOCK, LANES), 1) // HEAD_DIM
    for blk in range(SW_TQ // SW_BLOCK):
        n = i * (SW_TQ // SW_BLOCK) + blk
        q0 = pl.multiple_of(n * SW_BLOCK, SW_BLOCK)
        left = pl.multiple_of(jnp.maximum(q0 - SW_BLOCK, 0), SW_BLOCK)
        right = pl.multiple_of(jnp.minimum(q0 + SW_BLOCK, seq - SW_BLOCK), SW_BLOCK)
        kw = jnp.concatenate([k_ref[0, pl.ds(left, SW_BLOCK), :], k_ref[0, pl.ds(q0, SW_BLOCK), :],
                              k_ref[0, pl.ds(right, SW_BLOCK), :]], axis=0)
        vw = jnp.concatenate([v_ref[0, pl.ds(left, SW_BLOCK), :], v_ref[0, pl.ds(q0, SW_BLOCK), :],
                              v_ref[0, pl.ds(right, SW_BLOCK), :]], axis=0)
        outside = ((n == 0) & (col < SW_BLOCK)) | ((n == nblk_total - 1) & (col >= 2 * SW_BLOCK))
        edge = jnp.where(outside, NEG_INF, 0.0)
        for r in range(SW_QW // LANES):
            qcol = q_ref[0, blk * SW_BLOCK:(blk + 1) * SW_BLOCK, r * LANES:(r + 1) * LANES]
            out = jnp.zeros((SW_BLOCK, LANES), jnp.float32)
            for g in range(SW_KV_HEADS):
                h = g * (SW_HEADS // SW_KV_HEADS) + r
                mine = lane_group == g
                qh = jnp.where(mine, qcol, jnp.zeros_like(qcol))
                s = lax.dot_general(qh, kw, (((1,), (1,)), ((), ())),
                                    preferred_element_type=jnp.float32)
                s = s + bias_ref[h] + edge
                sink = sink_ref[h]
                m = jnp.maximum(jnp.max(s, axis=-1, keepdims=True), sink)
                p = jnp.exp(s - m)
                l = jnp.sum(p, axis=-1, keepdims=True) + jnp.exp(sink - m)
                o = jnp.dot(p.astype(jnp.bfloat16), vw, preferred_element_type=jnp.float32)
                out = jnp.where(mine, o / l, out)
            o_ref[0, blk * SW_BLOCK:(blk + 1) * SW_BLOCK, r * LANES:(r + 1) * LANES] = out


def _sw_attention(q, k, v, bias, sink):
    B, S, _ = q.shape
    return pl.pallas_call(
        functools.partial(_sw_kernel, seq=S),
        out_shape=jax.ShapeDtypeStruct((B, S, SW_QW), jnp.float32),
        grid_spec=pltpu.PrefetchScalarGridSpec(
            num_scalar_prefetch=1,
            grid=(B, S // SW_TQ),
            in_specs=[pl.BlockSpec((1, SW_TQ, SW_QW), lambda b, i, s: (b, i, 0)),
                      pl.BlockSpec((1, S, SW_KW), lambda b, i, s: (b, 0, 0),
                                   pipeline_mode=pl.Buffered(1)),
                      pl.BlockSpec((1, S, SW_KW), lambda b, i, s: (b, 0, 0),
                                   pipeline_mode=pl.Buffered(1)),
                      pl.BlockSpec(bias.shape, lambda b, i, s: (0, 0, 0),
                                   pipeline_mode=pl.Buffered(1))],
            out_specs=pl.BlockSpec((1, SW_TQ, SW_QW), lambda b, i, s: (b, i, 0)),
        ),
        compiler_params=_cparams("arbitrary", "arbitrary"),
        name="sw_attn",
    )(sink, q, k, v, bias)


def _t5_bucket(rel):
    nb = T5_BUCKETS // 2
    ret = (rel > 0).astype(jnp.int32) * nb
    n = jnp.abs(rel)
    max_exact = nb // 2
    nf = jnp.maximum(n, max_exact).astype(jnp.float32)
    large = max_exact + (jnp.log(nf / max_exact) / math.log(T5_MAX_DIST / max_exact)
                         * (nb - max_exact)).astype(jnp.int32)
    large = jnp.minimum(large, nb - 1)
    return ret + jnp.where(n < max_exact, n, large)


def _sw_bias_tile(t5_table):
    qpos = jnp.arange(SW_BLOCK)
    kpos = jnp.arange(3 * SW_BLOCK) - SW_BLOCK
    rel = kpos[None, :] - qpos[:, None]
    bias = t5_table[_t5_bucket(rel)].astype(jnp.float32)
    bias = jnp.where((jnp.abs(rel) <= SW_RADIUS)[..., None], bias, NEG_INF)
    return jnp.transpose(bias, (2, 0, 1))


def _ax_kernel(qt_ref, k_ref, vt_ref, o_ref, acc_ref, *, nkt):
    tq = qt_ref.shape[2]
    rep = AX_HEADS // AX_KV_HEADS
    for h in range(AX_HEADS):
        g = h // rep
        qth = qt_ref[0, h * HEAD_DIM:(h + 1) * HEAD_DIM, :]
        zero = jnp.zeros_like(qth)
        qpad = jnp.concatenate([qth, zero] if g == 0 else [zero, qth], axis=0)

        def body(kt, carry):
            m, l, acc = carry
            ks = pl.multiple_of(kt * AX_TK, AX_TK)
            kt_rows = k_ref[0, pl.ds(ks, AX_TK), :]
            s = jnp.dot(kt_rows, qpad, preferred_element_type=jnp.float32)
            m_new = jnp.maximum(m, jnp.max(s, axis=0, keepdims=True))
            alpha = jnp.exp(m - m_new)
            p = jnp.exp(s - m_new)
            l_new = alpha * l + jnp.sum(p, axis=0, keepdims=True)
            vt = vt_ref[0, kt, g * HEAD_DIM:(g + 1) * HEAD_DIM, :]
            pv = jnp.dot(vt, p.astype(jnp.bfloat16), preferred_element_type=jnp.float32)
            return m_new, l_new, alpha * acc + pv

        m0 = jnp.full((1, tq), -jnp.inf, jnp.float32)
        l0 = jnp.zeros((1, tq), jnp.float32)
        a0 = jnp.zeros((HEAD_DIM, tq), jnp.float32)
        _, l, acc = lax.fori_loop(0, nkt, body, (m0, l0, a0))
        acc_ref[h * HEAD_DIM:(h + 1) * HEAD_DIM, :] = acc / l
    o_ref[0] = acc_ref[...].T


def _ax_attention(qt, k, vt):
    B, W, S = qt.shape
    nkt = vt.shape[1]
    return pl.pallas_call(
        functools.partial(_ax_kernel, nkt=nkt),
        out_shape=jax.ShapeDtypeStruct((B, S, W), jnp.float32),
        grid=(B, S // AX_TQ),
        in_specs=[pl.BlockSpec((1, W, AX_TQ), lambda b, i: (b, 0, i)),
                  pl.BlockSpec((1, S, AX_KW), lambda b, i: (b, 0, 0), pipeline_mode=pl.Buffered(1)),
                  pl.BlockSpec((1, nkt, AX_KW, AX_TK), lambda b, i: (b, 0, 0, 0),
                               pipeline_mode=pl.Buffered(1))],
        out_specs=pl.BlockSpec((1, AX_TQ, W), lambda b, i: (b, i, 0)),
        scratch_shapes=[pltpu.VMEM((W, AX_TQ), jnp.float32)],
        compiler_params=_cparams("arbitrary", "arbitrary"),
        name="ax_attn",
    )(qt, k, vt)


def _post_kernel(x_ref, ya_ref, yb_ref, yc_ref, gg_ref, wo_ref, gta_ref, gf_ref, scf_ref, shf_ref,
                 gtf_ref, wgu_ref, wd_ref, gfin_ref, o_ref, *, final):
    bf = jnp.bfloat16
    gg = gg_ref[...]
    y = jnp.concatenate([
        _rms(ya_ref[0], gg[:, 0:NA_W]).astype(bf),
        _rms(yb_ref[0], gg[:, NA_W:NA_W + SW_QW]).astype(bf),
        _rms(yc_ref[0], gg[:, NA_W + SW_QW:]).astype(bf)], axis=1)
    x1 = x_ref[0] + gta_ref[0] * jnp.dot(y, wo_ref[...], preferred_element_type=jnp.float32)

    h = (_rms(x1, gf_ref[...]) * (1.0 + scf_ref[0]) + shf_ref[0]).astype(bf)
    acc = jnp.zeros(x1.shape, jnp.float32)
    for c0 in range(0, FFN_HIDDEN, FFN_CHUNK):
        c1 = min(c0 + FFN_CHUNK, FFN_HIDDEN)
        gate = jnp.dot(h, wgu_ref[:, c0:c1], preferred_element_type=jnp.float32)
        up = jnp.dot(h, wgu_ref[:, FFN_HIDDEN + c0:FFN_HIDDEN + c1],
                     preferred_element_type=jnp.float32)
        act = (gate * (1.0 / (1.0 + jnp.exp(-gate))) * up).astype(bf)
        acc = acc + jnp.dot(act, wd_ref[c0:c1, :], preferred_element_type=jnp.float32)
    x2 = x1 + gtf_ref[0] * acc
    if final:
        x2 = _rms(x2, gfin_ref[...])
    o_ref[0] = x2


def _post(x, ya, yb, yc, gg, wo, gta, gf, scf, shf, gtf, wgu, wd, gfin, final):
    B, S, D = x.shape
    tm = TOK_TILE
    tok = lambda width: pl.BlockSpec((1, tm, width), lambda b, i: (b, i, 0))
    vec = pl.BlockSpec((1, 1, D), lambda b, i: (b, 0, 0))
    return pl.pallas_call(
        functools.partial(_post_kernel, final=final),
        out_shape=jax.ShapeDtypeStruct((B, S, D), jnp.float32),
        grid=(B, S // tm),
        in_specs=[tok(D), tok(NA_W), tok(SW_QW), tok(AX_QW), _const_spec((1, D)),
                  _const_spec(wo.shape), vec, _const_spec((1, D)), vec, vec, vec,
                  _const_spec(wgu.shape), _const_spec(wd.shape), _const_spec((1, D))],
        out_specs=tok(D),
        compiler_params=_cparams("arbitrary", "arbitrary"),
        name="out_proj_ffn",
    )(x, ya, yb, yc, gg, wo, gta, gf, scf, shf, gtf, wgu, wd, gfin)


def _rope_tables(S):
    t = jnp.arange(S)
    row = (t // GRID_W).astype(jnp.float32)
    col = (t % GRID_W).astype(jnp.float32)
    axis_dim = HEAD_DIM // 2
    freqs = ROPE_THETA ** (-jnp.arange(0, axis_dim, 2, dtype=jnp.float32) / axis_dim)
    ang = jnp.stack([row[:, None] * freqs, col[:, None] * freqs], axis=1)
    cos = jnp.cos(ang)
    sin = jnp.sin(ang)
    cos_h = jnp.concatenate([cos, cos], axis=-1).reshape(S, HEAD_DIM)
    sin_h = jnp.concatenate([-sin, sin], axis=-1).reshape(S, HEAD_DIM)
    return jnp.tile(cos_h, (1, LANES // HEAD_DIM)), jnp.tile(sin_h, (1, LANES // HEAD_DIM))


def _block_diag_ones(width):
    idx = np.arange(width) // HEAD_DIM
    return jnp.asarray(idx[:, None] == idx[None, :], dtype=jnp.bfloat16)


_SW_PERM = np.concatenate([np.arange(HEAD_DIM) + (g * (SW_HEADS // SW_KV_HEADS) + r) * HEAD_DIM
                           for r in range(SW_HEADS // SW_KV_HEADS) for g in range(SW_KV_HEADS)])


def kernel(x, c, w_mod, b_mod, g_attn, w_in, rpb_na, sink_sw, t5_table, gq_ax, gk_ax, g_group,
           w_o, g_ffn, w_gu, w_down, g_final):
    B, S, D = x.shape
    L = w_mod.shape[0]
    bf = jnp.bfloat16
    rows = S // GRID_W

    c_pad = jnp.pad(c, ((0, 8 - B), (0, 0)))
    mod = _modulation(c_pad, w_mod, b_mod)[:, :B]
    mod = mod.reshape(L, B, 6, 1, D)

    cos, sin = _rope_tables(S)
    bdq = _block_diag_ones(AX_QW)
    bdk = _block_diag_ones(AX_KW)
    sw_bias = _sw_bias_tile(t5_table)

    in_cols = np.arange(IN_WIDTH)
    in_cols[OFF_QB:OFF_QB + SW_QW] = OFF_QB + _SW_PERM
    group_perm = np.arange(D)
    group_perm[NA_W:NA_W + SW_QW] = NA_W + _SW_PERM

    for l in range(L):
        sh_a, sc_a, gt_a, sh_f, sc_f, gt_f = [mod[l, :, i] for i in range(6)]
        w_in_l = w_in[l][:, in_cols].astype(bf)
        qa, ka, va, qb, kb, vb, qct, kc, vct = _in_proj(
            x, sc_a, sh_a, g_attn[l].reshape(1, D), w_in_l, cos, sin,
            jnp.tile(gq_ax[l], LANES // HEAD_DIM).reshape(1, LANES),
            jnp.tile(gk_ax[l], LANES // HEAD_DIM).reshape(1, LANES), bdq, bdk)
        ya = _na_attention(qa, ka, va, _na_bias_tiles(rpb_na[l], rows))
        yb = _sw_attention(qb, kb, vb, sw_bias, sink_sw[l])
        yc = _ax_attention(qct, kc, vct)
        x = _post(x, ya, yb, yc, g_group[l][group_perm].reshape(1, D),
                  w_o[l][group_perm, :].astype(bf), gt_a, g_ffn[l].reshape(1, D), sc_f, sh_f, gt_f,
                  w_gu[l].astype(bf), w_down[l].astype(bf), g_final.reshape(1, D),
                  final=(l == L - 1))
    return x
```

```python
import functools
import math

import jax
import jax.numpy as jnp
import numpy as np
from jax import lax
from jax.experimental import pallas as pl
from jax.experimental.pallas import tpu as pltpu

D_MODEL = 1024
HEAD_DIM = 64
GRID_W = 64
NA_HEADS = 4
SW_HEADS = 6
SW_KV_HEADS = 2
AX_HEADS = 6
AX_KV_HEADS = 2
NA_WIN_ROWS = 8
NA_WIN_COLS = 16
SW_RADIUS = 128
SW_BLOCK = 128
T5_BUCKETS = 32
T5_MAX_DIST = 128
ROPE_THETA = 10000.0
FFN_HIDDEN = 2816
EPS = 1e-6
NEG_INF = -1e30

NA_W = NA_HEADS * HEAD_DIM
SW_QW = SW_HEADS * HEAD_DIM
SW_KW = SW_KV_HEADS * HEAD_DIM
AX_QW = AX_HEADS * HEAD_DIM
AX_KW = AX_KV_HEADS * HEAD_DIM
IN_WIDTH = 3 * NA_W + SW_QW + 2 * SW_KW + AX_QW + 2 * AX_KW
OFF_QA, OFF_KA, OFF_VA = 0, NA_W, 2 * NA_W
OFF_QB = 3 * NA_W
OFF_KB = OFF_QB + SW_QW
OFF_VB = OFF_KB + SW_KW
OFF_QC = OFF_VB + SW_KW
OFF_KC = OFF_QC + AX_QW
OFF_VC = OFF_KC + AX_KW

LANES = 128
VMEM_LIMIT = 56 * 1024 * 1024

TOK_TILE = 512
AX_TQ = 256
AX_TK = TOK_TILE
NA_QROWS = 4
NA_BAND = NA_QROWS + NA_WIN_ROWS
SW_TQ = 512
FFN_CHUNK = 512

QK_SCALE = HEAD_DIM ** -0.5
AX_Q_SCALE = QK_SCALE * math.log2(math.e)


def _cparams(*sem):
    return pltpu.CompilerParams(dimension_semantics=sem, vmem_limit_bytes=VMEM_LIMIT)


def _const_spec(shape):
    n = len(shape)
    return pl.BlockSpec(shape, lambda *_: (0,) * n, pipeline_mode=pl.Buffered(1))


def _rms(x, g):
    return x * lax.rsqrt(jnp.mean(x * x, axis=-1, keepdims=True) + EPS) * g


def _mod_kernel(c_ref, w_ref, b_ref, o_ref):
    c = c_ref[...]
    cond = c * (1.0 / (1.0 + jnp.exp(-c)))
    o_ref[0] = jnp.dot(cond, w_ref[0], preferred_element_type=jnp.float32,
                       precision=lax.Precision.HIGHEST) + b_ref[0]


def _modulation(c_pad, w_mod, b_mod):
    L, D, N = w_mod.shape
    tn = 1536
    return pl.pallas_call(
        _mod_kernel,
        out_shape=jax.ShapeDtypeStruct((L, c_pad.shape[0], N), jnp.float32),
        grid=(L, N // tn),
        in_specs=[pl.BlockSpec(c_pad.shape, lambda l, j: (0, 0)),
                  pl.BlockSpec((1, D, tn), lambda l, j: (l, 0, j)),
                  pl.BlockSpec((1, 1, tn), lambda l, j: (l, 0, j))],
        out_specs=pl.BlockSpec((1, c_pad.shape[0], tn), lambda l, j: (l, 0, j)),
        compiler_params=_cparams("arbitrary", "arbitrary"),
        name="adaln_mod",
    )(c_pad, w_mod, b_mod.reshape(L, 1, N))


def _head_sumsq(t, bd):
    t2 = t * t
    hi = t2.astype(jnp.bfloat16)
    lo = (t2 - hi.astype(jnp.float32)).astype(jnp.bfloat16)
    return (jnp.dot(hi, bd, preferred_element_type=jnp.float32)
            + jnp.dot(lo, bd, preferred_element_type=jnp.float32))


def _rope_chunk(t, cos, sin_signed, first_half):
    swapped = jnp.where(first_half, pltpu.roll(t, LANES - 16, 1), pltpu.roll(t, 16, 1))
    return t * cos + swapped * sin_signed


def _in_kernel(x_ref, sc_ref, sh_ref, g_ref, w_ref, cos_ref, sin_ref, gq_ref, gk_ref,
               bdq_ref, bdk_ref,
               qa_ref, ka_ref, va_ref, qb_ref, kb_ref, vb_ref, qct_ref, kc_ref, vct_ref):
    x = x_ref[0]
    h = _rms(x, g_ref[...]) * (1.0 + sc_ref[0]) + sh_ref[0]
    proj = jnp.dot(h.astype(jnp.bfloat16), w_ref[...], preferred_element_type=jnp.float32)

    bf = jnp.bfloat16
    qa_ref[0] = (proj[:, OFF_QA:OFF_QA + NA_W] * QK_SCALE).astype(bf)
    ka_ref[0] = proj[:, OFF_KA:OFF_KA + NA_W].astype(bf)
    va_ref[0] = proj[:, OFF_VA:OFF_VA + NA_W].astype(bf)
    qb_ref[0] = (proj[:, OFF_QB:OFF_QB + SW_QW] * QK_SCALE).astype(bf)
    kb_ref[0] = proj[:, OFF_KB:OFF_KB + SW_KW].astype(bf)
    vb_ref[0] = proj[:, OFF_VB:OFF_VB + SW_KW].astype(bf)

    cos = cos_ref[...]
    sin = sin_ref[...]
    lane = lax.broadcasted_iota(jnp.int32, cos.shape, 1)
    first_half = (lane % 32) < 16

    qc = proj[:, OFF_QC:OFF_QC + AX_QW]
    qn = qc * lax.rsqrt(_head_sumsq(qc, bdq_ref[...]) * (1.0 / HEAD_DIM) + EPS)
    chunks = []
    for j in range(AX_QW // LANES):
        t = qn[:, j * LANES:(j + 1) * LANES] * gq_ref[...]
        chunks.append(_rope_chunk(t, cos, sin, first_half) * AX_Q_SCALE)
    qct_ref[0] = jnp.concatenate(chunks, axis=1).T.astype(bf)

    kc = proj[:, OFF_KC:OFF_KC + AX_KW]
    kn = kc * lax.rsqrt(_head_sumsq(kc, bdk_ref[...]) * (1.0 / HEAD_DIM) + EPS) * gk_ref[...]
    kc_ref[0] = _rope_chunk(kn, cos, sin, first_half).astype(bf)

    vct_ref[0, 0] = proj[:, OFF_VC:OFF_VC + AX_KW].T.astype(bf)


def _in_proj(x, sc, sh, g, w, cos, sin, gq, gk, bdq, bdk):
    B, S, D = x.shape
    tm = TOK_TILE
    bf = jnp.bfloat16
    tok = lambda width: pl.BlockSpec((1, tm, width), lambda b, i: (b, i, 0))
    vec = pl.BlockSpec((1, 1, D), lambda b, i: (b, 0, 0))
    out_shape = (
        jax.ShapeDtypeStruct((B, S, NA_W), bf), jax.ShapeDtypeStruct((B, S, NA_W), bf),
        jax.ShapeDtypeStruct((B, S, NA_W), bf),
        jax.ShapeDtypeStruct((B, S, SW_QW), bf), jax.ShapeDtypeStruct((B, S, SW_KW), bf),
        jax.ShapeDtypeStruct((B, S, SW_KW), bf),
        jax.ShapeDtypeStruct((B, AX_QW, S), bf), jax.ShapeDtypeStruct((B, S, AX_KW), bf),
        jax.ShapeDtypeStruct((B, S // tm, AX_KW, tm), bf),
    )
    out_specs = (
        tok(NA_W), tok(NA_W), tok(NA_W), tok(SW_QW), tok(SW_KW), tok(SW_KW),
        pl.BlockSpec((1, AX_QW, tm), lambda b, i: (b, 0, i)),
        tok(AX_KW),
        pl.BlockSpec((1, 1, AX_KW, tm), lambda b, i: (b, i, 0, 0)),
    )
    return pl.pallas_call(
        _in_kernel,
        out_shape=out_shape,
        grid=(B, S // tm),
        in_specs=[tok(D), vec, vec, _const_spec((1, D)), _const_spec(w.shape),
                  pl.BlockSpec((tm, LANES), lambda b, i: (i, 0)),
                  pl.BlockSpec((tm, LANES), lambda b, i: (i, 0)),
                  _const_spec((1, LANES)), _const_spec((1, LANES)),
                  _const_spec(bdq.shape), _const_spec(bdk.shape)],
        out_specs=out_specs,
        compiler_params=_cparams("arbitrary", "arbitrary"),
        name="in_proj",
    )(x, sc, sh, g, w, cos, sin, gq, gk, bdq, bdk)


def _na_build_bias(rp_ref, bias_ref, rows):
    shape = (GRID_W, LANES)
    c = lax.broadcasted_iota(jnp.int32, shape, 0)
    lane = lax.broadcasted_iota(jnp.int32, shape, 1)
    kc = lane % GRID_W
    cs = jnp.clip(c - NA_WIN_COLS // 2, 0, GRID_W - NA_WIN_COLS)
    col_ok = (kc >= cs) & (kc < cs + NA_WIN_COLS)
    left = lane < GRID_W
    neg = jnp.full(shape, NEG_INF, jnp.float32)
    n_off = 2 * NA_WIN_ROWS - 1
    for h in range(NA_HEADS):
        pair = []
        for a in range(-1, n_off):
            x = jnp.broadcast_to(rp_ref[h, a + 1:a + 2, :], shape)
            t = pltpu.roll(x, LANES - (NA_WIN_COLS - 1), 1, stride=1, stride_axis=0)
            pair.append(jnp.where(col_ok, t, neg))
        for variant, qr0 in enumerate((0, 2 * NA_QROWS, rows - NA_QROWS)):
            bs = int(np.clip(qr0 - NA_WIN_ROWS // 2, 0, rows - NA_BAND))
            for ri in range(NA_QROWS):
                r = qr0 + ri
                rs = int(np.clip(r - NA_WIN_ROWS // 2, 0, rows - NA_WIN_ROWS))
                for vc in range(NA_BAND // 2):
                    kr0 = bs + 2 * vc
                    ok0 = rs <= kr0 < rs + NA_WIN_ROWS
                    ok1 = rs <= kr0 + 1 < rs + NA_WIN_ROWS
                    a0 = kr0 - r + NA_WIN_ROWS - 1
                    if ok0 and ok1:
                        tile = pair[a0 + 1]
                    elif ok0:
                        tile = jnp.where(left, pair[a0 + 1], neg)
                    elif ok1:
                        tile = jnp.where(left, neg, pair[a0 + 1])
                    else:
                        tile = neg
                    bias_ref[variant, h, ri * GRID_W:(ri + 1) * GRID_W,
                             vc * LANES:(vc + 1) * LANES] = tile


def _na_kernel(q_ref, k_ref, v_ref, rp_ref, o_ref, bias_ref, *, rows):
    j = pl.program_id(1)
    nblk = pl.num_programs(1)

    @pl.when((pl.program_id(0) == 0) & (j == 0))
    def _():
        _na_build_bias(rp_ref, bias_ref, rows)

    variant = jnp.where(j == 0, 0, jnp.where(j == nblk - 1, 2, 1))
    band_row = jnp.clip(j * NA_QROWS - NA_WIN_ROWS // 2, 0, rows - NA_BAND)
    start = pl.multiple_of(band_row * GRID_W, GRID_W)
    kb = k_ref[0, pl.ds(start, NA_BAND * GRID_W), :]
    vb = v_ref[0, pl.ds(start, NA_BAND * GRID_W), :]
    q = q_ref[0]
    head_of_lane = lax.broadcasted_iota(jnp.int32, q.shape, 1) // HEAD_DIM
    out = jnp.zeros(q.shape, jnp.float32)
    for h in range(NA_HEADS):
        mine = head_of_lane == h
        qh = jnp.where(mine, q, jnp.zeros_like(q))
        s = lax.dot_general(qh, kb, (((1,), (1,)), ((), ())),
                            preferred_element_type=jnp.float32) + bias_ref[variant, h]
        m = jnp.max(s, axis=-1, keepdims=True)
        p = jnp.exp(s - m)
        l = jnp.sum(p, axis=-1, keepdims=True)
        o = jnp.dot(p.astype(jnp.bfloat16), vb, preferred_element_type=jnp.float32)
        out = jnp.where(mine, o / l, out)
    o_ref[0] = out


def _na_attention(q, k, v, rp):
    B, S, W = q.shape
    rows = S // GRID_W
    tq = NA_QROWS * GRID_W
    return pl.pallas_call(
        functools.partial(_na_kernel, rows=rows),
        out_shape=jax.ShapeDtypeStruct((B, S, W), jnp.float32),
        grid=(B, S // tq),
        in_specs=[pl.BlockSpec((1, tq, W), lambda b, j: (b, j, 0)),
                  pl.BlockSpec((1, S, W), lambda b, j: (b, 0, 0), pipeline_mode=pl.Buffered(1)),
                  pl.BlockSpec((1, S, W), lambda b, j: (b, 0, 0), pipeline_mode=pl.Buffered(1)),
                  _const_spec(rp.shape)],
        out_specs=pl.BlockSpec((1, tq, W), lambda b, j: (b, j, 0)),
        scratch_shapes=[pltpu.VMEM((3, NA_HEADS, tq, NA_BAND * GRID_W), jnp.float32)],
        compiler_params=_cparams("arbitrary", "arbitrary"),
        name="na_attn",
    )(q, k, v, rp)


def _na_rpb_rows(rpb):
    n_col = rpb.shape[-1]
    p = jnp.pad(rpb.astype(jnp.float32), ((0, 0), (1, 1), (0, GRID_W - n_col)))
    return jnp.concatenate([p[:, :-1], p[:, 1:]], axis=-1)


def _sw_kernel(sink_ref, t5_ref, q_ref, k_ref, v_ref, bucket_ref, o_ref, bias_ref, *, seq):
    i = pl.program_id(1)
    nblk_total = seq // SW_BLOCK

    @pl.when((pl.program_id(0) == 0) & (i == 0))
    def _():
        bucket = bucket_ref[...]
        for h in range(SW_HEADS):
            acc = jnp.full(bucket.shape, NEG_INF, jnp.float32)
            for b in range(T5_BUCKETS):
                acc = jnp.where(bucket == b, t5_ref[b * SW_HEADS + h], acc)
            bias_ref[h] = acc

    col = lax.broadcasted_iota(jnp.int32, (SW_BLOCK, 3 * SW_BLOCK), 1)
    lane_group = lax.broadcasted_iota(jnp.int32, (SW_BLOCK, LANES), 1) // HEAD_DIM
    for blk in range(SW_TQ // SW_BLOCK):
        n = i * (SW_TQ // SW_BLOCK) + blk
        q0 = pl.multiple_of(n * SW_BLOCK, SW_BLOCK)
        left = pl.multiple_of(jnp.maximum(q0 - SW_BLOCK, 0), SW_BLOCK)
        right = pl.multiple_of(jnp.minimum(q0 + SW_BLOCK, seq - SW_BLOCK), SW_BLOCK)
        kw = jnp.concatenate([k_ref[0, pl.ds(left, SW_BLOCK), :], k_ref[0, pl.ds(q0, SW_BLOCK), :],
                              k_ref[0, pl.ds(right, SW_BLOCK), :]], axis=0)
        vw = jnp.concatenate([v_ref[0, pl.ds(left, SW_BLOCK), :], v_ref[0, pl.ds(q0, SW_BLOCK), :],
                              v_ref[0, pl.ds(right, SW_BLOCK), :]], axis=0)
        outside = ((n == 0) & (col < SW_BLOCK)) | ((n == nblk_total - 1) & (col >= 2 * SW_BLOCK))
        edge = jnp.where(outside, NEG_INF, 0.0)
        for r in range(SW_QW // LANES):
            qcol = q_ref[0, blk * SW_BLOCK:(blk + 1) * SW_BLOCK, r * LANES:(r + 1) * LANES]
            out = jnp.zeros((SW_BLOCK, LANES), jnp.float32)
            for g in range(SW_KV_HEADS):
                h = g * (SW_HEADS // SW_KV_HEADS) + r
                mine = lane_group == g
                qh = jnp.where(mine, qcol, jnp.zeros_like(qcol))
                s = lax.dot_general(qh, kw, (((1,), (1,)), ((), ())),
                                    preferred_element_type=jnp.float32)
                s = s + bias_ref[h] + edge
                sink = sink_ref[h]
                m = jnp.maximum(jnp.max(s, axis=-1, keepdims=True), sink)
                p = jnp.exp(s - m)
                l = jnp.sum(p, axis=-1, keepdims=True) + jnp.exp(sink - m)
                o = jnp.dot(p.astype(jnp.bfloat16), vw, preferred_element_type=jnp.float32)
                out = jnp.where(mine, o / l, out)
            o_ref[0, blk * SW_BLOCK:(blk + 1) * SW_BLOCK, r * LANES:(r + 1) * LANES] = out


def _sw_attention(q, k, v, bucket, t5_flat, sink):
    B, S, _ = q.shape
    return pl.pallas_call(
        functools.partial(_sw_kernel, seq=S),
        out_shape=jax.ShapeDtypeStruct((B, S, SW_QW), jnp.float32),
        grid_spec=pltpu.PrefetchScalarGridSpec(
            num_scalar_prefetch=2,
            grid=(B, S // SW_TQ),
            in_specs=[pl.BlockSpec((1, SW_TQ, SW_QW), lambda b, i, *_: (b, i, 0)),
                      pl.BlockSpec((1, S, SW_KW), lambda b, i, *_: (b, 0, 0),
                                   pipeline_mode=pl.Buffered(1)),
                      pl.BlockSpec((1, S, SW_KW), lambda b, i, *_: (b, 0, 0),
                                   pipeline_mode=pl.Buffered(1)),
                      pl.BlockSpec(bucket.shape, lambda b, i, *_: (0, 0),
                                   pipeline_mode=pl.Buffered(1))],
            out_specs=pl.BlockSpec((1, SW_TQ, SW_QW), lambda b, i, *_: (b, i, 0)),
            scratch_shapes=[pltpu.VMEM((SW_HEADS, SW_BLOCK, 3 * SW_BLOCK), jnp.float32)],
        ),
        compiler_params=_cparams("arbitrary", "arbitrary"),
        name="sw_attn",
    )(sink, t5_flat, q, k, v, bucket)


def _t5_bucket(rel):
    nb = T5_BUCKETS // 2
    ret = (rel > 0).astype(jnp.int32) * nb
    n = jnp.abs(rel)
    max_exact = nb // 2
    nf = jnp.maximum(n, max_exact).astype(jnp.float32)
    large = max_exact + (jnp.log(nf / max_exact) / math.log(T5_MAX_DIST / max_exact)
                         * (nb - max_exact)).astype(jnp.int32)
    large = jnp.minimum(large, nb - 1)
    return ret + jnp.where(n < max_exact, n, large)


def _sw_bucket_tile():
    qpos = jnp.arange(SW_BLOCK)
    kpos = jnp.arange(3 * SW_BLOCK) - SW_BLOCK
    rel = kpos[None, :] - qpos[:, None]
    return jnp.where(jnp.abs(rel) <= SW_RADIUS, _t5_bucket(rel), -1).astype(jnp.int32)


def _ax_kernel(qt_ref, k_ref, vt_ref, o_ref, qpad_ref, m_ref, l_ref, acc_ref, *, nkt):
    tq = qt_ref.shape[2]
    rep = AX_HEADS // AX_KV_HEADS
    for h in range(AX_HEADS):
        qth = qt_ref[0, h * HEAD_DIM:(h + 1) * HEAD_DIM, :]
        zero = jnp.zeros_like(qth)
        qpad_ref[h] = jnp.concatenate([qth, zero] if h // rep == 0 else [zero, qth], axis=0)
    m_ref[...] = jnp.full(m_ref.shape, -jnp.inf, jnp.float32)
    l_ref[...] = jnp.zeros(l_ref.shape, jnp.float32)
    acc_ref[...] = jnp.zeros(acc_ref.shape, jnp.float32)

    def body(kt, carry):
        ks = pl.multiple_of(kt * AX_TK, AX_TK)
        k_rows = k_ref[0, pl.ds(ks, AX_TK), :]
        score = lambda h: jnp.dot(k_rows, qpad_ref[h], preferred_element_type=jnp.float32)
        s_next = score(0)
        for h in range(AX_HEADS):
            g = h // rep
            s = s_next
            if h + 1 < AX_HEADS:
                s_next = score(h + 1)
            m_old = m_ref[h]
            m_new = jnp.maximum(m_old, jnp.max(s, axis=0, keepdims=True))
            alpha = jnp.exp2(m_old - m_new)
            p = jnp.exp2(s - m_new)
            l_ref[h] = alpha * l_ref[h] + jnp.sum(p, axis=0, keepdims=True)
            m_ref[h] = m_new
            vt = vt_ref[0, kt, g * HEAD_DIM:(g + 1) * HEAD_DIM, :]
            pv = jnp.dot(vt, p.astype(jnp.bfloat16), preferred_element_type=jnp.float32)
            acc_ref[h] = alpha * acc_ref[h] + pv
        return carry

    lax.fori_loop(0, nkt, body, 0)
    out_t = jnp.concatenate([acc_ref[h] / l_ref[h] for h in range(AX_HEADS)], axis=0)
    o_ref[0] = out_t.T


def _ax_attention(qt, k, vt):
    B, W, S = qt.shape
    nkt = vt.shape[1]
    return pl.pallas_call(
        functools.partial(_ax_kernel, nkt=nkt),
        out_shape=jax.ShapeDtypeStruct((B, S, W), jnp.float32),
        grid=(B, S // AX_TQ),
        in_specs=[pl.BlockSpec((1, W, AX_TQ), lambda b, i: (b, 0, i)),
                  pl.BlockSpec((1, S, AX_KW), lambda b, i: (b, 0, 0), pipeline_mode=pl.Buffered(1)),
                  pl.BlockSpec((1, nkt, AX_KW, AX_TK), lambda b, i: (b, 0, 0, 0),
                               pipeline_mode=pl.Buffered(1))],
        out_specs=pl.BlockSpec((1, AX_TQ, W), lambda b, i: (b, i, 0)),
        scratch_shapes=[pltpu.VMEM((AX_HEADS, AX_KW, AX_TQ), jnp.bfloat16),
                        pltpu.VMEM((AX_HEADS, 1, AX_TQ), jnp.float32),
                        pltpu.VMEM((AX_HEADS, 1, AX_TQ), jnp.float32),
                        pltpu.VMEM((AX_HEADS, HEAD_DIM, AX_TQ), jnp.float32)],
        compiler_params=_cparams("arbitrary", "arbitrary"),
        name="ax_attn",
    )(qt, k, vt)


def _post_kernel(x_ref, ya_ref, yb_ref, yc_ref, gg_ref, wo_ref, gta_ref, gf_ref, scf_ref, shf_ref,
                 gtf_ref, wgu_ref, wd_ref, gfin_ref, o_ref, *, final):
    bf = jnp.bfloat16
    gg = gg_ref[...]
    y = jnp.concatenate([
        _rms(ya_ref[0], gg[:, 0:NA_W]).astype(bf),
        _rms(yb_ref[0], gg[:, NA_W:NA_W + SW_QW]).astype(bf),
        _rms(yc_ref[0], gg[:, NA_W + SW_QW:]).astype(bf)], axis=1)
    x1 = x_ref[0] + gta_ref[0] * jnp.dot(y, wo_ref[...], preferred_element_type=jnp.float32)

    h = (_rms(x1, gf_ref[...]) * (1.0 + scf_ref[0]) + shf_ref[0]).astype(bf)
    acc = jnp.zeros(x1.shape, jnp.float32)
    for c0 in range(0, FFN_HIDDEN, FFN_CHUNK):
        c1 = min(c0 + FFN_CHUNK, FFN_HIDDEN)
        gate = jnp.dot(h, wgu_ref[:, c0:c1], preferred_element_type=jnp.float32)
        up = jnp.dot(h, wgu_ref[:, FFN_HIDDEN + c0:FFN_HIDDEN + c1],
                     preferred_element_type=jnp.float32)
        act = (gate * (1.0 / (1.0 + jnp.exp(-gate))) * up).astype(bf)
        acc = acc + jnp.dot(act, wd_ref[c0:c1, :], preferred_element_type=jnp.float32)
    x2 = x1 + gtf_ref[0] * acc
    if final:
        x2 = _rms(x2, gfin_ref[...])
    o_ref[0] = x2


def _post(x, ya, yb, yc, gg, wo, gta, gf, scf, shf, gtf, wgu, wd, gfin, final):
    B, S, D = x.shape
    tm = TOK_TILE
    tok = lambda width: pl.BlockSpec((1, tm, width), lambda b, i: (b, i, 0))
    vec = pl.BlockSpec((1, 1, D), lambda b, i: (b, 0, 0))
    return pl.pallas_call(
        functools.partial(_post_kernel, final=final),
        out_shape=jax.ShapeDtypeStruct((B, S, D), jnp.float32),
        grid=(B, S // tm),
        in_specs=[tok(D), tok(NA_W), tok(SW_QW), tok(AX_QW), _const_spec((1, D)),
                  _const_spec(wo.shape), vec, _const_spec((1, D)), vec, vec, vec,
                  _const_spec(wgu.shape), _const_spec(wd.shape), _const_spec((1, D))],
        out_specs=tok(D),
        compiler_params=_cparams("arbitrary", "arbitrary"),
        name="out_proj_ffn",
    )(x, ya, yb, yc, gg, wo, gta, gf, scf, shf, gtf, wgu, wd, gfin)


def _rope_tables(S):
    t = jnp.arange(S)
    row = (t // GRID_W).astype(jnp.float32)
    col = (t % GRID_W).astype(jnp.float32)
    axis_dim = HEAD_DIM // 2
    freqs = ROPE_THETA ** (-jnp.arange(0, axis_dim, 2, dtype=jnp.float32) / axis_dim)
    ang = jnp.stack([row[:, None] * freqs, col[:, None] * freqs], axis=1)
    cos = jnp.cos(ang)
    sin = jnp.sin(ang)
    cos_h = jnp.concatenate([cos, cos], axis=-1).reshape(S, HEAD_DIM)
    sin_h = jnp.concatenate([-sin, sin], axis=-1).reshape(S, HEAD_DIM)
    return jnp.tile(cos_h, (1, LANES // HEAD_DIM)), jnp.tile(sin_h, (1, LANES // HEAD_DIM))


def _block_diag_ones(width):
    idx = np.arange(width) // HEAD_DIM
    return jnp.asarray(idx[:, None] == idx[None, :], dtype=jnp.bfloat16)


_SW_PERM = np.concatenate([np.arange(HEAD_DIM) + (g * (SW_HEADS // SW_KV_HEADS) + r) * HEAD_DIM
                           for r in range(SW_HEADS // SW_KV_HEADS) for g in range(SW_KV_HEADS)])


def kernel(x, c, w_mod, b_mod, g_attn, w_in, rpb_na, sink_sw, t5_table, gq_ax, gk_ax, g_group,
           w_o, g_ffn, w_gu, w_down, g_final):
    B, S, D = x.shape
    L = w_mod.shape[0]
    bf = jnp.bfloat16
    rows = S // GRID_W

    c_pad = jnp.pad(c, ((0, 8 - B), (0, 0)))
    mod = _modulation(c_pad, w_mod, b_mod)[:, :B]
    mod = mod.reshape(L, B, 6, 1, D)

    cos, sin = _rope_tables(S)
    bdq = _block_diag_ones(AX_QW)
    bdk = _block_diag_ones(AX_KW)
    sw_bucket = _sw_bucket_tile()
    t5_flat = t5_table.astype(jnp.float32).reshape(-1)

    in_cols = np.arange(IN_WIDTH)
    in_cols[OFF_QB:OFF_QB + SW_QW] = OFF_QB + _SW_PERM
    group_perm = np.arange(D)
    group_perm[NA_W:NA_W + SW_QW] = NA_W + _SW_PERM

    for l in range(L):
        sh_a, sc_a, gt_a, sh_f, sc_f, gt_f = [mod[l, :, i] for i in range(6)]
        w_in_l = w_in[l][:, in_cols].astype(bf)
        qa, ka, va, qb, kb, vb, qct, kc, vct = _in_proj(
            x, sc_a, sh_a, g_attn[l].reshape(1, D), w_in_l, cos, sin,
            jnp.tile(gq_ax[l], LANES // HEAD_DIM).reshape(1, LANES),
            jnp.tile(gk_ax[l], LANES // HEAD_DIM).reshape(1, LANES), bdq, bdk)
        ya = _na_attention(qa, ka, va, _na_rpb_rows(rpb_na[l]))
        yb = _sw_attention(qb, kb, vb, sw_bucket, t5_flat, sink_sw[l])
        yc = _ax_attention(qct, kc, vct)
        x = _post(x, ya, yb, yc, g_group[l][group_perm].reshape(1, D),
                  w_o[l][group_perm, :].astype(bf), gt_a, g_ffn[l].reshape(1, D), sc_f, sh_f, gt_f,
                  w_gu[l].astype(bf), w_down[l].astype(bf), g_final.reshape(1, D),
                  final=(l == L - 1))
    return x
```

```python
import functools
import math

import jax
import jax.numpy as jnp
import numpy as np
from jax import lax
from jax.experimental import pallas as pl
from jax.experimental.pallas import tpu as pltpu

D_MODEL = 1024
HEAD_DIM = 64
GRID_W = 64
NA_HEADS = 4
SW_HEADS = 6
SW_KV_HEADS = 2
AX_HEADS = 6
AX_KV_HEADS = 2
NA_WIN_ROWS = 8
NA_WIN_COLS = 16
SW_RADIUS = 128
SW_BLOCK = 128
T5_BUCKETS = 32
T5_MAX_DIST = 128
ROPE_THETA = 10000.0
FFN_HIDDEN = 2816
EPS = 1e-6
NEG_INF = -1e30

NA_W = NA_HEADS * HEAD_DIM
SW_QW = SW_HEADS * HEAD_DIM
SW_KW = SW_KV_HEADS * HEAD_DIM
AX_QW = AX_HEADS * HEAD_DIM
AX_KW = AX_KV_HEADS * HEAD_DIM
IN_WIDTH = 3 * NA_W + SW_QW + 2 * SW_KW + AX_QW + 2 * AX_KW
OFF_QA, OFF_KA, OFF_VA = 0, NA_W, 2 * NA_W
OFF_QB = 3 * NA_W
OFF_KB = OFF_QB + SW_QW
OFF_VB = OFF_KB + SW_KW
OFF_QC = OFF_VB + SW_KW
OFF_KC = OFF_QC + AX_QW
OFF_VC = OFF_KC + AX_KW

LANES = 128
VMEM_LIMIT = 56 * 1024 * 1024

TOK_TILE = 512
AX_TQ = 256
AX_TK = TOK_TILE
AX_LOOKAHEAD = 2
AX_UNROLL = 2
NA_QROWS = 4
NA_BAND = NA_QROWS + NA_WIN_ROWS
SW_TQ = 512
FFN_CHUNK = 512

QK_SCALE = HEAD_DIM ** -0.5
AX_Q_SCALE = QK_SCALE * math.log2(math.e)
AX_MAX_SHIFT = 60.0


def _cparams(*sem):
    return pltpu.CompilerParams(dimension_semantics=sem, vmem_limit_bytes=VMEM_LIMIT)


def _const_spec(shape):
    n = len(shape)
    return pl.BlockSpec(shape, lambda *_: (0,) * n, pipeline_mode=pl.Buffered(1))


def _rms(x, g):
    return x * lax.rsqrt(jnp.mean(x * x, axis=-1, keepdims=True) + EPS) * g


def _mod_kernel(c_ref, w_ref, b_ref, o_ref):
    c = c_ref[...]
    cond = c * (1.0 / (1.0 + jnp.exp(-c)))
    o_ref[0] = jnp.dot(cond, w_ref[0], preferred_element_type=jnp.float32,
                       precision=lax.Precision.HIGHEST) + b_ref[0]


def _modulation(c_pad, w_mod, b_mod):
    L, D, N = w_mod.shape
    tn = 1536
    return pl.pallas_call(
        _mod_kernel,
        out_shape=jax.ShapeDtypeStruct((L, c_pad.shape[0], N), jnp.float32),
        grid=(L, N // tn),
        in_specs=[pl.BlockSpec(c_pad.shape, lambda l, j: (0, 0)),
                  pl.BlockSpec((1, D, tn), lambda l, j: (l, 0, j)),
                  pl.BlockSpec((1, 1, tn), lambda l, j: (l, 0, j))],
        out_specs=pl.BlockSpec((1, c_pad.shape[0], tn), lambda l, j: (l, 0, j)),
        compiler_params=_cparams("arbitrary", "arbitrary"),
        name="adaln_mod",
    )(c_pad, w_mod, b_mod.reshape(L, 1, N))


def _head_sumsq(t, bd):
    t2 = t * t
    hi = t2.astype(jnp.bfloat16)
    lo = (t2 - hi.astype(jnp.float32)).astype(jnp.bfloat16)
    return (jnp.dot(hi, bd, preferred_element_type=jnp.float32)
            + jnp.dot(lo, bd, preferred_element_type=jnp.float32))


def _rope_chunk(t, cos, sin_signed, first_half):
    swapped = jnp.where(first_half, pltpu.roll(t, LANES - 16, 1), pltpu.roll(t, 16, 1))
    return t * cos + swapped * sin_signed


def _in_kernel(x_ref, sc_ref, sh_ref, g_ref, w_ref, cos_ref, sin_ref, gq_ref, gk_ref,
               bdq_ref, bdk_ref,
               qa_ref, ka_ref, va_ref, qb_ref, kb_ref, vb_ref, qct_ref, kc_ref, vct_ref,
               qnorm_ref, kmax_ref):
    x = x_ref[0]
    h = _rms(x, g_ref[...]) * (1.0 + sc_ref[0]) + sh_ref[0]
    proj = jnp.dot(h.astype(jnp.bfloat16), w_ref[...], preferred_element_type=jnp.float32)

    bf = jnp.bfloat16
    qa_ref[0] = (proj[:, OFF_QA:OFF_QA + NA_W] * QK_SCALE).astype(bf)
    ka_ref[0] = proj[:, OFF_KA:OFF_KA + NA_W].astype(bf)
    va_ref[0] = proj[:, OFF_VA:OFF_VA + NA_W].astype(bf)
    qb_ref[0] = (proj[:, OFF_QB:OFF_QB + SW_QW] * QK_SCALE).astype(bf)
    kb_ref[0] = proj[:, OFF_KB:OFF_KB + SW_KW].astype(bf)
    vb_ref[0] = proj[:, OFF_VB:OFF_VB + SW_KW].astype(bf)

    cos = cos_ref[...]
    sin = sin_ref[...]
    lane = lax.broadcasted_iota(jnp.int32, cos.shape, 1)
    first_half = (lane % 32) < 16

    qc = proj[:, OFF_QC:OFF_QC + AX_QW]
    qn = qc * lax.rsqrt(_head_sumsq(qc, bdq_ref[...]) * (1.0 / HEAD_DIM) + EPS)
    chunks = []
    for j in range(AX_QW // LANES):
        t = qn[:, j * LANES:(j + 1) * LANES] * gq_ref[...]
        chunks.append(_rope_chunk(t, cos, sin, first_half) * AX_Q_SCALE)
    qt = jnp.concatenate(chunks, axis=1).T.astype(bf)
    qct_ref[0] = qt
    qsq = qt.astype(jnp.float32)
    qsq = qsq * qsq
    norms = [jnp.sqrt(jnp.sum(qsq[hh * HEAD_DIM:(hh + 1) * HEAD_DIM], axis=0, keepdims=True))
             for hh in range(AX_HEADS)]
    qnorm_ref[0] = jnp.concatenate(norms + [jnp.zeros_like(norms[0])] * (8 - AX_HEADS), axis=0)

    kc = proj[:, OFF_KC:OFF_KC + AX_KW]
    kn = kc * lax.rsqrt(_head_sumsq(kc, bdk_ref[...]) * (1.0 / HEAD_DIM) + EPS) * gk_ref[...]
    kr = _rope_chunk(kn, cos, sin, first_half)
    kb = kr.astype(bf)
    ones_lane = jnp.where(lane == HEAD_DIM, 1.0, 0.0).astype(bf)
    kc_ref[0, 0] = jnp.where(lane < HEAD_DIM, kb, ones_lane)
    kc_ref[0, 1] = jnp.where(lane < HEAD_DIM, pltpu.roll(kr, HEAD_DIM, 1).astype(bf), ones_lane)
    kf = kb.astype(jnp.float32)
    tile_max = jnp.max(_head_sumsq(kf, bdk_ref[...]), axis=0, keepdims=True)
    first = pl.program_id(1) == 0

    @pl.when(first)
    def _():
        kmax_ref[0] = tile_max

    @pl.when(jnp.logical_not(first))
    def _():
        kmax_ref[0] = jnp.maximum(kmax_ref[0], tile_max)

    vct_ref[0, 0] = proj[:, OFF_VC:OFF_VC + AX_KW].T.astype(bf)


def _in_proj(x, sc, sh, g, w, cos, sin, gq, gk, bdq, bdk):
    B, S, D = x.shape
    tm = TOK_TILE
    bf = jnp.bfloat16
    tok = lambda width: pl.BlockSpec((1, tm, width), lambda b, i: (b, i, 0))
    vec = pl.BlockSpec((1, 1, D), lambda b, i: (b, 0, 0))
    out_shape = (
        jax.ShapeDtypeStruct((B, S, NA_W), bf), jax.ShapeDtypeStruct((B, S, NA_W), bf),
        jax.ShapeDtypeStruct((B, S, NA_W), bf),
        jax.ShapeDtypeStruct((B, S, SW_QW), bf), jax.ShapeDtypeStruct((B, S, SW_KW), bf),
        jax.ShapeDtypeStruct((B, S, SW_KW), bf),
        jax.ShapeDtypeStruct((B, AX_QW, S), bf),
        jax.ShapeDtypeStruct((B, AX_KV_HEADS, S, LANES), bf),
        jax.ShapeDtypeStruct((B, S // tm, AX_KW, tm), bf),
        jax.ShapeDtypeStruct((B, 8, S), jnp.float32),
        jax.ShapeDtypeStruct((B, 1, AX_KW), jnp.float32),
    )
    out_specs = (
        tok(NA_W), tok(NA_W), tok(NA_W), tok(SW_QW), tok(SW_KW), tok(SW_KW),
        pl.BlockSpec((1, AX_QW, tm), lambda b, i: (b, 0, i)),
        pl.BlockSpec((1, AX_KV_HEADS, tm, LANES), lambda b, i: (b, 0, i, 0)),
        pl.BlockSpec((1, 1, AX_KW, tm), lambda b, i: (b, i, 0, 0)),
        pl.BlockSpec((1, 8, tm), lambda b, i: (b, 0, i)),
        pl.BlockSpec((1, 1, AX_KW), lambda b, i: (b, 0, 0)),
    )
    return pl.pallas_call(
        _in_kernel,
        out_shape=out_shape,
        grid=(B, S // tm),
        in_specs=[tok(D), vec, vec, _const_spec((1, D)), _const_spec(w.shape),
                  pl.BlockSpec((tm, LANES), lambda b, i: (i, 0)),
                  pl.BlockSpec((tm, LANES), lambda b, i: (i, 0)),
                  _const_spec((1, LANES)), _const_spec((1, LANES)),
                  _const_spec(bdq.shape), _const_spec(bdk.shape)],
        out_specs=out_specs,
        compiler_params=_cparams("arbitrary", "arbitrary"),
        name="in_proj",
    )(x, sc, sh, g, w, cos, sin, gq, gk, bdq, bdk)


def _na_build_bias(rp_ref, bias_ref, rows):
    shape = (GRID_W, LANES)
    c = lax.broadcasted_iota(jnp.int32, shape, 0)
    lane = lax.broadcasted_iota(jnp.int32, shape, 1)
    kc = lane % GRID_W
    cs = jnp.clip(c - NA_WIN_COLS // 2, 0, GRID_W - NA_WIN_COLS)
    col_ok = (kc >= cs) & (kc < cs + NA_WIN_COLS)
    left = lane < GRID_W
    neg = jnp.full(shape, NEG_INF, jnp.float32)
    n_off = 2 * NA_WIN_ROWS - 1
    for h in range(NA_HEADS):
        pair = []
        for a in range(-1, n_off):
            x = jnp.broadcast_to(rp_ref[h, a + 1:a + 2, :], shape)
            t = pltpu.roll(x, LANES - (NA_WIN_COLS - 1), 1, stride=1, stride_axis=0)
            pair.append(jnp.where(col_ok, t, neg))
        for variant, qr0 in enumerate((0, 2 * NA_QROWS, rows - NA_QROWS)):
            bs = int(np.clip(qr0 - NA_WIN_ROWS // 2, 0, rows - NA_BAND))
            for ri in range(NA_QROWS):
                r = qr0 + ri
                rs = int(np.clip(r - NA_WIN_ROWS // 2, 0, rows - NA_WIN_ROWS))
                for vc in range(NA_BAND // 2):
                    kr0 = bs + 2 * vc
                    ok0 = rs <= kr0 < rs + NA_WIN_ROWS
                    ok1 = rs <= kr0 + 1 < rs + NA_WIN_ROWS
                    a0 = kr0 - r + NA_WIN_ROWS - 1
                    if ok0 and ok1:
                        tile = pair[a0 + 1]
                    elif ok0:
                        tile = jnp.where(left, pair[a0 + 1], neg)
                    elif ok1:
                        tile = jnp.where(left, neg, pair[a0 + 1])
                    else:
                        tile = neg
                    bias_ref[variant, h, ri * GRID_W:(ri + 1) * GRID_W,
                             vc * LANES:(vc + 1) * LANES] = tile


def _na_kernel(q_ref, k_ref, v_ref, rp_ref, o_ref, bias_ref, *, rows):
    j = pl.program_id(1)
    nblk = pl.num_programs(1)

    @pl.when((pl.program_id(0) == 0) & (j == 0))
    def _():
        _na_build_bias(rp_ref, bias_ref, rows)

    variant = jnp.where(j == 0, 0, jnp.where(j == nblk - 1, 2, 1))
    band_row = jnp.clip(j * NA_QROWS - NA_WIN_ROWS // 2, 0, rows - NA_BAND)
    start = pl.multiple_of(band_row * GRID_W, GRID_W)
    kb = k_ref[0, pl.ds(start, NA_BAND * GRID_W), :]
    vb = v_ref[0, pl.ds(start, NA_BAND * GRID_W), :]
    q = q_ref[0]
    head_of_lane = lax.broadcasted_iota(jnp.int32, q.shape, 1) // HEAD_DIM
    out = jnp.zeros(q.shape, jnp.float32)
    for h in range(NA_HEADS):
        mine = head_of_lane == h
        qh = jnp.where(mine, q, jnp.zeros_like(q))
        s = lax.dot_general(qh, kb, (((1,), (1,)), ((), ())),
                            preferred_element_type=jnp.float32) + bias_ref[variant, h]
        m = jnp.max(s, axis=-1, keepdims=True)
        p = jnp.exp(s - m)
        l = jnp.sum(p, axis=-1, keepdims=True)
        o = jnp.dot(p.astype(jnp.bfloat16), vb, preferred_element_type=jnp.float32)
        out = jnp.where(mine, o / l, out)
    o_ref[0] = out


def _na_attention(q, k, v, rp):
    B, S, W = q.shape
    rows = S // GRID_W
    tq = NA_QROWS * GRID_W
    return pl.pallas_call(
        functools.partial(_na_kernel, rows=rows),
        out_shape=jax.ShapeDtypeStruct((B, S, W), jnp.float32),
        grid=(B, S // tq),
        in_specs=[pl.BlockSpec((1, tq, W), lambda b, j: (b, j, 0)),
                  pl.BlockSpec((1, S, W), lambda b, j: (b, 0, 0), pipeline_mode=pl.Buffered(1)),
                  pl.BlockSpec((1, S, W), lambda b, j: (b, 0, 0), pipeline_mode=pl.Buffered(1)),
                  _const_spec(rp.shape)],
        out_specs=pl.BlockSpec((1, tq, W), lambda b, j: (b, j, 0)),
        scratch_shapes=[pltpu.VMEM((3, NA_HEADS, tq, NA_BAND * GRID_W), jnp.float32)],
        compiler_params=_cparams("arbitrary", "arbitrary"),
        name="na_attn",
    )(q, k, v, rp)


def _na_rpb_rows(rpb):
    n_col = rpb.shape[-1]
    p = jnp.pad(rpb.astype(jnp.float32), ((0, 0), (1, 1), (0, GRID_W - n_col)))
    return jnp.concatenate([p[:, :-1], p[:, 1:]], axis=-1)


def _sw_kernel(sink_ref, t5_ref, q_ref, k_ref, v_ref, bucket_ref, o_ref, bias_ref, *, seq):
    i = pl.program_id(1)
    nblk_total = seq // SW_BLOCK

    @pl.when((pl.program_id(0) == 0) & (i == 0))
    def _():
        bucket = bucket_ref[...]
        for h in range(SW_HEADS):
            acc = jnp.full(bucket.shape, NEG_INF, jnp.float32)
            for b in range(T5_BUCKETS):
                acc = jnp.where(bucket == b, t5_ref[b * SW_HEADS + h], acc)
            bias_ref[h] = acc

    col = lax.broadcasted_iota(jnp.int32, (SW_BLOCK, 3 * SW_BLOCK), 1)
    lane_group = lax.broadcasted_iota(jnp.int32, (SW_BLOCK, LANES), 1) // HEAD_DIM
    for blk in range(SW_TQ // SW_BLOCK):
        n = i * (SW_TQ // SW_BLOCK) + blk
        q0 = pl.multiple_of(n * SW_BLOCK, SW_BLOCK)
        left = pl.multiple_of(jnp.maximum(q0 - SW_BLOCK, 0), SW_BLOCK)
        right = pl.multiple_of(jnp.minimum(q0 + SW_BLOCK, seq - SW_BLOCK), SW_BLOCK)
        kw = jnp.concatenate([k_ref[0, pl.ds(left, SW_BLOCK), :], k_ref[0, pl.ds(q0, SW_BLOCK), :],
                              k_ref[0, pl.ds(right, SW_BLOCK), :]], axis=0)
        vw = jnp.concatenate([v_ref[0, pl.ds(left, SW_BLOCK), :], v_ref[0, pl.ds(q0, SW_BLOCK), :],
                              v_ref[0, pl.ds(right, SW_BLOCK), :]], axis=0)
        outside = ((n == 0) & (col < SW_BLOCK)) | ((n == nblk_total - 1) & (col >= 2 * SW_BLOCK))
        edge = jnp.where(outside, NEG_INF, 0.0)
        for r in range(SW_QW // LANES):
            qcol = q_ref[0, blk * SW_BLOCK:(blk + 1) * SW_BLOCK, r * LANES:(r + 1) * LANES]
            out = jnp.zeros((SW_BLOCK, LANES), jnp.float32)
            for g in range(SW_KV_HEADS):
                h = g * (SW_HEADS // SW_KV_HEADS) + r
                mine = lane_group == g
                qh = jnp.where(mine, qcol, jnp.zeros_like(qcol))
                s = lax.dot_general(qh, kw, (((1,), (1,)), ((), ())),
                                    preferred_element_type=jnp.float32)
                s = s + bias_ref[h] + edge
                sink = sink_ref[h]
                m = jnp.maximum(jnp.max(s, axis=-1, keepdims=True), sink)
                p = jnp.exp(s - m)
                l = jnp.sum(p, axis=-1, keepdims=True) + jnp.exp(sink - m)
                o = jnp.dot(p.astype(jnp.bfloat16), vw, preferred_element_type=jnp.float32)
                out = jnp.where(mine, o / l, out)
            o_ref[0, blk * SW_BLOCK:(blk + 1) * SW_BLOCK, r * LANES:(r + 1) * LANES] = out


def _sw_attention(q, k, v, bucket, t5_flat, sink):
    B, S, _ = q.shape
    return pl.pallas_call(
        functools.partial(_sw_kernel, seq=S),
        out_shape=jax.ShapeDtypeStruct((B, S, SW_QW), jnp.float32),
        grid_spec=pltpu.PrefetchScalarGridSpec(
            num_scalar_prefetch=2,
            grid=(B, S // SW_TQ),
            in_specs=[pl.BlockSpec((1, SW_TQ, SW_QW), lambda b, i, *_: (b, i, 0)),
                      pl.BlockSpec((1, S, SW_KW), lambda b, i, *_: (b, 0, 0),
                                   pipeline_mode=pl.Buffered(1)),
                      pl.BlockSpec((1, S, SW_KW), lambda b, i, *_: (b, 0, 0),
                                   pipeline_mode=pl.Buffered(1)),
                      pl.BlockSpec(bucket.shape, lambda b, i, *_: (0, 0),
                                   pipeline_mode=pl.Buffered(1))],
            out_specs=pl.BlockSpec((1, SW_TQ, SW_QW), lambda b, i, *_: (b, i, 0)),
            scratch_shapes=[pltpu.VMEM((SW_HEADS, SW_BLOCK, 3 * SW_BLOCK), jnp.float32)],
        ),
        compiler_params=_cparams("arbitrary", "arbitrary"),
        name="sw_attn",
    )(sink, t5_flat, q, k, v, bucket)


def _t5_bucket(rel):
    nb = T5_BUCKETS // 2
    ret = (rel > 0).astype(jnp.int32) * nb
    n = jnp.abs(rel)
    max_exact = nb // 2
    nf = jnp.maximum(n, max_exact).astype(jnp.float32)
    large = max_exact + (jnp.log(nf / max_exact) / math.log(T5_MAX_DIST / max_exact)
                         * (nb - max_exact)).astype(jnp.int32)
    large = jnp.minimum(large, nb - 1)
    return ret + jnp.where(n < max_exact, n, large)


def _sw_bucket_tile():
    qpos = jnp.arange(SW_BLOCK)
    kpos = jnp.arange(3 * SW_BLOCK) - SW_BLOCK
    rel = kpos[None, :] - qpos[:, None]
    return jnp.where(jnp.abs(rel) <= SW_RADIUS, _t5_bucket(rel), -1).astype(jnp.int32)


def _ax_kernel(qt_ref, qnorm_ref, kmax_ref, k_ref, vt_ref, o_ref, qpad_ref, m_ref, l_ref, acc_ref,
               s0_ref, *, nkt, bounded):
    tq = qt_ref.shape[2]
    tk = AX_TK
    rep = AX_HEADS // AX_KV_HEADS
    bf = jnp.bfloat16
    row = lax.broadcasted_iota(jnp.int32, (16, tq), 0)
    for h in range(AX_HEADS):
        g = h // rep
        qth = qt_ref[0, h * HEAD_DIM:(h + 1) * HEAD_DIM, :]
        if bounded:
            shift = qnorm_ref[0, h:h + 1, :] * jnp.sqrt(kmax_ref[0, :, g * HEAD_DIM:g * HEAD_DIM + 1])
            extra = jnp.where(row == 0, -shift, 0.0).astype(bf)
        else:
            extra = jnp.zeros((16, tq), bf)
        qpad_ref[h] = jnp.concatenate(
            [qth, extra, jnp.zeros((LANES - HEAD_DIM - 16, tq), bf)], axis=0)
    if not bounded:
        m_ref[...] = jnp.full(m_ref.shape, -jnp.inf, jnp.float32)
    l_ref[...] = jnp.zeros(l_ref.shape, jnp.float32)
    acc_ref[...] = jnp.zeros(acc_ref.shape, jnp.float32)

    def score(kt, h):
        ks = pl.multiple_of(kt * tk, tk)
        return jnp.dot(k_ref[0, h // rep, pl.ds(ks, tk), :], qpad_ref[h],
                       preferred_element_type=jnp.float32)

    for h in range(AX_LOOKAHEAD):
        s0_ref[h] = score(0, h)

    def body(kt, carry):
        pending = [s0_ref[h] for h in range(AX_LOOKAHEAD)]
        for h in range(AX_HEADS):
            g = h // rep
            s = pending.pop(0)
            ahead = h + AX_LOOKAHEAD
            if ahead < AX_HEADS:
                pending.append(score(kt, ahead))
            else:
                s0_ref[ahead - AX_HEADS] = score(jnp.minimum(kt + 1, nkt - 1), ahead - AX_HEADS)
            vt = vt_ref[0, kt, g * HEAD_DIM:(g + 1) * HEAD_DIM, :]
            if bounded:
                p = jnp.exp2(s)
                l_ref[h] += jnp.sum(p.reshape(tk // 8, 8, tq), axis=0)
                acc_ref[h] += jnp.dot(vt, p.astype(bf), preferred_element_type=jnp.float32)
            else:
                m_old = m_ref[h]
                m_new = jnp.maximum(m_old, jnp.max(s, axis=0, keepdims=True))
                alpha = jnp.exp2(m_old - m_new)
                p = jnp.exp2(s - m_new)
                l_ref[h] = alpha * l_ref[h] + jnp.sum(p.reshape(tk // 8, 8, tq), axis=0)
                m_ref[h] = m_new
                pv = jnp.dot(vt, p.astype(bf), preferred_element_type=jnp.float32)
                acc_ref[h] = alpha * acc_ref[h] + pv
        return carry

    lax.fori_loop(0, nkt, body, 0, unroll=AX_UNROLL)
    out_t = jnp.concatenate([acc_ref[h] / jnp.sum(l_ref[h], axis=0, keepdims=True)
                             for h in range(AX_HEADS)], axis=0)
    o_ref[0] = out_t.T


def _ax_attention(qt, qnorm, kmax, k, vt, bounded):
    B, W, S = qt.shape
    nkt = vt.shape[1]
    return pl.pallas_call(
        functools.partial(_ax_kernel, nkt=nkt, bounded=bounded),
        out_shape=jax.ShapeDtypeStruct((B, S, W), jnp.float32),
        grid=(B, S // AX_TQ),
        in_specs=[pl.BlockSpec((1, W, AX_TQ), lambda b, i: (b, 0, i)),
                  pl.BlockSpec((1, 8, AX_TQ), lambda b, i: (b, 0, i)),
                  pl.BlockSpec((1, 1, AX_KW), lambda b, i: (b, 0, 0)),
                  pl.BlockSpec((1, AX_KV_HEADS, S, LANES), lambda b, i: (b, 0, 0, 0),
                               pipeline_mode=pl.Buffered(1)),
                  pl.BlockSpec((1, nkt, AX_KW, AX_TK), lambda b, i: (b, 0, 0, 0),
                               pipeline_mode=pl.Buffered(1))],
        out_specs=pl.BlockSpec((1, AX_TQ, W), lambda b, i: (b, i, 0)),
        scratch_shapes=[pltpu.VMEM((AX_HEADS, LANES, AX_TQ), jnp.bfloat16),
                        pltpu.VMEM((AX_HEADS, 1, AX_TQ), jnp.float32),
                        pltpu.VMEM((AX_HEADS, 8, AX_TQ), jnp.float32),
                        pltpu.VMEM((AX_HEADS, HEAD_DIM, AX_TQ), jnp.float32),
                        pltpu.VMEM((AX_LOOKAHEAD, AX_TK, AX_TQ), jnp.float32)],
        compiler_params=_cparams("arbitrary", "arbitrary"),
        name="ax_attn_bounded" if bounded else "ax_attn_online",
    )(qt, qnorm, kmax, k, vt)


def _ax_dispatch(qt, qnorm, kmax, k, vt):
    rep = AX_HEADS // AX_KV_HEADS
    knorm = jnp.sqrt(kmax[:, 0, ::HEAD_DIM])
    shift_max = jnp.max(qnorm[:, :AX_HEADS], axis=-1) * jnp.repeat(knorm, rep, axis=1)
    safe = jnp.max(shift_max) <= AX_MAX_SHIFT
    return lax.cond(safe,
                    functools.partial(_ax_attention, bounded=True),
                    functools.partial(_ax_attention, bounded=False),
                    qt, qnorm, kmax, k, vt)


def _post_kernel(x_ref, ya_ref, yb_ref, yc_ref, gg_ref, wo_ref, gta_ref, gf_ref, scf_ref, shf_ref,
                 gtf_ref, wgu_ref, wd_ref, gfin_ref, o_ref, *, final):
    bf = jnp.bfloat16
    gg = gg_ref[...]
    y = jnp.concatenate([
        _rms(ya_ref[0], gg[:, 0:NA_W]).astype(bf),
        _rms(yb_ref[0], gg[:, NA_W:NA_W + SW_QW]).astype(bf),
        _rms(yc_ref[0], gg[:, NA_W + SW_QW:]).astype(bf)], axis=1)
    x1 = x_ref[0] + gta_ref[0] * jnp.dot(y, wo_ref[...], preferred_element_type=jnp.float32)

    h = (_rms(x1, gf_ref[...]) * (1.0 + scf_ref[0]) + shf_ref[0]).astype(bf)
    acc = jnp.zeros(x1.shape, jnp.float32)
    for c0 in range(0, FFN_HIDDEN, FFN_CHUNK):
        c1 = min(c0 + FFN_CHUNK, FFN_HIDDEN)
        gate = jnp.dot(h, wgu_ref[:, c0:c1], preferred_element_type=jnp.float32)
        up = jnp.dot(h, wgu_ref[:, FFN_HIDDEN + c0:FFN_HIDDEN + c1],
                     preferred_element_type=jnp.float32)
        act = (gate * (1.0 / (1.0 + jnp.exp(-gate))) * up).astype(bf)
        acc = acc + jnp.dot(act, wd_ref[c0:c1, :], preferred_element_type=jnp.float32)
    x2 = x1 + gtf_ref[0] * acc
    if final:
        x2 = _rms(x2, gfin_ref[...])
    o_ref[0] = x2


def _post(x, ya, yb, yc, gg, wo, gta, gf, scf, shf, gtf, wgu, wd, gfin, final):
    B, S, D = x.shape
    tm = TOK_TILE
    tok = lambda width: pl.BlockSpec((1, tm, width), lambda b, i: (b, i, 0))
    vec = pl.BlockSpec((1, 1, D), lambda b, i: (b, 0, 0))
    return pl.pallas_call(
        functools.partial(_post_kernel, final=final),
        out_shape=jax.ShapeDtypeStruct((B, S, D), jnp.float32),
        grid=(B, S // tm),
        in_specs=[tok(D), tok(NA_W), tok(SW_QW), tok(AX_QW), _const_spec((1, D)),
                  _const_spec(wo.shape), vec, _const_spec((1, D)), vec, vec, vec,
                  _const_spec(wgu.shape), _const_spec(wd.shape), _const_spec((1, D))],
        out_specs=tok(D),
        compiler_params=_cparams("arbitrary", "arbitrary"),
        name="out_proj_ffn",
    )(x, ya, yb, yc, gg, wo, gta, gf, scf, shf, gtf, wgu, wd, gfin)


def _rope_tables(S):
    t = jnp.arange(S)
    row = (t // GRID_W).astype(jnp.float32)
    col = (t % GRID_W).astype(jnp.float32)
    axis_dim = HEAD_DIM // 2
    freqs = ROPE_THETA ** (-jnp.arange(0, axis_dim, 2, dtype=jnp.float32) / axis_dim)
    ang = jnp.stack([row[:, None] * freqs, col[:, None] * freqs], axis=1)
    cos = jnp.cos(ang)
    sin = jnp.sin(ang)
    cos_h = jnp.concatenate([cos, cos], axis=-1).reshape(S, HEAD_DIM)
    sin_h = jnp.concatenate([-sin, sin], axis=-1).reshape(S, HEAD_DIM)
    return jnp.tile(cos_h, (1, LANES // HEAD_DIM)), jnp.tile(sin_h, (1, LANES // HEAD_DIM))


def _block_diag_ones(width):
    idx = np.arange(width) // HEAD_DIM
    return jnp.asarray(idx[:, None] == idx[None, :], dtype=jnp.bfloat16)


_SW_PERM = np.concatenate([np.arange(HEAD_DIM) + (g * (SW_HEADS // SW_KV_HEADS) + r) * HEAD_DIM
                           for r in range(SW_HEADS // SW_KV_HEADS) for g in range(SW_KV_HEADS)])


def kernel(x, c, w_mod, b_mod, g_attn, w_in, rpb_na, sink_sw, t5_table, gq_ax, gk_ax, g_group,
           w_o, g_ffn, w_gu, w_down, g_final):
    B, S, D = x.shape
    L = w_mod.shape[0]
    bf = jnp.bfloat16
    rows = S // GRID_W

    c_pad = jnp.pad(c, ((0, 8 - B), (0, 0)))
    mod = _modulation(c_pad, w_mod, b_mod)[:, :B]
    mod = mod.reshape(L, B, 6, 1, D)

    cos, sin = _rope_tables(S)
    bdq = _block_diag_ones(AX_QW)
    bdk = _block_diag_ones(AX_KW)
    sw_bucket = _sw_bucket_tile()
    t5_flat = t5_table.astype(jnp.float32).reshape(-1)

    in_cols = np.arange(IN_WIDTH)
    in_cols[OFF_QB:OFF_QB + SW_QW] = OFF_QB + _SW_PERM
    group_perm = np.arange(D)
    group_perm[NA_W:NA_W + SW_QW] = NA_W + _SW_PERM

    for l in range(L):
        sh_a, sc_a, gt_a, sh_f, sc_f, gt_f = [mod[l, :, i] for i in range(6)]
        w_in_l = w_in[l][:, in_cols].astype(bf)
        qa, ka, va, qb, kb, vb, qct, kc, vct, qnorm, kmax = _in_proj(
            x, sc_a, sh_a, g_attn[l].reshape(1, D), w_in_l, cos, sin,
            jnp.tile(gq_ax[l], LANES // HEAD_DIM).reshape(1, LANES),
            jnp.tile(gk_ax[l], LANES // HEAD_DIM).reshape(1, LANES), bdq, bdk)
        ya = _na_attention(qa, ka, va, _na_rpb_rows(rpb_na[l]))
        yb = _sw_attention(qb, kb, vb, sw_bucket, t5_flat, sink_sw[l])
        yc = _ax_dispatch(qct, qnorm, kmax, kc, vct)
        x = _post(x, ya, yb, yc, g_group[l][group_perm].reshape(1, D),
                  w_o[l][group_perm, :].astype(bf), gt_a, g_ffn[l].reshape(1, D), sc_f, sh_f, gt_f,
                  w_gu[l].astype(bf), w_down[l].astype(bf), g_final.reshape(1, D),
                  final=(l == L - 1))
    return x
```

```python
import functools
import math

import jax
import jax.numpy as jnp
import numpy as np
from jax import lax
from jax.experimental import pallas as pl
from jax.experimental.pallas import tpu as pltpu

D_MODEL = 1024
HEAD_DIM = 64
GRID_W = 64
NA_HEADS = 4
SW_HEADS = 6
SW_KV_HEADS = 2
AX_HEADS = 6
AX_KV_HEADS = 2
NA_WIN_ROWS = 8
NA_WIN_COLS = 16
SW_RADIUS = 128
SW_BLOCK = 128
T5_BUCKETS = 32
T5_MAX_DIST = 128
ROPE_THETA = 10000.0
FFN_HIDDEN = 2816
EPS = 1e-6
NEG_INF = -1e30

NA_W = NA_HEADS * HEAD_DIM
SW_QW = SW_HEADS * HEAD_DIM
SW_KW = SW_KV_HEADS * HEAD_DIM
AX_QW = AX_HEADS * HEAD_DIM
AX_KW = AX_KV_HEADS * HEAD_DIM
IN_WIDTH = 3 * NA_W + SW_QW + 2 * SW_KW + AX_QW + 2 * AX_KW
OFF_QA, OFF_KA, OFF_VA = 0, NA_W, 2 * NA_W
OFF_QB = 3 * NA_W
OFF_KB = OFF_QB + SW_QW
OFF_VB = OFF_KB + SW_KW
OFF_QC = OFF_VB + SW_KW
OFF_KC = OFF_QC + AX_QW
OFF_VC = OFF_KC + AX_KW

LANES = 128
VMEM_LIMIT = 56 * 1024 * 1024

TOK_TILE = 512
AX_TQ = 256
AX_TK = TOK_TILE
AX_LOOKAHEAD = 2
AX_UNROLL = 4
NA_QROWS = 4
NA_BAND = NA_QROWS + NA_WIN_ROWS
SW_TQ = 1024
SW_LOOKAHEAD = 2
FFN_CHUNK = 512

QK_SCALE = HEAD_DIM ** -0.5
LOG2E = math.log2(math.e)
Q_SCALE_EXP2 = QK_SCALE * LOG2E
AX_MAX_SHIFT = 60.0


def _cparams(*sem):
    return pltpu.CompilerParams(dimension_semantics=sem, vmem_limit_bytes=VMEM_LIMIT)


def _const_spec(shape):
    n = len(shape)
    return pl.BlockSpec(shape, lambda *_: (0,) * n, pipeline_mode=pl.Buffered(1))


def _rms(x, g):
    return x * lax.rsqrt(jnp.mean(x * x, axis=-1, keepdims=True) + EPS) * g


def _mod_kernel(c_ref, w_ref, b_ref, o_ref):
    c = c_ref[...]
    cond = c * (1.0 / (1.0 + jnp.exp(-c)))
    o_ref[0] = jnp.dot(cond, w_ref[0], preferred_element_type=jnp.float32,
                       precision=lax.Precision.HIGHEST) + b_ref[0]


def _modulation(c_pad, w_mod, b_mod):
    L, D, N = w_mod.shape
    tn = 1536
    return pl.pallas_call(
        _mod_kernel,
        out_shape=jax.ShapeDtypeStruct((L, c_pad.shape[0], N), jnp.float32),
        grid=(L, N // tn),
        in_specs=[pl.BlockSpec(c_pad.shape, lambda l, j: (0, 0)),
                  pl.BlockSpec((1, D, tn), lambda l, j: (l, 0, j)),
                  pl.BlockSpec((1, 1, tn), lambda l, j: (l, 0, j))],
        out_specs=pl.BlockSpec((1, c_pad.shape[0], tn), lambda l, j: (l, 0, j)),
        compiler_params=_cparams("arbitrary", "arbitrary"),
        name="adaln_mod",
    )(c_pad, w_mod, b_mod.reshape(L, 1, N))


def _head_sumsq(t, bd):
    t2 = t * t
    hi = t2.astype(jnp.bfloat16)
    lo = (t2 - hi.astype(jnp.float32)).astype(jnp.bfloat16)
    return (jnp.dot(hi, bd, preferred_element_type=jnp.float32)
            + jnp.dot(lo, bd, preferred_element_type=jnp.float32))


def _rope_chunk(t, cos, sin_signed, first_half):
    swapped = jnp.where(first_half, pltpu.roll(t, LANES - 16, 1), pltpu.roll(t, 16, 1))
    return t * cos + swapped * sin_signed


def _in_kernel(x_ref, sc_ref, sh_ref, g_ref, w_ref, cos_ref, sin_ref, gq_ref, gk_ref,
               bdq_ref, bdk_ref,
               qa_ref, ka_ref, va_ref, qb_ref, kb_ref, vbt_ref, qct_ref, kc_ref, vct_ref,
               qnorm_ref, kmax_ref):
    x = x_ref[0]
    h = _rms(x, g_ref[...]) * (1.0 + sc_ref[0]) + sh_ref[0]
    proj = jnp.dot(h.astype(jnp.bfloat16), w_ref[...], preferred_element_type=jnp.float32)

    bf = jnp.bfloat16
    qa_ref[0] = (proj[:, OFF_QA:OFF_QA + NA_W] * QK_SCALE).astype(bf)
    ka_ref[0] = proj[:, OFF_KA:OFF_KA + NA_W].astype(bf)
    va_ref[0] = proj[:, OFF_VA:OFF_VA + NA_W].astype(bf)
    qb_ref[0] = (proj[:, OFF_QB:OFF_QB + SW_QW] * Q_SCALE_EXP2).astype(bf)
    kb_ref[0] = proj[:, OFF_KB:OFF_KB + SW_KW].astype(bf)
    for j in range(vbt_ref.shape[1]):
        vbt_ref[0, j] = proj[j * SW_BLOCK:(j + 1) * SW_BLOCK, OFF_VB:OFF_VB + SW_KW].T.astype(bf)

    cos = cos_ref[...]
    sin = sin_ref[...]
    lane = lax.broadcasted_iota(jnp.int32, cos.shape, 1)
    first_half = (lane % 32) < 16

    qc = proj[:, OFF_QC:OFF_QC + AX_QW]
    qn = qc * lax.rsqrt(_head_sumsq(qc, bdq_ref[...]) * (1.0 / HEAD_DIM) + EPS)
    chunks = []
    for j in range(AX_QW // LANES):
        t = qn[:, j * LANES:(j + 1) * LANES] * gq_ref[...]
        chunks.append(_rope_chunk(t, cos, sin, first_half) * Q_SCALE_EXP2)
    qt = jnp.concatenate(chunks, axis=1).T.astype(bf)
    qct_ref[0] = qt
    qsq = qt.astype(jnp.float32)
    qsq = qsq * qsq
    norms = [jnp.sqrt(jnp.sum(qsq[hh * HEAD_DIM:(hh + 1) * HEAD_DIM], axis=0, keepdims=True))
             for hh in range(AX_HEADS)]
    qnorm_ref[0] = jnp.concatenate(norms + [jnp.zeros_like(norms[0])] * (8 - AX_HEADS), axis=0)

    kc = proj[:, OFF_KC:OFF_KC + AX_KW]
    kn = kc * lax.rsqrt(_head_sumsq(kc, bdk_ref[...]) * (1.0 / HEAD_DIM) + EPS) * gk_ref[...]
    kr = _rope_chunk(kn, cos, sin, first_half)
    kb = kr.astype(bf)
    ones_lane = jnp.where(lane == HEAD_DIM, 1.0, 0.0).astype(bf)
    kc_ref[0, 0] = jnp.where(lane < HEAD_DIM, kb, ones_lane)
    kc_ref[0, 1] = jnp.where(lane < HEAD_DIM, pltpu.roll(kr, HEAD_DIM, 1).astype(bf), ones_lane)
    kf = kb.astype(jnp.float32)
    tile_max = jnp.max(_head_sumsq(kf, bdk_ref[...]), axis=0, keepdims=True)
    first = pl.program_id(1) == 0

    @pl.when(first)
    def _():
        kmax_ref[0] = tile_max

    @pl.when(jnp.logical_not(first))
    def _():
        kmax_ref[0] = jnp.maximum(kmax_ref[0], tile_max)

    vct_ref[0, 0] = proj[:, OFF_VC:OFF_VC + AX_KW].T.astype(bf)


def _in_proj(x, sc, sh, g, w, cos, sin, gq, gk, bdq, bdk):
    B, S, D = x.shape
    tm = TOK_TILE
    bf = jnp.bfloat16
    tok = lambda width: pl.BlockSpec((1, tm, width), lambda b, i: (b, i, 0))
    vec = pl.BlockSpec((1, 1, D), lambda b, i: (b, 0, 0))
    out_shape = (
        jax.ShapeDtypeStruct((B, S, NA_W), bf), jax.ShapeDtypeStruct((B, S, NA_W), bf),
        jax.ShapeDtypeStruct((B, S, NA_W), bf),
        jax.ShapeDtypeStruct((B, S, SW_QW), bf), jax.ShapeDtypeStruct((B, S, SW_KW), bf),
        jax.ShapeDtypeStruct((B, S // SW_BLOCK, SW_KW, SW_BLOCK), bf),
        jax.ShapeDtypeStruct((B, AX_QW, S), bf),
        jax.ShapeDtypeStruct((B, AX_KV_HEADS, S, LANES), bf),
        jax.ShapeDtypeStruct((B, S // tm, AX_KW, tm), bf),
        jax.ShapeDtypeStruct((B, 8, S), jnp.float32),
        jax.ShapeDtypeStruct((B, 1, AX_KW), jnp.float32),
    )
    out_specs = (
        tok(NA_W), tok(NA_W), tok(NA_W), tok(SW_QW), tok(SW_KW),
        pl.BlockSpec((1, tm // SW_BLOCK, SW_KW, SW_BLOCK), lambda b, i: (b, i, 0, 0)),
        pl.BlockSpec((1, AX_QW, tm), lambda b, i: (b, 0, i)),
        pl.BlockSpec((1, AX_KV_HEADS, tm, LANES), lambda b, i: (b, 0, i, 0)),
        pl.BlockSpec((1, 1, AX_KW, tm), lambda b, i: (b, i, 0, 0)),
        pl.BlockSpec((1, 8, tm), lambda b, i: (b, 0, i)),
        pl.BlockSpec((1, 1, AX_KW), lambda b, i: (b, 0, 0)),
    )
    return pl.pallas_call(
        _in_kernel,
        out_shape=out_shape,
        grid=(B, S // tm),
        in_specs=[tok(D), vec, vec, _const_spec((1, D)), _const_spec(w.shape),
                  pl.BlockSpec((tm, LANES), lambda b, i: (i, 0)),
                  pl.BlockSpec((tm, LANES), lambda b, i: (i, 0)),
                  _const_spec((1, LANES)), _const_spec((1, LANES)),
                  _const_spec(bdq.shape), _const_spec(bdk.shape)],
        out_specs=out_specs,
        compiler_params=_cparams("arbitrary", "arbitrary"),
        name="in_proj",
    )(x, sc, sh, g, w, cos, sin, gq, gk, bdq, bdk)


def _na_build_bias(rp_ref, bias_ref, rows):
    shape = (GRID_W, LANES)
    c = lax.broadcasted_iota(jnp.int32, shape, 0)
    lane = lax.broadcasted_iota(jnp.int32, shape, 1)
    kc = lane % GRID_W
    cs = jnp.clip(c - NA_WIN_COLS // 2, 0, GRID_W - NA_WIN_COLS)
    col_ok = (kc >= cs) & (kc < cs + NA_WIN_COLS)
    left = lane < GRID_W
    neg = jnp.full(shape, NEG_INF, jnp.float32)
    n_off = 2 * NA_WIN_ROWS - 1
    for h in range(NA_HEADS):
        pair = []
        for a in range(-1, n_off):
            x = jnp.broadcast_to(rp_ref[h, a + 1:a + 2, :], shape)
            t = pltpu.roll(x, LANES - (NA_WIN_COLS - 1), 1, stride=1, stride_axis=0)
            pair.append(jnp.where(col_ok, t, neg))
        for variant, qr0 in enumerate((0, 2 * NA_QROWS, rows - NA_QROWS)):
            bs = int(np.clip(qr0 - NA_WIN_ROWS // 2, 0, rows - NA_BAND))
            for ri in range(NA_QROWS):
                r = qr0 + ri
                rs = int(np.clip(r - NA_WIN_ROWS // 2, 0, rows - NA_WIN_ROWS))
                for vc in range(NA_BAND // 2):
                    kr0 = bs + 2 * vc
                    ok0 = rs <= kr0 < rs + NA_WIN_ROWS
                    ok1 = rs <= kr0 + 1 < rs + NA_WIN_ROWS
                    a0 = kr0 - r + NA_WIN_ROWS - 1
                    if ok0 and ok1:
                        tile = pair[a0 + 1]
                    elif ok0:
                        tile = jnp.where(left, pair[a0 + 1], neg)
                    elif ok1:
                        tile = jnp.where(left, neg, pair[a0 + 1])
                    else:
                        tile = neg
                    bias_ref[variant, h, ri * GRID_W:(ri + 1) * GRID_W,
                             vc * LANES:(vc + 1) * LANES] = tile


def _na_kernel(q_ref, k_ref, v_ref, rp_ref, o_ref, bias_ref, *, rows):
    j = pl.program_id(1)
    nblk = pl.num_programs(1)

    @pl.when((pl.program_id(0) == 0) & (j == 0))
    def _():
        _na_build_bias(rp_ref, bias_ref, rows)

    variant = jnp.where(j == 0, 0, jnp.where(j == nblk - 1, 2, 1))
    band_row = jnp.clip(j * NA_QROWS - NA_WIN_ROWS // 2, 0, rows - NA_BAND)
    start = pl.multiple_of(band_row * GRID_W, GRID_W)
    kb = k_ref[0, pl.ds(start, NA_BAND * GRID_W), :]
    vb = v_ref[0, pl.ds(start, NA_BAND * GRID_W), :]
    q = q_ref[0]
    head_of_lane = lax.broadcasted_iota(jnp.int32, q.shape, 1) // HEAD_DIM
    out = jnp.zeros(q.shape, jnp.float32)
    for h in range(NA_HEADS):
        mine = head_of_lane == h
        qh = jnp.where(mine, q, jnp.zeros_like(q))
        s = lax.dot_general(qh, kb, (((1,), (1,)), ((), ())),
                            preferred_element_type=jnp.float32) + bias_ref[variant, h]
        m = jnp.max(s, axis=-1, keepdims=True)
        p = jnp.exp(s - m)
        l = jnp.sum(p, axis=-1, keepdims=True)
        o = jnp.dot(p.astype(jnp.bfloat16), vb, preferred_element_type=jnp.float32)
        out = jnp.where(mine, o / l, out)
    o_ref[0] = out


def _na_attention(q, k, v, rp):
    B, S, W = q.shape
    rows = S // GRID_W
    tq = NA_QROWS * GRID_W
    return pl.pallas_call(
        functools.partial(_na_kernel, rows=rows),
        out_shape=jax.ShapeDtypeStruct((B, S, W), jnp.float32),
        grid=(B, S // tq),
        in_specs=[pl.BlockSpec((1, tq, W), lambda b, j: (b, j, 0)),
                  pl.BlockSpec((1, S, W), lambda b, j: (b, 0, 0), pipeline_mode=pl.Buffered(1)),
                  pl.BlockSpec((1, S, W), lambda b, j: (b, 0, 0), pipeline_mode=pl.Buffered(1)),
                  _const_spec(rp.shape)],
        out_specs=pl.BlockSpec((1, tq, W), lambda b, j: (b, j, 0)),
        scratch_shapes=[pltpu.VMEM((3, NA_HEADS, tq, NA_BAND * GRID_W), jnp.float32)],
        compiler_params=_cparams("arbitrary", "arbitrary"),
        name="na_attn",
    )(q, k, v, rp)


def _na_rpb_rows(rpb):
    n_col = rpb.shape[-1]
    p = jnp.pad(rpb.astype(jnp.float32), ((0, 0), (1, 1), (0, GRID_W - n_col)))
    return jnp.concatenate([p[:, :-1], p[:, 1:]], axis=-1)


def _sw_kernel(sink_ref, t5_ref, q_ref, k_ref, vt_ref, bucket_ref, o_ref, bias_ref, *, seq):
    i = pl.program_id(1)
    nblk_total = seq // SW_BLOCK

    @pl.when((pl.program_id(0) == 0) & (i == 0))
    def _():
        bucket = bucket_ref[...]
        for h in range(SW_HEADS):
            acc = jnp.full(bucket.shape, NEG_INF, jnp.float32)
            for b in range(T5_BUCKETS):
                acc = jnp.where(bucket == b, t5_ref[b * SW_HEADS + h] * LOG2E, acc)
            bias_ref[:, h * SW_BLOCK:(h + 1) * SW_BLOCK] = acc

    nblk = SW_TQ // SW_BLOCK
    rep = SW_HEADS // SW_KV_HEADS
    bf = jnp.bfloat16
    width = SW_HEADS * SW_BLOCK
    lane_group = lax.broadcasted_iota(jnp.int32, (SW_BLOCK, LANES), 1) // HEAD_DIM
    row_group = lax.broadcasted_iota(jnp.int32, (SW_BLOCK, LANES), 0) // HEAD_DIM
    key_row = lax.broadcasted_iota(jnp.int32, (3 * SW_BLOCK, width), 0)
    sink = jnp.concatenate([jnp.full((1, SW_BLOCK), sink_ref[h] * LOG2E, jnp.float32)
                            for h in range(SW_HEADS)], axis=1)
    ones = jnp.ones((16, 3 * SW_BLOCK), bf)

    def neighbours(blk):
        n = i * nblk + blk
        return jnp.maximum(n - 1, 0), n, jnp.minimum(n + 1, nblk_total - 1)

    def scores(blk):
        rows = slice(blk * SW_BLOCK, (blk + 1) * SW_BLOCK)
        stack = []
        for g in range(SW_KV_HEADS):
            for r in range(rep):
                qcol = q_ref[0, rows, r * LANES:(r + 1) * LANES]
                stack.append(jnp.where(lane_group == g, qcol, jnp.zeros_like(qcol)))
        kw = jnp.concatenate([k_ref[0, pl.ds(pl.multiple_of(nb * SW_BLOCK, SW_BLOCK), SW_BLOCK), :]
                              for nb in neighbours(blk)], axis=0)
        return lax.dot_general(kw, jnp.concatenate(stack, axis=0), (((1,), (1,)), ((), ())),
                               preferred_element_type=jnp.float32)

    pending = [scores(blk) for blk in range(min(SW_LOOKAHEAD, nblk))]
    for blk in range(nblk):
        s = pending.pop(0)
        if blk + SW_LOOKAHEAD < nblk:
            pending.append(scores(blk + SW_LOOKAHEAD))
        s = s + bias_ref[...]
        left, n, right = neighbours(blk)
        if blk in (0, nblk - 1):
            outside = (((n == 0) & (key_row < SW_BLOCK))
                       | ((n == nblk_total - 1) & (key_row >= 2 * SW_BLOCK)))
            s = jnp.where(outside, NEG_INF, s)
        m = jnp.maximum(jnp.max(s, axis=0, keepdims=True), sink)
        p = jnp.exp2(s - m).astype(bf)
        vt_ext = jnp.concatenate(
            [jnp.concatenate([vt_ref[0, left], vt_ref[0, n], vt_ref[0, right]], axis=1), ones], axis=0)
        o = jnp.dot(vt_ext, p, preferred_element_type=jnp.float32)
        res = o[:LANES] / (o[LANES:LANES + 1] + jnp.exp2(sink - m))
        for r in range(rep):
            yt = jnp.where(row_group == 0, res[:, r * SW_BLOCK:(r + 1) * SW_BLOCK],
                           res[:, (rep + r) * SW_BLOCK:(rep + r + 1) * SW_BLOCK])
            o_ref[0, blk * SW_BLOCK:(blk + 1) * SW_BLOCK, r * LANES:(r + 1) * LANES] = yt.T


def _sw_attention(q, k, vt, bucket, t5_flat, sink):
    B, S, _ = q.shape
    nb = S // SW_BLOCK
    return pl.pallas_call(
        functools.partial(_sw_kernel, seq=S),
        out_shape=jax.ShapeDtypeStruct((B, S, SW_QW), jnp.float32),
        grid_spec=pltpu.PrefetchScalarGridSpec(
            num_scalar_prefetch=2,
            grid=(B, S // SW_TQ),
            in_specs=[pl.BlockSpec((1, SW_TQ, SW_QW), lambda b, i, *_: (b, i, 0)),
                      pl.BlockSpec((1, S, SW_KW), lambda b, i, *_: (b, 0, 0),
                                   pipeline_mode=pl.Buffered(1)),
                      pl.BlockSpec((1, nb, SW_KW, SW_BLOCK), lambda b, i, *_: (b, 0, 0, 0),
                                   pipeline_mode=pl.Buffered(1)),
                      pl.BlockSpec(bucket.shape, lambda b, i, *_: (0, 0),
                                   pipeline_mode=pl.Buffered(1))],
            out_specs=pl.BlockSpec((1, SW_TQ, SW_QW), lambda b, i, *_: (b, i, 0)),
            scratch_shapes=[pltpu.VMEM((3 * SW_BLOCK, SW_HEADS * SW_BLOCK), jnp.float32)],
        ),
        compiler_params=_cparams("arbitrary", "arbitrary"),
        name="sw_attn",
    )(sink, t5_flat, q, k, vt, bucket)


def _t5_bucket(rel):
    nb = T5_BUCKETS // 2
    ret = (rel > 0).astype(jnp.int32) * nb
    n = jnp.abs(rel)
    max_exact = nb // 2
    nf = jnp.maximum(n, max_exact).astype(jnp.float32)
    large = max_exact + (jnp.log(nf / max_exact) / math.log(T5_MAX_DIST / max_exact)
                         * (nb - max_exact)).astype(jnp.int32)
    large = jnp.minimum(large, nb - 1)
    return ret + jnp.where(n < max_exact, n, large)


def _sw_bucket_tile():
    qpos = jnp.arange(SW_BLOCK)
    kpos = jnp.arange(3 * SW_BLOCK) - SW_BLOCK
    rel = kpos[:, None] - qpos[None, :]
    return jnp.where(jnp.abs(rel) <= SW_RADIUS, _t5_bucket(rel), -1).astype(jnp.int32)


def _ax_kernel(qt_ref, qnorm_ref, kmax_ref, k_ref, vt_ref, o_ref, qpad_ref, m_ref, l_ref, acc_ref,
               s0_ref, *, nkt, bounded):
    tq = qt_ref.shape[2]
    tk = AX_TK
    rep = AX_HEADS // AX_KV_HEADS
    bf = jnp.bfloat16
    row = lax.broadcasted_iota(jnp.int32, (16, tq), 0)
    for h in range(AX_HEADS):
        g = h // rep
        qth = qt_ref[0, h * HEAD_DIM:(h + 1) * HEAD_DIM, :]
        if bounded:
            shift = qnorm_ref[0, h:h + 1, :] * jnp.sqrt(kmax_ref[0, :, g * HEAD_DIM:g * HEAD_DIM + 1])
            extra = jnp.where(row == 0, -shift, 0.0).astype(bf)
        else:
            extra = jnp.zeros((16, tq), bf)
        qpad_ref[h] = jnp.concatenate(
            [qth, extra, jnp.zeros((LANES - HEAD_DIM - 16, tq), bf)], axis=0)
    if not bounded:
        m_ref[...] = jnp.full(m_ref.shape, -jnp.inf, jnp.float32)
    l_ref[...] = jnp.zeros(l_ref.shape, jnp.float32)
    acc_ref[...] = jnp.zeros(acc_ref.shape, jnp.float32)

    def score(kt, h):
        ks = pl.multiple_of(kt * tk, tk)
        return jnp.dot(k_ref[0, h // rep, pl.ds(ks, tk), :], qpad_ref[h],
                       preferred_element_type=jnp.float32)

    for h in range(AX_LOOKAHEAD):
        s0_ref[h] = score(0, h)

    def body(kt, carry):
        pending = [s0_ref[h] for h in range(AX_LOOKAHEAD)]
        for h in range(AX_HEADS):
            g = h // rep
            s = pending.pop(0)
            ahead = h + AX_LOOKAHEAD
            if ahead < AX_HEADS:
                pending.append(score(kt, ahead))
            else:
                s0_ref[ahead - AX_HEADS] = score(jnp.minimum(kt + 1, nkt - 1), ahead - AX_HEADS)
            vt = vt_ref[0, kt, g * HEAD_DIM:(g + 1) * HEAD_DIM, :]
            if bounded:
                p = jnp.exp2(s)
                l_ref[h] += jnp.sum(p.reshape(tk // 8, 8, tq), axis=0)
                acc_ref[h] += jnp.dot(vt, p.astype(bf), preferred_element_type=jnp.float32)
            else:
                m_old = m_ref[h]
                m_new = jnp.maximum(m_old, jnp.max(s, axis=0, keepdims=True))
                alpha = jnp.exp2(m_old - m_new)
                p = jnp.exp2(s - m_new)
                l_ref[h] = alpha * l_ref[h] + jnp.sum(p.reshape(tk // 8, 8, tq), axis=0)
                m_ref[h] = m_new
                pv = jnp.dot(vt, p.astype(bf), preferred_element_type=jnp.float32)
                acc_ref[h] = alpha * acc_ref[h] + pv
        return carry

    lax.fori_loop(0, nkt, body, 0, unroll=AX_UNROLL)
    out_t = jnp.concatenate([acc_ref[h] / jnp.sum(l_ref[h], axis=0, keepdims=True)
                             for h in range(AX_HEADS)], axis=0)
    o_ref[0] = out_t.T


def _ax_attention(qt, qnorm, kmax, k, vt, bounded):
    B, W, S = qt.shape
    nkt = vt.shape[1]
    return pl.pallas_call(
        functools.partial(_ax_kernel, nkt=nkt, bounded=bounded),
        out_shape=jax.ShapeDtypeStruct((B, S, W), jnp.float32),
        grid=(B, S // AX_TQ),
        in_specs=[pl.BlockSpec((1, W, AX_TQ), lambda b, i: (b, 0, i)),
                  pl.BlockSpec((1, 8, AX_TQ), lambda b, i: (b, 0, i)),
                  pl.BlockSpec((1, 1, AX_KW), lambda b, i: (b, 0, 0)),
                  pl.BlockSpec((1, AX_KV_HEADS, S, LANES), lambda b, i: (b, 0, 0, 0),
                               pipeline_mode=pl.Buffered(1)),
                  pl.BlockSpec((1, nkt, AX_KW, AX_TK), lambda b, i: (b, 0, 0, 0),
                               pipeline_mode=pl.Buffered(1))],
        out_specs=pl.BlockSpec((1, AX_TQ, W), lambda b, i: (b, i, 0)),
        scratch_shapes=[pltpu.VMEM((AX_HEADS, LANES, AX_TQ), jnp.bfloat16),
                        pltpu.VMEM((AX_HEADS, 1, AX_TQ), jnp.float32),
                        pltpu.VMEM((AX_HEADS, 8, AX_TQ), jnp.float32),
                        pltpu.VMEM((AX_HEADS, HEAD_DIM, AX_TQ), jnp.float32),
                        pltpu.VMEM((AX_LOOKAHEAD, AX_TK, AX_TQ), jnp.float32)],
        compiler_params=_cparams("arbitrary", "arbitrary"),
        name="ax_attn_bounded" if bounded else "ax_attn_online",
    )(qt, qnorm, kmax, k, vt)


def _ax_dispatch(qt, qnorm, kmax, k, vt):
    rep = AX_HEADS // AX_KV_HEADS
    knorm = jnp.sqrt(kmax[:, 0, ::HEAD_DIM])
    shift_max = jnp.max(qnorm[:, :AX_HEADS], axis=-1) * jnp.repeat(knorm, rep, axis=1)
    safe = jnp.max(shift_max) <= AX_MAX_SHIFT
    return lax.cond(safe,
                    functools.partial(_ax_attention, bounded=True),
                    functools.partial(_ax_attention, bounded=False),
                    qt, qnorm, kmax, k, vt)


def _post_kernel(x_ref, ya_ref, yb_ref, yc_ref, gg_ref, wo_ref, gta_ref, gf_ref, scf_ref, shf_ref,
                 gtf_ref, wgu_ref, wd_ref, gfin_ref, o_ref, *, final):
    bf = jnp.bfloat16
    gg = gg_ref[...]
    y = jnp.concatenate([
        _rms(ya_ref[0], gg[:, 0:NA_W]).astype(bf),
        _rms(yb_ref[0], gg[:, NA_W:NA_W + SW_QW]).astype(bf),
        _rms(yc_ref[0], gg[:, NA_W + SW_QW:]).astype(bf)], axis=1)
    x1 = x_ref[0] + gta_ref[0] * jnp.dot(y, wo_ref[...], preferred_element_type=jnp.float32)

    h = (_rms(x1, gf_ref[...]) * (1.0 + scf_ref[0]) + shf_ref[0]).astype(bf)
    acc = jnp.zeros(x1.shape, jnp.float32)
    for c0 in range(0, FFN_HIDDEN, FFN_CHUNK):
        c1 = min(c0 + FFN_CHUNK, FFN_HIDDEN)
        gate = jnp.dot(h, wgu_ref[:, c0:c1], preferred_element_type=jnp.float32)
        up = jnp.dot(h, wgu_ref[:, FFN_HIDDEN + c0:FFN_HIDDEN + c1],
                     preferred_element_type=jnp.float32)
        act = (gate * (1.0 / (1.0 + jnp.exp(-gate))) * up).astype(bf)
        acc = acc + jnp.dot(act, wd_ref[c0:c1, :], preferred_element_type=jnp.float32)
    x2 = x1 + gtf_ref[0] * acc
    if final:
        x2 = _rms(x2, gfin_ref[...])
    o_ref[0] = x2


def _post(x, ya, yb, yc, gg, wo, gta, gf, scf, shf, gtf, wgu, wd, gfin, final):
    B, S, D = x.shape
    tm = TOK_TILE
    tok = lambda width: pl.BlockSpec((1, tm, width), lambda b, i: (b, i, 0))
    vec = pl.BlockSpec((1, 1, D), lambda b, i: (b, 0, 0))
    return pl.pallas_call(
        functools.partial(_post_kernel, final=final),
        out_shape=jax.ShapeDtypeStruct((B, S, D), jnp.float32),
        grid=(B, S // tm),
        in_specs=[tok(D), tok(NA_W), tok(SW_QW), tok(AX_QW), _const_spec((1, D)),
                  _const_spec(wo.shape), vec, _const_spec((1, D)), vec, vec, vec,
                  _const_spec(wgu.shape), _const_spec(wd.shape), _const_spec((1, D))],
        out_specs=tok(D),
        compiler_params=_cparams("arbitrary", "arbitrary"),
        name="out_proj_ffn",
    )(x, ya, yb, yc, gg, wo, gta, gf, scf, shf, gtf, wgu, wd, gfin)


def _rope_tables(S):
    t = jnp.arange(S)
    row = (t // GRID_W).astype(jnp.float32)
    col = (t % GRID_W).astype(jnp.float32)
    axis_dim = HEAD_DIM // 2
    freqs = ROPE_THETA ** (-jnp.arange(0, axis_dim, 2, dtype=jnp.float32) / axis_dim)
    ang = jnp.stack([row[:, None] * freqs, col[:, None] * freqs], axis=1)
    cos = jnp.cos(ang)
    sin = jnp.sin(ang)
    cos_h = jnp.concatenate([cos, cos], axis=-1).reshape(S, HEAD_DIM)
    sin_h = jnp.concatenate([-sin, sin], axis=-1).reshape(S, HEAD_DIM)
    return jnp.tile(cos_h, (1, LANES // HEAD_DIM)), jnp.tile(sin_h, (1, LANES // HEAD_DIM))


def _block_diag_ones(width):
    idx = np.arange(width) // HEAD_DIM
    return jnp.asarray(idx[:, None] == idx[None, :], dtype=jnp.bfloat16)


_SW_PERM = np.concatenate([np.arange(HEAD_DIM) + (g * (SW_HEADS // SW_KV_HEADS) + r) * HEAD_DIM
                           for r in range(SW_HEADS // SW_KV_HEADS) for g in range(SW_KV_HEADS)])


def kernel(x, c, w_mod, b_mod, g_attn, w_in, rpb_na, sink_sw, t5_table, gq_ax, gk_ax, g_group,
           w_o, g_ffn, w_gu, w_down, g_final):
    B, S, D = x.shape
    L = w_mod.shape[0]
    bf = jnp.bfloat16
    rows = S // GRID_W

    c_pad = jnp.pad(c, ((0, 8 - B), (0, 0)))
    mod = _modulation(c_pad, w_mod, b_mod)[:, :B]
    mod = mod.reshape(L, B, 6, 1, D)

    cos, sin = _rope_tables(S)
    bdq = _block_diag_ones(AX_QW)
    bdk = _block_diag_ones(AX_KW)
    sw_bucket = _sw_bucket_tile()
    t5_flat = t5_table.astype(jnp.float32).reshape(-1)

    in_cols = np.arange(IN_WIDTH)
    in_cols[OFF_QB:OFF_QB + SW_QW] = OFF_QB + _SW_PERM
    group_perm = np.arange(D)
    group_perm[NA_W:NA_W + SW_QW] = NA_W + _SW_PERM

    for l in range(L):
        sh_a, sc_a, gt_a, sh_f, sc_f, gt_f = [mod[l, :, i] for i in range(6)]
        w_in_l = w_in[l][:, in_cols].astype(bf)
        qa, ka, va, qb, kb, vb, qct, kc, vct, qnorm, kmax = _in_proj(
            x, sc_a, sh_a, g_attn[l].reshape(1, D), w_in_l, cos, sin,
            jnp.tile(gq_ax[l], LANES // HEAD_DIM).reshape(1, LANES),
            jnp.tile(gk_ax[l], LANES // HEAD_DIM).reshape(1, LANES), bdq, bdk)
        ya = _na_attention(qa, ka, va, _na_rpb_rows(rpb_na[l]))
        yb = _sw_attention(qb, kb, vb, sw_bucket, t5_flat, sink_sw[l])
        yc = _ax_dispatch(qct, qnorm, kmax, kc, vct)
        x = _post(x, ya, yb, yc, g_group[l][group_perm].reshape(1, D),
                  w_o[l][group_perm, :].astype(bf), gt_a, g_ffn[l].reshape(1, D), sc_f, sh_f, gt_f,
                  w_gu[l].astype(bf), w_down[l].astype(bf), g_final.reshape(1, D),
                  final=(l == L - 1))
    return x
```

```python
import functools
import math

import jax
import jax.numpy as jnp
import numpy as np
from jax import lax
from jax.experimental import pallas as pl
from jax.experimental.pallas import tpu as pltpu

D_MODEL = 1024
HEAD_DIM = 64
GRID_W = 64
NA_HEADS = 4
SW_HEADS = 6
SW_KV_HEADS = 2
AX_HEADS = 6
AX_KV_HEADS = 2
NA_WIN_ROWS = 8
NA_WIN_COLS = 16
SW_RADIUS = 128
SW_BLOCK = 128
T5_BUCKETS = 32
T5_MAX_DIST = 128
ROPE_THETA = 10000.0
FFN_HIDDEN = 2816
EPS = 1e-6
NEG_INF = -1e30

NA_W = NA_HEADS * HEAD_DIM
SW_QW = SW_HEADS * HEAD_DIM
SW_KW = SW_KV_HEADS * HEAD_DIM
AX_QW = AX_HEADS * HEAD_DIM
AX_KW = AX_KV_HEADS * HEAD_DIM
IN_WIDTH = 3 * NA_W + SW_QW + 2 * SW_KW + AX_QW + 2 * AX_KW
OFF_QA, OFF_KA, OFF_VA = 0, NA_W, 2 * NA_W
OFF_QB = 3 * NA_W
OFF_KB = OFF_QB + SW_QW
OFF_VB = OFF_KB + SW_KW
OFF_QC = OFF_VB + SW_KW
OFF_KC = OFF_QC + AX_QW
OFF_VC = OFF_KC + AX_KW

LANES = 128
VMEM_LIMIT = 56 * 1024 * 1024

TOK_TILE = 512
AX_TQ = 256
AX_TK = TOK_TILE
AX_LOOKAHEAD = 2
AX_UNROLL = 4
NA_QROWS = 4
NA_BAND = NA_QROWS + NA_WIN_ROWS
NA_TQ = 1024
NA_LOOKAHEAD = 1
SW_TQ = 1024
SW_LOOKAHEAD = 2
FFN_CHUNK = 512

QK_SCALE = HEAD_DIM ** -0.5
LOG2E = math.log2(math.e)
Q_SCALE_EXP2 = QK_SCALE * LOG2E
AX_MAX_SHIFT = 60.0


def _cparams(*sem):
    return pltpu.CompilerParams(dimension_semantics=sem, vmem_limit_bytes=VMEM_LIMIT)


def _const_spec(shape):
    n = len(shape)
    return pl.BlockSpec(shape, lambda *_: (0,) * n, pipeline_mode=pl.Buffered(1))


def _rms(x, g):
    return x * lax.rsqrt(jnp.mean(x * x, axis=-1, keepdims=True) + EPS) * g


def _mod_kernel(c_ref, w_ref, b_ref, o_ref):
    c = c_ref[...]
    cond = c * (1.0 / (1.0 + jnp.exp(-c)))
    o_ref[0] = jnp.dot(cond, w_ref[0], preferred_element_type=jnp.float32,
                       precision=lax.Precision.HIGHEST) + b_ref[0]


def _modulation(c_pad, w_mod, b_mod):
    L, D, N = w_mod.shape
    tn = 1536
    return pl.pallas_call(
        _mod_kernel,
        out_shape=jax.ShapeDtypeStruct((L, c_pad.shape[0], N), jnp.float32),
        grid=(L, N // tn),
        in_specs=[pl.BlockSpec(c_pad.shape, lambda l, j: (0, 0)),
                  pl.BlockSpec((1, D, tn), lambda l, j: (l, 0, j)),
                  pl.BlockSpec((1, 1, tn), lambda l, j: (l, 0, j))],
        out_specs=pl.BlockSpec((1, c_pad.shape[0], tn), lambda l, j: (l, 0, j)),
        compiler_params=_cparams("arbitrary", "arbitrary"),
        name="adaln_mod",
    )(c_pad, w_mod, b_mod.reshape(L, 1, N))


def _head_sumsq(t, bd):
    t2 = t * t
    hi = t2.astype(jnp.bfloat16)
    lo = (t2 - hi.astype(jnp.float32)).astype(jnp.bfloat16)
    return (jnp.dot(hi, bd, preferred_element_type=jnp.float32)
            + jnp.dot(lo, bd, preferred_element_type=jnp.float32))


def _rope_chunk(t, cos, sin_signed, first_half):
    swapped = jnp.where(first_half, pltpu.roll(t, LANES - 16, 1), pltpu.roll(t, 16, 1))
    return t * cos + swapped * sin_signed


def _in_kernel(x_ref, sc_ref, sh_ref, g_ref, w_ref, cos_ref, sin_ref, gq_ref, gk_ref,
               bdq_ref, bdk_ref,
               qa_ref, ka_ref, vat_ref, qb_ref, kb_ref, vbt_ref, qct_ref, kc_ref, vct_ref,
               qnorm_ref, kmax_ref):
    x = x_ref[0]
    h = _rms(x, g_ref[...]) * (1.0 + sc_ref[0]) + sh_ref[0]
    proj = jnp.dot(h.astype(jnp.bfloat16), w_ref[...], preferred_element_type=jnp.float32)

    bf = jnp.bfloat16
    qa_ref[0] = (proj[:, OFF_QA:OFF_QA + NA_W] * Q_SCALE_EXP2).astype(bf)
    ka_ref[0] = proj[:, OFF_KA:OFF_KA + NA_W].astype(bf)
    for j in range(vat_ref.shape[1]):
        vat_ref[0, j] = proj[j * LANES:(j + 1) * LANES, OFF_VA:OFF_VA + NA_W].T.astype(bf)
    qb_ref[0] = (proj[:, OFF_QB:OFF_QB + SW_QW] * Q_SCALE_EXP2).astype(bf)
    kb_ref[0] = proj[:, OFF_KB:OFF_KB + SW_KW].astype(bf)
    for j in range(vbt_ref.shape[1]):
        vbt_ref[0, j] = proj[j * SW_BLOCK:(j + 1) * SW_BLOCK, OFF_VB:OFF_VB + SW_KW].T.astype(bf)

    cos = cos_ref[...]
    sin = sin_ref[...]
    lane = lax.broadcasted_iota(jnp.int32, cos.shape, 1)
    first_half = (lane % 32) < 16

    qc = proj[:, OFF_QC:OFF_QC + AX_QW]
    qn = qc * lax.rsqrt(_head_sumsq(qc, bdq_ref[...]) * (1.0 / HEAD_DIM) + EPS)
    chunks = []
    for j in range(AX_QW // LANES):
        t = qn[:, j * LANES:(j + 1) * LANES] * gq_ref[...]
        chunks.append(_rope_chunk(t, cos, sin, first_half) * Q_SCALE_EXP2)
    qt = jnp.concatenate(chunks, axis=1).T.astype(bf)
    qct_ref[0] = qt
    qsq = qt.astype(jnp.float32)
    qsq = qsq * qsq
    norms = [jnp.sqrt(jnp.sum(qsq[hh * HEAD_DIM:(hh + 1) * HEAD_DIM], axis=0, keepdims=True))
             for hh in range(AX_HEADS)]
    qnorm_ref[0] = jnp.concatenate(norms + [jnp.zeros_like(norms[0])] * (8 - AX_HEADS), axis=0)

    kc = proj[:, OFF_KC:OFF_KC + AX_KW]
    kn = kc * lax.rsqrt(_head_sumsq(kc, bdk_ref[...]) * (1.0 / HEAD_DIM) + EPS) * gk_ref[...]
    kr = _rope_chunk(kn, cos, sin, first_half)
    kb = kr.astype(bf)
    ones_lane = jnp.where(lane == HEAD_DIM, 1.0, 0.0).astype(bf)
    kc_ref[0, 0] = jnp.where(lane < HEAD_DIM, kb, ones_lane)
    kc_ref[0, 1] = jnp.where(lane < HEAD_DIM, pltpu.roll(kr, HEAD_DIM, 1).astype(bf), ones_lane)
    kf = kb.astype(jnp.float32)
    tile_max = jnp.max(_head_sumsq(kf, bdk_ref[...]), axis=0, keepdims=True)
    first = pl.program_id(1) == 0

    @pl.when(first)
    def _():
        kmax_ref[0] = tile_max

    @pl.when(jnp.logical_not(first))
    def _():
        kmax_ref[0] = jnp.maximum(kmax_ref[0], tile_max)

    vct_ref[0, 0] = proj[:, OFF_VC:OFF_VC + AX_KW].T.astype(bf)


def _in_proj(x, sc, sh, g, w, cos, sin, gq, gk, bdq, bdk):
    B, S, D = x.shape
    tm = TOK_TILE
    bf = jnp.bfloat16
    tok = lambda width: pl.BlockSpec((1, tm, width), lambda b, i: (b, i, 0))
    vec = pl.BlockSpec((1, 1, D), lambda b, i: (b, 0, 0))
    out_shape = (
        jax.ShapeDtypeStruct((B, S, NA_W), bf), jax.ShapeDtypeStruct((B, S, NA_W), bf),
        jax.ShapeDtypeStruct((B, S // LANES, NA_W, LANES), bf),
        jax.ShapeDtypeStruct((B, S, SW_QW), bf), jax.ShapeDtypeStruct((B, S, SW_KW), bf),
        jax.ShapeDtypeStruct((B, S // SW_BLOCK, SW_KW, SW_BLOCK), bf),
        jax.ShapeDtypeStruct((B, AX_QW, S), bf),
        jax.ShapeDtypeStruct((B, AX_KV_HEADS, S, LANES), bf),
        jax.ShapeDtypeStruct((B, S // tm, AX_KW, tm), bf),
        jax.ShapeDtypeStruct((B, 8, S), jnp.float32),
        jax.ShapeDtypeStruct((B, 1, AX_KW), jnp.float32),
    )
    out_specs = (
        tok(NA_W), tok(NA_W),
        pl.BlockSpec((1, tm // LANES, NA_W, LANES), lambda b, i: (b, i, 0, 0)),
        tok(SW_QW), tok(SW_KW),
        pl.BlockSpec((1, tm // SW_BLOCK, SW_KW, SW_BLOCK), lambda b, i: (b, i, 0, 0)),
        pl.BlockSpec((1, AX_QW, tm), lambda b, i: (b, 0, i)),
        pl.BlockSpec((1, AX_KV_HEADS, tm, LANES), lambda b, i: (b, 0, i, 0)),
        pl.BlockSpec((1, 1, AX_KW, tm), lambda b, i: (b, i, 0, 0)),
        pl.BlockSpec((1, 8, tm), lambda b, i: (b, 0, i)),
        pl.BlockSpec((1, 1, AX_KW), lambda b, i: (b, 0, 0)),
    )
    return pl.pallas_call(
        _in_kernel,
        out_shape=out_shape,
        grid=(B, S // tm),
        in_specs=[tok(D), vec, vec, _const_spec((1, D)), _const_spec(w.shape),
                  pl.BlockSpec((tm, LANES), lambda b, i: (i, 0)),
                  pl.BlockSpec((tm, LANES), lambda b, i: (i, 0)),
                  _const_spec((1, LANES)), _const_spec((1, LANES)),
                  _const_spec(bdq.shape), _const_spec(bdk.shape)],
        out_specs=out_specs,
        compiler_params=_cparams("arbitrary", "arbitrary"),
        name="in_proj",
    )(x, sc, sh, g, w, cos, sin, gq, gk, bdq, bdk)


def _na_build_bias(rp_ref, bias_ref, rows):
    shape = (GRID_W, LANES)
    kc = lax.broadcasted_iota(jnp.int32, shape, 0)
    lane = lax.broadcasted_iota(jnp.int32, shape, 1)
    c = lane % GRID_W
    cs = jnp.clip(c - NA_WIN_COLS // 2, 0, GRID_W - NA_WIN_COLS)
    col_ok = (kc >= cs) & (kc < cs + NA_WIN_COLS)
    left = lane < GRID_W
    neg = jnp.full(shape, NEG_INF, jnp.float32)
    tq = NA_QROWS * GRID_W
    for h in range(NA_HEADS):
        pair = []
        for a in range(2 * NA_WIN_ROWS):
            x = jnp.broadcast_to(rp_ref[h, a:a + 1, :] * LOG2E, shape)
            t = pltpu.roll(x, LANES - (NA_WIN_COLS - 1), 1, stride=1, stride_axis=0)
            pair.append(jnp.where(col_ok, t, neg))
        for variant, qr0 in enumerate((0, 2 * NA_QROWS, rows - NA_QROWS)):
            bs = int(np.clip(qr0 - NA_WIN_ROWS // 2, 0, rows - NA_BAND))
            for kj in range(NA_BAND):
                kr = bs + kj
                for u in range(NA_QROWS // 2):
                    r0 = qr0 + 2 * u
                    ok = [int(np.clip(r - NA_WIN_ROWS // 2, 0, rows - NA_WIN_ROWS)) <= kr
                          < int(np.clip(r - NA_WIN_ROWS // 2, 0, rows - NA_WIN_ROWS)) + NA_WIN_ROWS
                          for r in (r0, r0 + 1)]
                    a0 = kr - r0 + NA_WIN_ROWS - 1
                    if ok[0] and ok[1]:
                        tile = pair[a0]
                    elif ok[0]:
                        tile = jnp.where(left, pair[a0], neg)
                    elif ok[1]:
                        tile = jnp.where(left, neg, pair[a0])
                    else:
                        tile = neg
                    bias_ref[variant, kj * GRID_W:(kj + 1) * GRID_W,
                             h * tq + u * LANES:h * tq + (u + 1) * LANES] = tile


def _na_kernel(q_ref, k_ref, vt_ref, rp_ref, o_ref, bias_ref, *, rows):
    i = pl.program_id(1)
    tq = NA_QROWS * GRID_W
    nblk = NA_TQ // tq
    nblk_total = rows // NA_QROWS
    bf = jnp.bfloat16

    @pl.when((pl.program_id(0) == 0) & (i == 0))
    def _():
        _na_build_bias(rp_ref, bias_ref, rows)

    head_of_lane = lax.broadcasted_iota(jnp.int32, (tq, NA_W), 1) // HEAD_DIM
    ones = jnp.ones((16, NA_BAND * GRID_W), bf)

    def band_row(blk):
        return jnp.clip((i * nblk + blk) * NA_QROWS - NA_WIN_ROWS // 2, 0, rows - NA_BAND)

    def scores(blk):
        q = q_ref[0, blk * tq:(blk + 1) * tq, :]
        stack = jnp.concatenate([jnp.where(head_of_lane == h, q, jnp.zeros_like(q))
                                 for h in range(NA_HEADS)], axis=0)
        start = pl.multiple_of(band_row(blk) * GRID_W, NA_QROWS * GRID_W)
        kb = k_ref[0, pl.ds(start, NA_BAND * GRID_W), :]
        return lax.dot_general(kb, stack, (((1,), (1,)), ((), ())),
                               preferred_element_type=jnp.float32)

    pending = [scores(blk) for blk in range(min(NA_LOOKAHEAD, nblk))]
    for blk in range(nblk):
        s = pending.pop(0)
        if blk + NA_LOOKAHEAD < nblk:
            pending.append(scores(blk + NA_LOOKAHEAD))
        n = i * nblk + blk
        variant = jnp.where(n == 0, 0, jnp.where(n == nblk_total - 1, 2, 1))
        s = s + bias_ref[variant]
        m = jnp.max(s, axis=0, keepdims=True)
        p = jnp.exp2(s - m).astype(bf)
        first = band_row(blk) * GRID_W // LANES
        vt = jnp.concatenate([vt_ref[0, first + t] for t in range(NA_BAND * GRID_W // LANES)],
                             axis=1)
        outs = []
        for h in range(NA_HEADS):
            vt_ext = jnp.concatenate([vt[h * HEAD_DIM:(h + 1) * HEAD_DIM], ones], axis=0)
            o = jnp.dot(vt_ext, p[:, h * tq:(h + 1) * tq], preferred_element_type=jnp.float32)
            outs.append(o[:HEAD_DIM] / o[HEAD_DIM:HEAD_DIM + 1])
        o_ref[0, blk * tq:(blk + 1) * tq, :] = jnp.concatenate(outs, axis=0).T


def _na_attention(q, k, vt, rp):
    B, S, W = q.shape
    rows = S // GRID_W
    tq = NA_QROWS * GRID_W
    return pl.pallas_call(
        functools.partial(_na_kernel, rows=rows),
        out_shape=jax.ShapeDtypeStruct((B, S, W), jnp.float32),
        grid=(B, S // NA_TQ),
        in_specs=[pl.BlockSpec((1, NA_TQ, W), lambda b, j: (b, j, 0)),
                  pl.BlockSpec((1, S, W), lambda b, j: (b, 0, 0), pipeline_mode=pl.Buffered(1)),
                  pl.BlockSpec((1, S // LANES, W, LANES), lambda b, j: (b, 0, 0, 0),
                               pipeline_mode=pl.Buffered(1)),
                  _const_spec(rp.shape)],
        out_specs=pl.BlockSpec((1, NA_TQ, W), lambda b, j: (b, j, 0)),
        scratch_shapes=[pltpu.VMEM((3, NA_BAND * GRID_W, NA_HEADS * tq), jnp.float32)],
        compiler_params=_cparams("arbitrary", "arbitrary"),
        name="na_attn",
    )(q, k, vt, rp)


def _na_rpb_rows(rpb):
    n_col = rpb.shape[-1]
    p = jnp.pad(rpb.astype(jnp.float32)[:, :, ::-1], ((0, 0), (1, 1), (0, GRID_W - n_col)))
    return jnp.concatenate([p[:, 1:], p[:, :-1]], axis=-1)


def _sw_kernel(sink_ref, t5_ref, q_ref, k_ref, vt_ref, bucket_ref, o_ref, bias_ref, *, seq):
    i = pl.program_id(1)
    nblk_total = seq // SW_BLOCK

    @pl.when((pl.program_id(0) == 0) & (i == 0))
    def _():
        bucket = bucket_ref[...]
        for h in range(SW_HEADS):
            acc = jnp.full(bucket.shape, NEG_INF, jnp.float32)
            for b in range(T5_BUCKETS):
                acc = jnp.where(bucket == b, t5_ref[b * SW_HEADS + h] * LOG2E, acc)
            bias_ref[:, h * SW_BLOCK:(h + 1) * SW_BLOCK] = acc

    nblk = SW_TQ // SW_BLOCK
    rep = SW_HEADS // SW_KV_HEADS
    bf = jnp.bfloat16
    width = SW_HEADS * SW_BLOCK
    lane_group = lax.broadcasted_iota(jnp.int32, (SW_BLOCK, LANES), 1) // HEAD_DIM
    row_group = lax.broadcasted_iota(jnp.int32, (SW_BLOCK, LANES), 0) // HEAD_DIM
    key_row = lax.broadcasted_iota(jnp.int32, (3 * SW_BLOCK, width), 0)
    sink = jnp.concatenate([jnp.full((1, SW_BLOCK), sink_ref[h] * LOG2E, jnp.float32)
                            for h in range(SW_HEADS)], axis=1)
    ones = jnp.ones((16, 3 * SW_BLOCK), bf)

    def neighbours(blk):
        n = i * nblk + blk
        return jnp.maximum(n - 1, 0), n, jnp.minimum(n + 1, nblk_total - 1)

    def scores(blk):
        rows = slice(blk * SW_BLOCK, (blk + 1) * SW_BLOCK)
        stack = []
        for g in range(SW_KV_HEADS):
            for r in range(rep):
                qcol = q_ref[0, rows, r * LANES:(r + 1) * LANES]
                stack.append(jnp.where(lane_group == g, qcol, jnp.zeros_like(qcol)))
        kw = jnp.concatenate([k_ref[0, pl.ds(pl.multiple_of(nb * SW_BLOCK, SW_BLOCK), SW_BLOCK), :]
                              for nb in neighbours(blk)], axis=0)
        return lax.dot_general(kw, jnp.concatenate(stack, axis=0), (((1,), (1,)), ((), ())),
                               preferred_element_type=jnp.float32)

    pending = [scores(blk) for blk in range(min(SW_LOOKAHEAD, nblk))]
    for blk in range(nblk):
        s = pending.pop(0)
        if blk + SW_LOOKAHEAD < nblk:
            pending.append(scores(blk + SW_LOOKAHEAD))
        s = s + bias_ref[...]
        left, n, right = neighbours(blk)
        if blk in (0, nblk - 1):
            outside = (((n == 0) & (key_row < SW_BLOCK))
                       | ((n == nblk_total - 1) & (key_row >= 2 * SW_BLOCK)))
            s = jnp.where(outside, NEG_INF, s)
        m = jnp.maximum(jnp.max(s, axis=0, keepdims=True), sink)
        p = jnp.exp2(s - m).astype(bf)
        vt_ext = jnp.concatenate(
            [jnp.concatenate([vt_ref[0, left], vt_ref[0, n], vt_ref[0, right]], axis=1), ones], axis=0)
        o = jnp.dot(vt_ext, p, preferred_element_type=jnp.float32)
        res = o[:LANES] / (o[LANES:LANES + 1] + jnp.exp2(sink - m))
        for r in range(rep):
            yt = jnp.where(row_group == 0, res[:, r * SW_BLOCK:(r + 1) * SW_BLOCK],
                           res[:, (rep + r) * SW_BLOCK:(rep + r + 1) * SW_BLOCK])
            o_ref[0, blk * SW_BLOCK:(blk + 1) * SW_BLOCK, r * LANES:(r + 1) * LANES] = yt.T


def _sw_attention(q, k, vt, bucket, t5_flat, sink):
    B, S, _ = q.shape
    nb = S // SW_BLOCK
    return pl.pallas_call(
        functools.partial(_sw_kernel, seq=S),
        out_shape=jax.ShapeDtypeStruct((B, S, SW_QW), jnp.float32),
        grid_spec=pltpu.PrefetchScalarGridSpec(
            num_scalar_prefetch=2,
            grid=(B, S // SW_TQ),
            in_specs=[pl.BlockSpec((1, SW_TQ, SW_QW), lambda b, i, *_: (b, i, 0)),
                      pl.BlockSpec((1, S, SW_KW), lambda b, i, *_: (b, 0, 0),
                                   pipeline_mode=pl.Buffered(1)),
                      pl.BlockSpec((1, nb, SW_KW, SW_BLOCK), lambda b, i, *_: (b, 0, 0, 0),
                                   pipeline_mode=pl.Buffered(1)),
                      pl.BlockSpec(bucket.shape, lambda b, i, *_: (0, 0),
                                   pipeline_mode=pl.Buffered(1))],
            out_specs=pl.BlockSpec((1, SW_TQ, SW_QW), lambda b, i, *_: (b, i, 0)),
            scratch_shapes=[pltpu.VMEM((3 * SW_BLOCK, SW_HEADS * SW_BLOCK), jnp.float32)],
        ),
        compiler_params=_cparams("arbitrary", "arbitrary"),
        name="sw_attn",
    )(sink, t5_flat, q, k, vt, bucket)


def _t5_bucket(rel):
    nb = T5_BUCKETS // 2
    ret = (rel > 0).astype(jnp.int32) * nb
    n = jnp.abs(rel)
    max_exact = nb // 2
    nf = jnp.maximum(n, max_exact).astype(jnp.float32)
    large = max_exact + (jnp.log(nf / max_exact) / math.log(T5_MAX_DIST / max_exact)
                         * (nb - max_exact)).astype(jnp.int32)
    large = jnp.minimum(large, nb - 1)
    return ret + jnp.where(n < max_exact, n, large)


def _sw_bucket_tile():
    qpos = jnp.arange(SW_BLOCK)
    kpos = jnp.arange(3 * SW_BLOCK) - SW_BLOCK
    rel = kpos[:, None] - qpos[None, :]
    return jnp.where(jnp.abs(rel) <= SW_RADIUS, _t5_bucket(rel), -1).astype(jnp.int32)


def _ax_kernel(qt_ref, qnorm_ref, kmax_ref, k_ref, vt_ref, o_ref, qpad_ref, m_ref, l_ref, acc_ref,
               s0_ref, *, nkt, bounded):
    tq = qt_ref.shape[2]
    tk = AX_TK
    rep = AX_HEADS // AX_KV_HEADS
    bf = jnp.bfloat16
    row = lax.broadcasted_iota(jnp.int32, (16, tq), 0)
    for h in range(AX_HEADS):
        g = h // rep
        qth = qt_ref[0, h * HEAD_DIM:(h + 1) * HEAD_DIM, :]
        if bounded:
            shift = qnorm_ref[0, h:h + 1, :] * jnp.sqrt(kmax_ref[0, :, g * HEAD_DIM:g * HEAD_DIM + 1])
            extra = jnp.where(row == 0, -shift, 0.0).astype(bf)
        else:
            extra = jnp.zeros((16, tq), bf)
        qpad_ref[h] = jnp.concatenate(
            [qth, extra, jnp.zeros((LANES - HEAD_DIM - 16, tq), bf)], axis=0)
    if not bounded:
        m_ref[...] = jnp.full(m_ref.shape, -jnp.inf, jnp.float32)
    l_ref[...] = jnp.zeros(l_ref.shape, jnp.float32)
    acc_ref[...] = jnp.zeros(acc_ref.shape, jnp.float32)

    def score(kt, h):
        ks = pl.multiple_of(kt * tk, tk)
        return jnp.dot(k_ref[0, h // rep, pl.ds(ks, tk), :], qpad_ref[h],
                       preferred_element_type=jnp.float32)

    for h in range(AX_LOOKAHEAD):
        s0_ref[h] = score(0, h)

    def body(kt, carry):
        pending = [s0_ref[h] for h in range(AX_LOOKAHEAD)]
        for h in range(AX_HEADS):
            g = h // rep
            s = pending.pop(0)
            ahead = h + AX_LOOKAHEAD
            if ahead < AX_HEADS:
                pending.append(score(kt, ahead))
            else:
                s0_ref[ahead - AX_HEADS] = score(jnp.minimum(kt + 1, nkt - 1), ahead - AX_HEADS)
            vt = vt_ref[0, kt, g * HEAD_DIM:(g + 1) * HEAD_DIM, :]
            if bounded:
                p = jnp.exp2(s)
                l_ref[h] += jnp.sum(p.reshape(tk // 8, 8, tq), axis=0)
                acc_ref[h] += jnp.dot(vt, p.astype(bf), preferred_element_type=jnp.float32)
            else:
                m_old = m_ref[h]
                m_new = jnp.maximum(m_old, jnp.max(s, axis=0, keepdims=True))
                alpha = jnp.exp2(m_old - m_new)
                p = jnp.exp2(s - m_new)
                l_ref[h] = alpha * l_ref[h] + jnp.sum(p.reshape(tk // 8, 8, tq), axis=0)
                m_ref[h] = m_new
                pv = jnp.dot(vt, p.astype(bf), preferred_element_type=jnp.float32)
                acc_ref[h] = alpha * acc_ref[h] + pv
        return carry

    lax.fori_loop(0, nkt, body, 0, unroll=AX_UNROLL)
    out_t = jnp.concatenate([acc_ref[h] / jnp.sum(l_ref[h], axis=0, keepdims=True)
                             for h in range(AX_HEADS)], axis=0)
    o_ref[0] = out_t.T


def _ax_attention(qt, qnorm, kmax, k, vt, bounded):
    B, W, S = qt.shape
    nkt = vt.shape[1]
    return pl.pallas_call(
        functools.partial(_ax_kernel, nkt=nkt, bounded=bounded),
        out_shape=jax.ShapeDtypeStruct((B, S, W), jnp.float32),
        grid=(B, S // AX_TQ),
        in_specs=[pl.BlockSpec((1, W, AX_TQ), lambda b, i: (b, 0, i)),
                  pl.BlockSpec((1, 8, AX_TQ), lambda b, i: (b, 0, i)),
                  pl.BlockSpec((1, 1, AX_KW), lambda b, i: (b, 0, 0)),
                  pl.BlockSpec((1, AX_KV_HEADS, S, LANES), lambda b, i: (b, 0, 0, 0),
                               pipeline_mode=pl.Buffered(1)),
                  pl.BlockSpec((1, nkt, AX_KW, AX_TK), lambda b, i: (b, 0, 0, 0),
                               pipeline_mode=pl.Buffered(1))],
        out_specs=pl.BlockSpec((1, AX_TQ, W), lambda b, i: (b, i, 0)),
        scratch_shapes=[pltpu.VMEM((AX_HEADS, LANES, AX_TQ), jnp.bfloat16),
                        pltpu.VMEM((AX_HEADS, 1, AX_TQ), jnp.float32),
                        pltpu.VMEM((AX_HEADS, 8, AX_TQ), jnp.float32),
                        pltpu.VMEM((AX_HEADS, HEAD_DIM, AX_TQ), jnp.float32),
                        pltpu.VMEM((AX_LOOKAHEAD, AX_TK, AX_TQ), jnp.float32)],
        compiler_params=_cparams("arbitrary", "arbitrary"),
        name="ax_attn_bounded" if bounded else "ax_attn_online",
    )(qt, qnorm, kmax, k, vt)


def _ax_dispatch(qt, qnorm, kmax, k, vt):
    rep = AX_HEADS // AX_KV_HEADS
    knorm = jnp.sqrt(kmax[:, 0, ::HEAD_DIM])
    shift_max = jnp.max(qnorm[:, :AX_HEADS], axis=-1) * jnp.repeat(knorm, rep, axis=1)
    safe = jnp.max(shift_max) <= AX_MAX_SHIFT
    return lax.cond(safe,
                    functools.partial(_ax_attention, bounded=True),
                    functools.partial(_ax_attention, bounded=False),
                    qt, qnorm, kmax, k, vt)


def _post_kernel(x_ref, ya_ref, yb_ref, yc_ref, gg_ref, wo_ref, gta_ref, gf_ref, scf_ref, shf_ref,
                 gtf_ref, wgu_ref, wd_ref, gfin_ref, o_ref, *, final):
    bf = jnp.bfloat16
    gg = gg_ref[...]
    y = jnp.concatenate([
        _rms(ya_ref[0], gg[:, 0:NA_W]).astype(bf),
        _rms(yb_ref[0], gg[:, NA_W:NA_W + SW_QW]).astype(bf),
        _rms(yc_ref[0], gg[:, NA_W + SW_QW:]).astype(bf)], axis=1)
    x1 = x_ref[0] + gta_ref[0] * jnp.dot(y, wo_ref[...], preferred_element_type=jnp.float32)

    h = (_rms(x1, gf_ref[...]) * (1.0 + scf_ref[0]) + shf_ref[0]).astype(bf)
    acc = jnp.zeros(x1.shape, jnp.float32)
    for c0 in range(0, FFN_HIDDEN, FFN_CHUNK):
        c1 = min(c0 + FFN_CHUNK, FFN_HIDDEN)
        gate = jnp.dot(h, wgu_ref[:, c0:c1], preferred_element_type=jnp.float32)
        up = jnp.dot(h, wgu_ref[:, FFN_HIDDEN + c0:FFN_HIDDEN + c1],
                     preferred_element_type=jnp.float32)
        act = (gate * (1.0 / (1.0 + jnp.exp(-gate))) * up).astype(bf)
        acc = acc + jnp.dot(act, wd_ref[c0:c1, :], preferred_element_type=jnp.float32)
    x2 = x1 + gtf_ref[0] * acc
    if final:
        x2 = _rms(x2, gfin_ref[...])
    o_ref[0] = x2


def _post(x, ya, yb, yc, gg, wo, gta, gf, scf, shf, gtf, wgu, wd, gfin, final):
    B, S, D = x.shape
    tm = TOK_TILE
    tok = lambda width: pl.BlockSpec((1, tm, width), lambda b, i: (b, i, 0))
    vec = pl.BlockSpec((1, 1, D), lambda b, i: (b, 0, 0))
    return pl.pallas_call(
        functools.partial(_post_kernel, final=final),
        out_shape=jax.ShapeDtypeStruct((B, S, D), jnp.float32),
        grid=(B, S // tm),
        in_specs=[tok(D), tok(NA_W), tok(SW_QW), tok(AX_QW), _const_spec((1, D)),
                  _const_spec(wo.shape), vec, _const_spec((1, D)), vec, vec, vec,
                  _const_spec(wgu.shape), _const_spec(wd.shape), _const_spec((1, D))],
        out_specs=tok(D),
        compiler_params=_cparams("arbitrary", "arbitrary"),
        name="out_proj_ffn",
    )(x, ya, yb, yc, gg, wo, gta, gf, scf, shf, gtf, wgu, wd, gfin)


def _rope_tables(S):
    t = jnp.arange(S)
    row = (t // GRID_W).astype(jnp.float32)
    col = (t % GRID_W).astype(jnp.float32)
    axis_dim = HEAD_DIM // 2
    freqs = ROPE_THETA ** (-jnp.arange(0, axis_dim, 2, dtype=jnp.float32) / axis_dim)
    ang = jnp.stack([row[:, None] * freqs, col[:, None] * freqs], axis=1)
    cos = jnp.cos(ang)
    sin = jnp.sin(ang)
    cos_h = jnp.concatenate([cos, cos], axis=-1).reshape(S, HEAD_DIM)
    sin_h = jnp.concatenate([-sin, sin], axis=-1).reshape(S, HEAD_DIM)
    return jnp.tile(cos_h, (1, LANES // HEAD_DIM)), jnp.tile(sin_h, (1, LANES // HEAD_DIM))


def _block_diag_ones(width):
    idx = np.arange(width) // HEAD_DIM
    return jnp.asarray(idx[:, None] == idx[None, :], dtype=jnp.bfloat16)


_SW_PERM = np.concatenate([np.arange(HEAD_DIM) + (g * (SW_HEADS // SW_KV_HEADS) + r) * HEAD_DIM
                           for r in range(SW_HEADS // SW_KV_HEADS) for g in range(SW_KV_HEADS)])


def kernel(x, c, w_mod, b_mod, g_attn, w_in, rpb_na, sink_sw, t5_table, gq_ax, gk_ax, g_group,
           w_o, g_ffn, w_gu, w_down, g_final):
    B, S, D = x.shape
    L = w_mod.shape[0]
    bf = jnp.bfloat16
    rows = S // GRID_W

    c_pad = jnp.pad(c, ((0, 8 - B), (0, 0)))
    mod = _modulation(c_pad, w_mod, b_mod)[:, :B]
    mod = mod.reshape(L, B, 6, 1, D)

    cos, sin = _rope_tables(S)
    bdq = _block_diag_ones(AX_QW)
    bdk = _block_diag_ones(AX_KW)
    sw_bucket = _sw_bucket_tile()
    t5_flat = t5_table.astype(jnp.float32).reshape(-1)

    in_cols = np.arange(IN_WIDTH)
    in_cols[OFF_QB:OFF_QB + SW_QW] = OFF_QB + _SW_PERM
    group_perm = np.arange(D)
    group_perm[NA_W:NA_W + SW_QW] = NA_W + _SW_PERM

    for l in range(L):
        sh_a, sc_a, gt_a, sh_f, sc_f, gt_f = [mod[l, :, i] for i in range(6)]
        w_in_l = w_in[l][:, in_cols].astype(bf)
        qa, ka, va, qb, kb, vb, qct, kc, vct, qnorm, kmax = _in_proj(
            x, sc_a, sh_a, g_attn[l].reshape(1, D), w_in_l, cos, sin,
            jnp.tile(gq_ax[l], LANES // HEAD_DIM).reshape(1, LANES),
            jnp.tile(gk_ax[l], LANES // HEAD_DIM).reshape(1, LANES), bdq, bdk)
        ya = _na_attention(qa, ka, va, _na_rpb_rows(rpb_na[l]))
        yb = _sw_attention(qb, kb, vb, sw_bucket, t5_flat, sink_sw[l])
        yc = _ax_dispatch(qct, qnorm, kmax, kc, vct)
        x = _post(x, ya, yb, yc, g_group[l][group_perm].reshape(1, D),
                  w_o[l][group_perm, :].astype(bf), gt_a, g_ffn[l].reshape(1, D), sc_f, sh_f, gt_f,
                  w_gu[l].astype(bf), w_down[l].astype(bf), g_final.reshape(1, D),
                  final=(l == L - 1))
    return x
```

```python
import functools
import math

import jax
import jax.numpy as jnp
import numpy as np
from jax import lax
from jax.experimental import pallas as pl
from jax.experimental.pallas import tpu as pltpu

D_MODEL = 1024
HEAD_DIM = 64
GRID_W = 64
NA_HEADS = 4
SW_HEADS = 6
SW_KV_HEADS = 2
AX_HEADS = 6
AX_KV_HEADS = 2
NA_WIN_ROWS = 8
NA_WIN_COLS = 16
SW_RADIUS = 128
SW_BLOCK = 128
T5_BUCKETS = 32
T5_MAX_DIST = 128
ROPE_THETA = 10000.0
FFN_HIDDEN = 2816
EPS = 1e-6
NEG_INF = -1e30

NA_W = NA_HEADS * HEAD_DIM
SW_QW = SW_HEADS * HEAD_DIM
SW_KW = SW_KV_HEADS * HEAD_DIM
AX_QW = AX_HEADS * HEAD_DIM
AX_KW = AX_KV_HEADS * HEAD_DIM
IN_WIDTH = 3 * NA_W + SW_QW + 2 * SW_KW + AX_QW + 2 * AX_KW
OFF_QA, OFF_KA, OFF_VA = 0, NA_W, 2 * NA_W
OFF_QB = 3 * NA_W
OFF_KB = OFF_QB + SW_QW
OFF_VB = OFF_KB + SW_KW
OFF_QC = OFF_VB + SW_KW
OFF_KC = OFF_QC + AX_QW
OFF_VC = OFF_KC + AX_KW

LANES = 128
VMEM_LIMIT = 56 * 1024 * 1024

TOK_TILE = 512
IN_TILE = 1024
IN_SUB = 256
AX_TQ = 256
AX_TK = 512
AX_LOOKAHEAD = 2
AX_UNROLL = 8
NA_QROWS = 4
NA_BAND = NA_QROWS + NA_WIN_ROWS
NA_TQ = 1024
NA_LOOKAHEAD = 1
SW_TQ = 1024
SW_LOOKAHEAD = 2
FFN_CHUNK = 512

QK_SCALE = HEAD_DIM ** -0.5
LOG2E = math.log2(math.e)
Q_SCALE_EXP2 = QK_SCALE * LOG2E
AX_MAX_SHIFT = 60.0


def _cparams(*sem):
    return pltpu.CompilerParams(dimension_semantics=sem, vmem_limit_bytes=VMEM_LIMIT)


def _const_spec(shape):
    n = len(shape)
    return pl.BlockSpec(shape, lambda *_: (0,) * n, pipeline_mode=pl.Buffered(1))


def _rms(x, g):
    return x * lax.rsqrt(jnp.mean(x * x, axis=-1, keepdims=True) + EPS) * g


def _mod_kernel(c_ref, w_ref, b_ref, o_ref):
    c = c_ref[...]
    cond = c * (1.0 / (1.0 + jnp.exp(-c)))
    o_ref[0] = jnp.dot(cond, w_ref[0], preferred_element_type=jnp.float32,
                       precision=lax.Precision.HIGHEST) + b_ref[0]


def _modulation(c_pad, w_mod, b_mod):
    L, D, N = w_mod.shape
    tn = 1536
    return pl.pallas_call(
        _mod_kernel,
        out_shape=jax.ShapeDtypeStruct((L, c_pad.shape[0], N), jnp.float32),
        grid=(L, N // tn),
        in_specs=[pl.BlockSpec(c_pad.shape, lambda l, j: (0, 0)),
                  pl.BlockSpec((1, D, tn), lambda l, j: (l, 0, j)),
                  pl.BlockSpec((1, 1, tn), lambda l, j: (l, 0, j))],
        out_specs=pl.BlockSpec((1, c_pad.shape[0], tn), lambda l, j: (l, 0, j)),
        compiler_params=_cparams("arbitrary", "arbitrary"),
        name="adaln_mod",
    )(c_pad, w_mod, b_mod.reshape(L, 1, N))


def _head_sumsq(t, bd):
    t2 = t * t
    hi = t2.astype(jnp.bfloat16)
    lo = (t2 - hi.astype(jnp.float32)).astype(jnp.bfloat16)
    return (jnp.dot(hi, bd, preferred_element_type=jnp.float32)
            + jnp.dot(lo, bd, preferred_element_type=jnp.float32))


def _rope_chunk(t, cos, sin_signed, first_half):
    swapped = jnp.where(first_half, pltpu.roll(t, LANES - 16, 1), pltpu.roll(t, 16, 1))
    return t * cos + swapped * sin_signed


def _in_kernel(x_ref, sc_ref, sh_ref, g_ref, w_ref, cos_ref, sin_ref, gq_ref, gk_ref,
               bdq_ref, bdk_ref,
               qa_ref, ka_ref, vat_ref, qb_ref, kb_ref, vbt_ref, qct_ref, kc_ref, vct_ref,
               qnorm_ref, kmax_ref):
    bf = jnp.bfloat16
    n_sub = x_ref.shape[1] // IN_SUB
    lane = lax.broadcasted_iota(jnp.int32, (IN_SUB, LANES), 1)
    first_half = (lane % 32) < 16
    ones_lane = jnp.where(lane == HEAD_DIM, 1.0, 0.0).astype(bf)

    def normed(j):
        x = x_ref[0, j * IN_SUB:(j + 1) * IN_SUB, :]
        return (_rms(x, g_ref[...]) * (1.0 + sc_ref[0]) + sh_ref[0]).astype(bf)

    def project(h):
        return jnp.dot(h, w_ref[...], preferred_element_type=jnp.float32)

    def head_stats(proj):
        return (_head_sumsq(proj[:, OFF_QC:OFF_QC + AX_QW], bdq_ref[...]),
                _head_sumsq(proj[:, OFF_KC:OFF_KC + AX_KW], bdk_ref[...]))

    def finish(j, proj, stats):
        rows = slice(j * IN_SUB, (j + 1) * IN_SUB)
        qa_ref[0, rows] = (proj[:, OFF_QA:OFF_QA + NA_W] * Q_SCALE_EXP2).astype(bf)
        ka_ref[0, rows] = proj[:, OFF_KA:OFF_KA + NA_W].astype(bf)
        qb_ref[0, rows] = (proj[:, OFF_QB:OFF_QB + SW_QW] * Q_SCALE_EXP2).astype(bf)
        kb_ref[0, rows] = proj[:, OFF_KB:OFF_KB + SW_KW].astype(bf)
        per_sub = IN_SUB // LANES
        for t in range(per_sub):
            blk = slice(t * LANES, (t + 1) * LANES)
            vat_ref[0, j * per_sub + t] = proj[blk, OFF_VA:OFF_VA + NA_W].T.astype(bf)
            vbt_ref[0, j * per_sub + t] = proj[blk, OFF_VB:OFF_VB + SW_KW].T.astype(bf)

        cos = cos_ref[rows, :]
        sin = sin_ref[rows, :]
        qss, kss = stats
        qn = proj[:, OFF_QC:OFF_QC + AX_QW] * lax.rsqrt(qss * (1.0 / HEAD_DIM) + EPS)
        chunks = []
        for c in range(AX_QW // LANES):
            t = qn[:, c * LANES:(c + 1) * LANES] * gq_ref[...]
            chunks.append(_rope_chunk(t, cos, sin, first_half) * Q_SCALE_EXP2)
        qt = jnp.concatenate(chunks, axis=1).T.astype(bf)
        qct_ref[0, :, rows] = qt
        qsq = qt.astype(jnp.float32)
        qsq = qsq * qsq
        norms = [jnp.sqrt(jnp.sum(qsq[hh * HEAD_DIM:(hh + 1) * HEAD_DIM], axis=0, keepdims=True))
                 for hh in range(AX_HEADS)]
        qnorm_ref[0, :, rows] = jnp.concatenate(
            norms + [jnp.zeros_like(norms[0])] * (8 - AX_HEADS), axis=0)

        kn = (proj[:, OFF_KC:OFF_KC + AX_KW] * lax.rsqrt(kss * (1.0 / HEAD_DIM) + EPS)
              * gk_ref[...])
        kr = _rope_chunk(kn, cos, sin, first_half)
        kb = kr.astype(bf)
        kc_ref[0, 0, rows] = jnp.where(lane < HEAD_DIM, kb, ones_lane)
        kc_ref[0, 1, rows] = jnp.where(lane < HEAD_DIM, pltpu.roll(kr, HEAD_DIM, 1).astype(bf),
                                       ones_lane)
        first_tk = (j * IN_SUB) // AX_TK
        off = (j * IN_SUB) % AX_TK
        vct_ref[0, first_tk, :, off:off + IN_SUB] = proj[:, OFF_VC:OFF_VC + AX_KW].T.astype(bf)

        kf = kb.astype(jnp.float32)
        kf2 = kf * kf
        lo = jnp.max(jnp.sum(jnp.where(lane < HEAD_DIM, kf2, 0.0), axis=1, keepdims=True),
                     axis=0, keepdims=True)
        hi = jnp.max(jnp.sum(jnp.where(lane < HEAD_DIM, 0.0, kf2), axis=1, keepdims=True),
                     axis=0, keepdims=True)
        return jnp.where(lane[:1] < HEAD_DIM, lo, hi)

    proj = project(normed(0))
    tile_max = None
    for j in range(n_sub):
        if j + 1 < n_sub:
            h_next = normed(j + 1)
        stats = head_stats(proj)
        if j + 1 < n_sub:
            proj_next = project(h_next)
        sub_max = finish(j, proj, stats)
        tile_max = sub_max if tile_max is None else jnp.maximum(tile_max, sub_max)
        if j + 1 < n_sub:
            proj = proj_next

    first = pl.program_id(1) == 0

    @pl.when(first)
    def _():
        kmax_ref[0] = tile_max

    @pl.when(jnp.logical_not(first))
    def _():
        kmax_ref[0] = jnp.maximum(kmax_ref[0], tile_max)


def _in_proj(x, sc, sh, g, w, cos, sin, gq, gk, bdq, bdk):
    B, S, D = x.shape
    tm = IN_TILE
    bf = jnp.bfloat16
    tok = lambda width: pl.BlockSpec((1, tm, width), lambda b, i: (b, i, 0))
    vec = pl.BlockSpec((1, 1, D), lambda b, i: (b, 0, 0))
    out_shape = (
        jax.ShapeDtypeStruct((B, S, NA_W), bf), jax.ShapeDtypeStruct((B, S, NA_W), bf),
        jax.ShapeDtypeStruct((B, S // LANES, NA_W, LANES), bf),
        jax.ShapeDtypeStruct((B, S, SW_QW), bf), jax.ShapeDtypeStruct((B, S, SW_KW), bf),
        jax.ShapeDtypeStruct((B, S // SW_BLOCK, SW_KW, SW_BLOCK), bf),
        jax.ShapeDtypeStruct((B, AX_QW, S), bf),
        jax.ShapeDtypeStruct((B, AX_KV_HEADS, S, LANES), bf),
        jax.ShapeDtypeStruct((B, S // AX_TK, AX_KW, AX_TK), bf),
        jax.ShapeDtypeStruct((B, 8, S), jnp.float32),
        jax.ShapeDtypeStruct((B, 1, AX_KW), jnp.float32),
    )
    out_specs = (
        tok(NA_W), tok(NA_W),
        pl.BlockSpec((1, tm // LANES, NA_W, LANES), lambda b, i: (b, i, 0, 0)),
        tok(SW_QW), tok(SW_KW),
        pl.BlockSpec((1, tm // SW_BLOCK, SW_KW, SW_BLOCK), lambda b, i: (b, i, 0, 0)),
        pl.BlockSpec((1, AX_QW, tm), lambda b, i: (b, 0, i)),
        pl.BlockSpec((1, AX_KV_HEADS, tm, LANES), lambda b, i: (b, 0, i, 0)),
        pl.BlockSpec((1, tm // AX_TK, AX_KW, AX_TK), lambda b, i: (b, i, 0, 0)),
        pl.BlockSpec((1, 8, tm), lambda b, i: (b, 0, i)),
        pl.BlockSpec((1, 1, AX_KW), lambda b, i: (b, 0, 0)),
    )
    return pl.pallas_call(
        _in_kernel,
        out_shape=out_shape,
        grid=(B, S // tm),
        in_specs=[tok(D), vec, vec, _const_spec((1, D)), _const_spec(w.shape),
                  pl.BlockSpec((tm, LANES), lambda b, i: (i, 0)),
                  pl.BlockSpec((tm, LANES), lambda b, i: (i, 0)),
                  _const_spec((1, LANES)), _const_spec((1, LANES)),
                  _const_spec(bdq.shape), _const_spec(bdk.shape)],
        out_specs=out_specs,
        compiler_params=_cparams("arbitrary", "arbitrary"),
        name="in_proj",
    )(x, sc, sh, g, w, cos, sin, gq, gk, bdq, bdk)


def _na_build_bias(rp_ref, bias_ref, rows):
    shape = (GRID_W, LANES)
    kc = lax.broadcasted_iota(jnp.int32, shape, 0)
    lane = lax.broadcasted_iota(jnp.int32, shape, 1)
    c = lane % GRID_W
    cs = jnp.clip(c - NA_WIN_COLS // 2, 0, GRID_W - NA_WIN_COLS)
    col_ok = (kc >= cs) & (kc < cs + NA_WIN_COLS)
    left = lane < GRID_W
    neg = jnp.full(shape, NEG_INF, jnp.float32)
    tq = NA_QROWS * GRID_W
    for h in range(NA_HEADS):
        pair = []
        for a in range(2 * NA_WIN_ROWS):
            x = jnp.broadcast_to(rp_ref[h, a:a + 1, :] * LOG2E, shape)
            t = pltpu.roll(x, LANES - (NA_WIN_COLS - 1), 1, stride=1, stride_axis=0)
            pair.append(jnp.where(col_ok, t, neg))
        for variant, qr0 in enumerate((0, 2 * NA_QROWS, rows - NA_QROWS)):
            bs = int(np.clip(qr0 - NA_WIN_ROWS // 2, 0, rows - NA_BAND))
            for kj in range(NA_BAND):
                kr = bs + kj
                for u in range(NA_QROWS // 2):
                    r0 = qr0 + 2 * u
                    ok = [int(np.clip(r - NA_WIN_ROWS // 2, 0, rows - NA_WIN_ROWS)) <= kr
                          < int(np.clip(r - NA_WIN_ROWS // 2, 0, rows - NA_WIN_ROWS)) + NA_WIN_ROWS
                          for r in (r0, r0 + 1)]
                    a0 = kr - r0 + NA_WIN_ROWS - 1
                    if ok[0] and ok[1]:
                        tile = pair[a0]
                    elif ok[0]:
                        tile = jnp.where(left, pair[a0], neg)
                    elif ok[1]:
                        tile = jnp.where(left, neg, pair[a0])
                    else:
                        tile = neg
                    bias_ref[variant, kj * GRID_W:(kj + 1) * GRID_W,
                             h * tq + u * LANES:h * tq + (u + 1) * LANES] = tile


def _na_kernel(q_ref, k_ref, vt_ref, rp_ref, o_ref, bias_ref, *, rows):
    i = pl.program_id(1)
    tq = NA_QROWS * GRID_W
    nblk = NA_TQ // tq
    nblk_total = rows // NA_QROWS
    bf = jnp.bfloat16

    @pl.when((pl.program_id(0) == 0) & (i == 0))
    def _():
        _na_build_bias(rp_ref, bias_ref, rows)

    head_of_lane = lax.broadcasted_iota(jnp.int32, (tq, NA_W), 1) // HEAD_DIM
    ones = jnp.ones((16, NA_BAND * GRID_W), bf)

    def band_row(blk):
        return jnp.clip((i * nblk + blk) * NA_QROWS - NA_WIN_ROWS // 2, 0, rows - NA_BAND)

    def scores(blk):
        q = q_ref[0, blk * tq:(blk + 1) * tq, :]
        stack = jnp.concatenate([jnp.where(head_of_lane == h, q, jnp.zeros_like(q))
                                 for h in range(NA_HEADS)], axis=0)
        start = pl.multiple_of(band_row(blk) * GRID_W, NA_QROWS * GRID_W)
        kb = k_ref[0, pl.ds(start, NA_BAND * GRID_W), :]
        return lax.dot_general(kb, stack, (((1,), (1,)), ((), ())),
                               preferred_element_type=jnp.float32)

    pending = [scores(blk) for blk in range(min(NA_LOOKAHEAD, nblk))]
    for blk in range(nblk):
        s = pending.pop(0)
        if blk + NA_LOOKAHEAD < nblk:
            pending.append(scores(blk + NA_LOOKAHEAD))
        n = i * nblk + blk
        variant = jnp.where(n == 0, 0, jnp.where(n == nblk_total - 1, 2, 1))
        s = s + bias_ref[variant]
        m = jnp.max(s, axis=0, keepdims=True)
        p = jnp.exp2(s - m).astype(bf)
        first = band_row(blk) * GRID_W // LANES
        vt = jnp.concatenate([vt_ref[0, first + t] for t in range(NA_BAND * GRID_W // LANES)],
                             axis=1)
        outs = []
        for h in range(NA_HEADS):
            vt_ext = jnp.concatenate([vt[h * HEAD_DIM:(h + 1) * HEAD_DIM], ones], axis=0)
            o = jnp.dot(vt_ext, p[:, h * tq:(h + 1) * tq], preferred_element_type=jnp.float32)
            outs.append(o[:HEAD_DIM] / o[HEAD_DIM:HEAD_DIM + 1])
        o_ref[0, blk * tq:(blk + 1) * tq, :] = jnp.concatenate(outs, axis=0).T


def _na_attention(q, k, vt, rp):
    B, S, W = q.shape
    rows = S // GRID_W
    tq = NA_QROWS * GRID_W
    return pl.pallas_call(
        functools.partial(_na_kernel, rows=rows),
        out_shape=jax.ShapeDtypeStruct((B, S, W), jnp.float32),
        grid=(B, S // NA_TQ),
        in_specs=[pl.BlockSpec((1, NA_TQ, W), lambda b, j: (b, j, 0)),
                  pl.BlockSpec((1, S, W), lambda b, j: (b, 0, 0), pipeline_mode=pl.Buffered(1)),
                  pl.BlockSpec((1, S // LANES, W, LANES), lambda b, j: (b, 0, 0, 0),
                               pipeline_mode=pl.Buffered(1)),
                  _const_spec(rp.shape)],
        out_specs=pl.BlockSpec((1, NA_TQ, W), lambda b, j: (b, j, 0)),
        scratch_shapes=[pltpu.VMEM((3, NA_BAND * GRID_W, NA_HEADS * tq), jnp.float32)],
        compiler_params=_cparams("arbitrary", "arbitrary"),
        name="na_attn",
    )(q, k, vt, rp)


def _na_rpb_rows(rpb):
    n_col = rpb.shape[-1]
    p = jnp.pad(rpb.astype(jnp.float32)[:, :, ::-1], ((0, 0), (1, 1), (0, GRID_W - n_col)))
    return jnp.concatenate([p[:, 1:], p[:, :-1]], axis=-1)


def _sw_kernel(sink_ref, t5_ref, q_ref, k_ref, vt_ref, bucket_ref, o_ref, bias_ref, *, seq):
    i = pl.program_id(1)
    nblk_total = seq // SW_BLOCK

    @pl.when((pl.program_id(0) == 0) & (i == 0))
    def _():
        bucket = bucket_ref[...]
        for h in range(SW_HEADS):
            acc = jnp.full(bucket.shape, NEG_INF, jnp.float32)
            for b in range(T5_BUCKETS):
                acc = jnp.where(bucket == b, t5_ref[b * SW_HEADS + h] * LOG2E, acc)
            bias_ref[:, h * SW_BLOCK:(h + 1) * SW_BLOCK] = acc

    nblk = SW_TQ // SW_BLOCK
    rep = SW_HEADS // SW_KV_HEADS
    bf = jnp.bfloat16
    width = SW_HEADS * SW_BLOCK
    lane_group = lax.broadcasted_iota(jnp.int32, (SW_BLOCK, LANES), 1) // HEAD_DIM
    row_group = lax.broadcasted_iota(jnp.int32, (SW_BLOCK, LANES), 0) // HEAD_DIM
    key_row = lax.broadcasted_iota(jnp.int32, (3 * SW_BLOCK, width), 0)
    sink = jnp.concatenate([jnp.full((1, SW_BLOCK), sink_ref[h] * LOG2E, jnp.float32)
                            for h in range(SW_HEADS)], axis=1)
    ones = jnp.ones((16, 3 * SW_BLOCK), bf)

    def neighbours(blk):
        n = i * nblk + blk
        return jnp.maximum(n - 1, 0), n, jnp.minimum(n + 1, nblk_total - 1)

    def scores(blk):
        rows = slice(blk * SW_BLOCK, (blk + 1) * SW_BLOCK)
        stack = []
        for g in range(SW_KV_HEADS):
            for r in range(rep):
                qcol = q_ref[0, rows, r * LANES:(r + 1) * LANES]
                stack.append(jnp.where(lane_group == g, qcol, jnp.zeros_like(qcol)))
        kw = jnp.concatenate([k_ref[0, pl.ds(pl.multiple_of(nb * SW_BLOCK, SW_BLOCK), SW_BLOCK), :]
                              for nb in neighbours(blk)], axis=0)
        return lax.dot_general(kw, jnp.concatenate(stack, axis=0), (((1,), (1,)), ((), ())),
                               preferred_element_type=jnp.float32)

    pending = [scores(blk) for blk in range(min(SW_LOOKAHEAD, nblk))]
    for blk in range(nblk):
        s = pending.pop(0)
        if blk + SW_LOOKAHEAD < nblk:
            pending.append(scores(blk + SW_LOOKAHEAD))
        s = s + bias_ref[...]
        left, n, right = neighbours(blk)
        if blk in (0, nblk - 1):
            outside = (((n == 0) & (key_row < SW_BLOCK))
                       | ((n == nblk_total - 1) & (key_row >= 2 * SW_BLOCK)))
            s = jnp.where(outside, NEG_INF, s)
        m = jnp.maximum(jnp.max(s, axis=0, keepdims=True), sink)
        p = jnp.exp2(s - m).astype(bf)
        vt_ext = jnp.concatenate(
            [jnp.concatenate([vt_ref[0, left], vt_ref[0, n], vt_ref[0, right]], axis=1), ones], axis=0)
        o = jnp.dot(vt_ext, p, preferred_element_type=jnp.float32)
        res = o[:LANES] / (o[LANES:LANES + 1] + jnp.exp2(sink - m))
        for r in range(rep):
            yt = jnp.where(row_group == 0, res[:, r * SW_BLOCK:(r + 1) * SW_BLOCK],
                           res[:, (rep + r) * SW_BLOCK:(rep + r + 1) * SW_BLOCK])
            o_ref[0, blk * SW_BLOCK:(blk + 1) * SW_BLOCK, r * LANES:(r + 1) * LANES] = yt.T


def _sw_attention(q, k, vt, bucket, t5_flat, sink):
    B, S, _ = q.shape
    nb = S // SW_BLOCK
    return pl.pallas_call(
        functools.partial(_sw_kernel, seq=S),
        out_shape=jax.ShapeDtypeStruct((B, S, SW_QW), jnp.float32),
        grid_spec=pltpu.PrefetchScalarGridSpec(
            num_scalar_prefetch=2,
            grid=(B, S // SW_TQ),
            in_specs=[pl.BlockSpec((1, SW_TQ, SW_QW), lambda b, i, *_: (b, i, 0)),
                      pl.BlockSpec((1, S, SW_KW), lambda b, i, *_: (b, 0, 0),
                                   pipeline_mode=pl.Buffered(1)),
                      pl.BlockSpec((1, nb, SW_KW, SW_BLOCK), lambda b, i, *_: (b, 0, 0, 0),
                                   pipeline_mode=pl.Buffered(1)),
                      pl.BlockSpec(bucket.shape, lambda b, i, *_: (0, 0),
                                   pipeline_mode=pl.Buffered(1))],
            out_specs=pl.BlockSpec((1, SW_TQ, SW_QW), lambda b, i, *_: (b, i, 0)),
            scratch_shapes=[pltpu.VMEM((3 * SW_BLOCK, SW_HEADS * SW_BLOCK), jnp.float32)],
        ),
        compiler_params=_cparams("arbitrary", "arbitrary"),
        name="sw_attn",
    )(sink, t5_flat, q, k, vt, bucket)


def _t5_bucket(rel):
    nb = T5_BUCKETS // 2
    ret = (rel > 0).astype(jnp.int32) * nb
    n = jnp.abs(rel)
    max_exact = nb // 2
    nf = jnp.maximum(n, max_exact).astype(jnp.float32)
    large = max_exact + (jnp.log(nf / max_exact) / math.log(T5_MAX_DIST / max_exact)
                         * (nb - max_exact)).astype(jnp.int32)
    large = jnp.minimum(large, nb - 1)
    return ret + jnp.where(n < max_exact, n, large)


def _sw_bucket_tile():
    qpos = jnp.arange(SW_BLOCK)
    kpos = jnp.arange(3 * SW_BLOCK) - SW_BLOCK
    rel = kpos[:, None] - qpos[None, :]
    return jnp.where(jnp.abs(rel) <= SW_RADIUS, _t5_bucket(rel), -1).astype(jnp.int32)


def _ax_kernel(qt_ref, qnorm_ref, kmax_ref, k_ref, vt_ref, o_ref, qpad_ref, m_ref, l_ref, acc_ref,
               s0_ref, *, nkt, bounded):
    tq = qt_ref.shape[2]
    tk = AX_TK
    rep = AX_HEADS // AX_KV_HEADS
    bf = jnp.bfloat16
    row = lax.broadcasted_iota(jnp.int32, (16, tq), 0)
    for h in range(AX_HEADS):
        g = h // rep
        qth = qt_ref[0, h * HEAD_DIM:(h + 1) * HEAD_DIM, :]
        if bounded:
            shift = qnorm_ref[0, h:h + 1, :] * jnp.sqrt(kmax_ref[0, :, g * HEAD_DIM:g * HEAD_DIM + 1])
            extra = jnp.where(row == 0, -shift, 0.0).astype(bf)
        else:
            extra = jnp.zeros((16, tq), bf)
        qpad_ref[h] = jnp.concatenate(
            [qth, extra, jnp.zeros((LANES - HEAD_DIM - 16, tq), bf)], axis=0)
    if not bounded:
        m_ref[...] = jnp.full(m_ref.shape, -jnp.inf, jnp.float32)
    l_ref[...] = jnp.zeros(l_ref.shape, jnp.float32)
    acc_ref[...] = jnp.zeros(acc_ref.shape, jnp.float32)

    def score(kt, h):
        ks = pl.multiple_of(kt * tk, tk)
        return jnp.dot(k_ref[0, h // rep, pl.ds(ks, tk), :], qpad_ref[h],
                       preferred_element_type=jnp.float32)

    for h in range(AX_LOOKAHEAD):
        s0_ref[h] = score(0, h)

    def body(kt, carry):
        pending = [s0_ref[h] for h in range(AX_LOOKAHEAD)]
        for h in range(AX_HEADS):
            g = h // rep
            s = pending.pop(0)
            ahead = h + AX_LOOKAHEAD
            if ahead < AX_HEADS:
                pending.append(score(kt, ahead))
            else:
                s0_ref[ahead - AX_HEADS] = score(jnp.minimum(kt + 1, nkt - 1), ahead - AX_HEADS)
            vt = vt_ref[0, kt, g * HEAD_DIM:(g + 1) * HEAD_DIM, :]
            if bounded:
                p = jnp.exp2(s)
                l_ref[h] += jnp.sum(p.reshape(tk // 8, 8, tq), axis=0)
                acc_ref[h] += jnp.dot(vt, p.astype(bf), preferred_element_type=jnp.float32)
            else:
                m_old = m_ref[h]
                m_new = jnp.maximum(m_old, jnp.max(s, axis=0, keepdims=True))
                alpha = jnp.exp2(m_old - m_new)
                p = jnp.exp2(s - m_new)
                l_ref[h] = alpha * l_ref[h] + jnp.sum(p.reshape(tk // 8, 8, tq), axis=0)
                m_ref[h] = m_new
                pv = jnp.dot(vt, p.astype(bf), preferred_element_type=jnp.float32)
                acc_ref[h] = alpha * acc_ref[h] + pv
        return carry

    lax.fori_loop(0, nkt, body, 0, unroll=AX_UNROLL)
    out_t = jnp.concatenate([acc_ref[h] / jnp.sum(l_ref[h], axis=0, keepdims=True)
                             for h in range(AX_HEADS)], axis=0)
    o_ref[0] = out_t.T


def _ax_attention(qt, qnorm, kmax, k, vt, bounded):
    B, W, S = qt.shape
    nkt = vt.shape[1]
    return pl.pallas_call(
        functools.partial(_ax_kernel, nkt=nkt, bounded=bounded),
        out_shape=jax.ShapeDtypeStruct((B, S, W), jnp.float32),
        grid=(B, S // AX_TQ),
        in_specs=[pl.BlockSpec((1, W, AX_TQ), lambda b, i: (b, 0, i)),
                  pl.BlockSpec((1, 8, AX_TQ), lambda b, i: (b, 0, i)),
                  pl.BlockSpec((1, 1, AX_KW), lambda b, i: (b, 0, 0)),
                  pl.BlockSpec((1, AX_KV_HEADS, S, LANES), lambda b, i: (b, 0, 0, 0),
                               pipeline_mode=pl.Buffered(1)),
                  pl.BlockSpec((1, nkt, AX_KW, AX_TK), lambda b, i: (b, 0, 0, 0),
                               pipeline_mode=pl.Buffered(1))],
        out_specs=pl.BlockSpec((1, AX_TQ, W), lambda b, i: (b, i, 0)),
        scratch_shapes=[pltpu.VMEM((AX_HEADS, LANES, AX_TQ), jnp.bfloat16),
                        pltpu.VMEM((AX_HEADS, 1, AX_TQ), jnp.float32),
                        pltpu.VMEM((AX_HEADS, 8, AX_TQ), jnp.float32),
                        pltpu.VMEM((AX_HEADS, HEAD_DIM, AX_TQ), jnp.float32),
                        pltpu.VMEM((AX_LOOKAHEAD, AX_TK, AX_TQ), jnp.float32)],
        compiler_params=_cparams("arbitrary", "arbitrary"),
        name="ax_attn_bounded" if bounded else "ax_attn_online",
    )(qt, qnorm, kmax, k, vt)


def _ax_dispatch(qt, qnorm, kmax, k, vt):
    rep = AX_HEADS // AX_KV_HEADS
    knorm = jnp.sqrt(kmax[:, 0, ::HEAD_DIM])
    shift_max = jnp.max(qnorm[:, :AX_HEADS], axis=-1) * jnp.repeat(knorm, rep, axis=1)
    safe = jnp.max(shift_max) <= AX_MAX_SHIFT
    return lax.cond(safe,
                    functools.partial(_ax_attention, bounded=True),
                    functools.partial(_ax_attention, bounded=False),
                    qt, qnorm, kmax, k, vt)


def _post_kernel(x_ref, ya_ref, yb_ref, yc_ref, gg_ref, wo_ref, gta_ref, gf_ref, scf_ref, shf_ref,
                 gtf_ref, wgu_ref, wd_ref, gfin_ref, o_ref, *, final):
    bf = jnp.bfloat16
    gg = gg_ref[...]
    y = jnp.concatenate([
        _rms(ya_ref[0], gg[:, 0:NA_W]).astype(bf),
        _rms(yb_ref[0], gg[:, NA_W:NA_W + SW_QW]).astype(bf),
        _rms(yc_ref[0], gg[:, NA_W + SW_QW:]).astype(bf)], axis=1)
    x1 = x_ref[0] + gta_ref[0] * jnp.dot(y, wo_ref[...], preferred_element_type=jnp.float32)

    h = (_rms(x1, gf_ref[...]) * (1.0 + scf_ref[0]) + shf_ref[0]).astype(bf)
    acc = jnp.zeros(x1.shape, jnp.float32)
    for c0 in range(0, FFN_HIDDEN, FFN_CHUNK):
        c1 = min(c0 + FFN_CHUNK, FFN_HIDDEN)
        gate = jnp.dot(h, wgu_ref[:, c0:c1], preferred_element_type=jnp.float32)
        up = jnp.dot(h, wgu_ref[:, FFN_HIDDEN + c0:FFN_HIDDEN + c1],
                     preferred_element_type=jnp.float32)
        act = (gate * (1.0 / (1.0 + jnp.exp(-gate))) * up).astype(bf)
        acc = acc + jnp.dot(act, wd_ref[c0:c1, :], preferred_element_type=jnp.float32)
    x2 = x1 + gtf_ref[0] * acc
    if final:
        x2 = _rms(x2, gfin_ref[...])
    o_ref[0] = x2


def _post(x, ya, yb, yc, gg, wo, gta, gf, scf, shf, gtf, wgu, wd, gfin, final):
    B, S, D = x.shape
    tm = TOK_TILE
    tok = lambda width: pl.BlockSpec((1, tm, width), lambda b, i: (b, i, 0))
    vec = pl.BlockSpec((1, 1, D), lambda b, i: (b, 0, 0))
    return pl.pallas_call(
        functools.partial(_post_kernel, final=final),
        out_shape=jax.ShapeDtypeStruct((B, S, D), jnp.float32),
        grid=(B, S // tm),
        in_specs=[tok(D), tok(NA_W), tok(SW_QW), tok(AX_QW), _const_spec((1, D)),
                  _const_spec(wo.shape), vec, _const_spec((1, D)), vec, vec, vec,
                  _const_spec(wgu.shape), _const_spec(wd.shape), _const_spec((1, D))],
        out_specs=tok(D),
        compiler_params=_cparams("arbitrary", "arbitrary"),
        name="out_proj_ffn",
    )(x, ya, yb, yc, gg, wo, gta, gf, scf, shf, gtf, wgu, wd, gfin)


def _rope_tables(S):
    t = jnp.arange(S)
    row = (t // GRID_W).astype(jnp.float32)
    col = (t % GRID_W).astype(jnp.float32)
    axis_dim = HEAD_DIM // 2
    freqs = ROPE_THETA ** (-jnp.arange(0, axis_dim, 2, dtype=jnp.float32) / axis_dim)
    ang = jnp.stack([row[:, None] * freqs, col[:, None] * freqs], axis=1)
    cos = jnp.cos(ang)
    sin = jnp.sin(ang)
    cos_h = jnp.concatenate([cos, cos], axis=-1).reshape(S, HEAD_DIM)
    sin_h = jnp.concatenate([-sin, sin], axis=-1).reshape(S, HEAD_DIM)
    return jnp.tile(cos_h, (1, LANES // HEAD_DIM)), jnp.tile(sin_h, (1, LANES // HEAD_DIM))


def _block_diag_ones(width):
    idx = np.arange(width) // HEAD_DIM
    return jnp.asarray(idx[:, None] == idx[None, :], dtype=jnp.bfloat16)


_SW_PERM = np.concatenate([np.arange(HEAD_DIM) + (g * (SW_HEADS // SW_KV_HEADS) + r) * HEAD_DIM
                           for r in range(SW_HEADS // SW_KV_HEADS) for g in range(SW_KV_HEADS)])


def kernel(x, c, w_mod, b_mod, g_attn, w_in, rpb_na, sink_sw, t5_table, gq_ax, gk_ax, g_group,
           w_o, g_ffn, w_gu, w_down, g_final):
    B, S, D = x.shape
    L = w_mod.shape[0]
    bf = jnp.bfloat16
    rows = S // GRID_W

    c_pad = jnp.pad(c, ((0, 8 - B), (0, 0)))
    mod = _modulation(c_pad, w_mod, b_mod)[:, :B]
    mod = mod.reshape(L, B, 6, 1, D)

    cos, sin = _rope_tables(S)
    bdq = _block_diag_ones(AX_QW)
    bdk = _block_diag_ones(AX_KW)
    sw_bucket = _sw_bucket_tile()
    t5_flat = t5_table.astype(jnp.float32).reshape(-1)

    in_cols = np.arange(IN_WIDTH)
    in_cols[OFF_QB:OFF_QB + SW_QW] = OFF_QB + _SW_PERM
    group_perm = np.arange(D)
    group_perm[NA_W:NA_W + SW_QW] = NA_W + _SW_PERM

    for l in range(L):
        sh_a, sc_a, gt_a, sh_f, sc_f, gt_f = [mod[l, :, i] for i in range(6)]
        w_in_l = w_in[l][:, in_cols].astype(bf)
        qa, ka, va, qb, kb, vb, qct, kc, vct, qnorm, kmax = _in_proj(
            x, sc_a, sh_a, g_attn[l].reshape(1, D), w_in_l, cos, sin,
            jnp.tile(gq_ax[l], LANES // HEAD_DIM).reshape(1, LANES),
            jnp.tile(gk_ax[l], LANES // HEAD_DIM).reshape(1, LANES), bdq, bdk)
        ya = _na_attention(qa, ka, va, _na_rpb_rows(rpb_na[l]))
        yb = _sw_attention(qb, kb, vb, sw_bucket, t5_flat, sink_sw[l])
        yc = _ax_dispatch(qct, qnorm, kmax, kc, vct)
        x = _post(x, ya, yb, yc, g_group[l][group_perm].reshape(1, D),
                  w_o[l][group_perm, :].astype(bf), gt_a, g_ffn[l].reshape(1, D), sc_f, sh_f, gt_f,
                  w_gu[l].astype(bf), w_down[l].astype(bf), g_final.reshape(1, D),
                  final=(l == L - 1))
    return x
```

```python
import functools
import math

import jax
import jax.numpy as jnp
import numpy as np
from jax import lax
from jax.experimental import pallas as pl
from jax.experimental.pallas import tpu as pltpu

D_MODEL = 1024
HEAD_DIM = 64
GRID_W = 64
NA_HEADS = 4
SW_HEADS = 6
SW_KV_HEADS = 2
AX_HEADS = 6
AX_KV_HEADS = 2
NA_WIN_ROWS = 8
NA_WIN_COLS = 16
SW_RADIUS = 128
SW_BLOCK = 128
T5_BUCKETS = 32
T5_MAX_DIST = 128
ROPE_THETA = 10000.0
FFN_HIDDEN = 2816
EPS = 1e-6
NEG_INF = -1e30

NA_W = NA_HEADS * HEAD_DIM
SW_QW = SW_HEADS * HEAD_DIM
SW_KW = SW_KV_HEADS * HEAD_DIM
AX_QW = AX_HEADS * HEAD_DIM
AX_KW = AX_KV_HEADS * HEAD_DIM
IN_WIDTH = 3 * NA_W + SW_QW + 2 * SW_KW + AX_QW + 2 * AX_KW
OFF_QA, OFF_KA, OFF_VA = 0, NA_W, 2 * NA_W
OFF_QB = 3 * NA_W
OFF_KB = OFF_QB + SW_QW
OFF_VB = OFF_KB + SW_KW
OFF_QC = OFF_VB + SW_KW
OFF_KC = OFF_QC + AX_QW
OFF_VC = OFF_KC + AX_KW

LANES = 128
VMEM_LIMIT = 56 * 1024 * 1024

TOK_TILE = 512
IN_TILE = 1024
IN_SUB = 256
AX_TQ = 256
AX_TK = 512
AX_LOOKAHEAD = 2
AX_UNROLL = 8
NA_QROWS = 4
NA_BAND = NA_QROWS + NA_WIN_ROWS
NA_TQ = 1024
NA_LOOKAHEAD = 1
SW_TQ = 1024
SW_LOOKAHEAD = 2
FFN_CHUNK = 512

QK_SCALE = HEAD_DIM ** -0.5
LOG2E = math.log2(math.e)
Q_SCALE_EXP2 = QK_SCALE * LOG2E
AX_MAX_SHIFT = 60.0


def _cparams(*sem):
    return pltpu.CompilerParams(dimension_semantics=sem, vmem_limit_bytes=VMEM_LIMIT)


def _const_spec(shape):
    n = len(shape)
    return pl.BlockSpec(shape, lambda *_: (0,) * n, pipeline_mode=pl.Buffered(1))


def _layer_spec(stacked, layer):
    n = stacked.ndim - 1
    return pl.BlockSpec((None,) + stacked.shape[1:], lambda *_: (layer,) + (0,) * n,
                        pipeline_mode=pl.Buffered(1))


def _rms(x, g):
    return x * lax.rsqrt(jnp.mean(x * x, axis=-1, keepdims=True) + EPS) * g


def _mod_kernel(c_ref, w_ref, b_ref, o_ref):
    c = c_ref[...]
    cond = c * (1.0 / (1.0 + jnp.exp(-c)))
    o_ref[0] = jnp.dot(cond, w_ref[0], preferred_element_type=jnp.float32,
                       precision=lax.Precision.HIGHEST) + b_ref[0]


def _modulation(c_pad, w_mod, b_mod):
    L, D, N = w_mod.shape
    tn = 1536
    return pl.pallas_call(
        _mod_kernel,
        out_shape=jax.ShapeDtypeStruct((L, c_pad.shape[0], N), jnp.float32),
        grid=(L, N // tn),
        in_specs=[pl.BlockSpec(c_pad.shape, lambda l, j: (0, 0)),
                  pl.BlockSpec((1, D, tn), lambda l, j: (l, 0, j)),
                  pl.BlockSpec((1, 1, tn), lambda l, j: (l, 0, j))],
        out_specs=pl.BlockSpec((1, c_pad.shape[0], tn), lambda l, j: (l, 0, j)),
        compiler_params=_cparams("arbitrary", "arbitrary"),
        name="adaln_mod",
    )(c_pad, w_mod, b_mod.reshape(L, 1, N))


def _head_sumsq(t, bd):
    t2 = t * t
    hi = t2.astype(jnp.bfloat16)
    lo = (t2 - hi.astype(jnp.float32)).astype(jnp.bfloat16)
    return (jnp.dot(hi, bd, preferred_element_type=jnp.float32)
            + jnp.dot(lo, bd, preferred_element_type=jnp.float32))


def _rope_chunk(t, cos, sin_signed, first_half):
    swapped = jnp.where(first_half, pltpu.roll(t, LANES - 16, 1), pltpu.roll(t, 16, 1))
    return t * cos + swapped * sin_signed


def _in_kernel(x_ref, sc_ref, sh_ref, g_ref, w_ref, cos_ref, sin_ref, gq_ref, gk_ref,
               bdq_ref, bdk_ref,
               qa_ref, ka_ref, vat_ref, qb_ref, kb_ref, vbt_ref, qct_ref, kc_ref, vct_ref,
               qnorm_ref, kmax_ref):
    bf = jnp.bfloat16
    n_sub = x_ref.shape[1] // IN_SUB
    lane = lax.broadcasted_iota(jnp.int32, (IN_SUB, LANES), 1)
    first_half = (lane % 32) < 16
    ones_lane = jnp.where(lane == HEAD_DIM, 1.0, 0.0).astype(bf)

    def normed(j):
        x = x_ref[0, j * IN_SUB:(j + 1) * IN_SUB, :]
        return (_rms(x, g_ref[...]) * (1.0 + sc_ref[0]) + sh_ref[0]).astype(bf)

    def project(h):
        return jnp.dot(h, w_ref[...], preferred_element_type=jnp.float32)

    def head_stats(proj):
        return (_head_sumsq(proj[:, OFF_QC:OFF_QC + AX_QW], bdq_ref[...]),
                _head_sumsq(proj[:, OFF_KC:OFF_KC + AX_KW], bdk_ref[...]))

    def finish(j, proj, stats):
        rows = slice(j * IN_SUB, (j + 1) * IN_SUB)
        qa_ref[0, rows] = (proj[:, OFF_QA:OFF_QA + NA_W] * Q_SCALE_EXP2).astype(bf)
        ka_ref[0, rows] = proj[:, OFF_KA:OFF_KA + NA_W].astype(bf)
        qb_ref[0, rows] = (proj[:, OFF_QB:OFF_QB + SW_QW] * Q_SCALE_EXP2).astype(bf)
        kb_ref[0, rows] = proj[:, OFF_KB:OFF_KB + SW_KW].astype(bf)
        per_sub = IN_SUB // LANES
        for t in range(per_sub):
            blk = slice(t * LANES, (t + 1) * LANES)
            vat_ref[0, j * per_sub + t] = proj[blk, OFF_VA:OFF_VA + NA_W].T.astype(bf)
            vbt_ref[0, j * per_sub + t] = proj[blk, OFF_VB:OFF_VB + SW_KW].T.astype(bf)

        cos = cos_ref[rows, :]
        sin = sin_ref[rows, :]
        qss, kss = stats
        qn = proj[:, OFF_QC:OFF_QC + AX_QW] * lax.rsqrt(qss * (1.0 / HEAD_DIM) + EPS)
        chunks = []
        for c in range(AX_QW // LANES):
            t = qn[:, c * LANES:(c + 1) * LANES] * gq_ref[...]
            chunks.append(_rope_chunk(t, cos, sin, first_half) * Q_SCALE_EXP2)
        qt = jnp.concatenate(chunks, axis=1).T.astype(bf)
        qct_ref[0, :, rows] = qt
        qsq = qt.astype(jnp.float32)
        qsq = qsq * qsq
        norms = [jnp.sqrt(jnp.sum(qsq[hh * HEAD_DIM:(hh + 1) * HEAD_DIM], axis=0, keepdims=True))
                 for hh in range(AX_HEADS)]
        qnorm_ref[0, :, rows] = jnp.concatenate(
            norms + [jnp.zeros_like(norms[0])] * (8 - AX_HEADS), axis=0)

        kn = (proj[:, OFF_KC:OFF_KC + AX_KW] * lax.rsqrt(kss * (1.0 / HEAD_DIM) + EPS)
              * gk_ref[...])
        kr = _rope_chunk(kn, cos, sin, first_half)
        kb = kr.astype(bf)
        kc_ref[0, 0, rows] = jnp.where(lane < HEAD_DIM, kb, ones_lane)
        kc_ref[0, 1, rows] = jnp.where(lane < HEAD_DIM, pltpu.roll(kr, HEAD_DIM, 1).astype(bf),
                                       ones_lane)
        first_tk = (j * IN_SUB) // AX_TK
        off = (j * IN_SUB) % AX_TK
        vct_ref[0, first_tk, :, off:off + IN_SUB] = proj[:, OFF_VC:OFF_VC + AX_KW].T.astype(bf)

        kf = kb.astype(jnp.float32)
        kf2 = kf * kf
        lo = jnp.max(jnp.sum(jnp.where(lane < HEAD_DIM, kf2, 0.0), axis=1, keepdims=True),
                     axis=0, keepdims=True)
        hi = jnp.max(jnp.sum(jnp.where(lane < HEAD_DIM, 0.0, kf2), axis=1, keepdims=True),
                     axis=0, keepdims=True)
        return jnp.where(lane[:1] < HEAD_DIM, lo, hi)

    proj = project(normed(0))
    tile_max = None
    for j in range(n_sub):
        if j + 1 < n_sub:
            h_next = normed(j + 1)
        stats = head_stats(proj)
        if j + 1 < n_sub:
            proj_next = project(h_next)
        sub_max = finish(j, proj, stats)
        tile_max = sub_max if tile_max is None else jnp.maximum(tile_max, sub_max)
        if j + 1 < n_sub:
            proj = proj_next

    first = pl.program_id(1) == 0

    @pl.when(first)
    def _():
        kmax_ref[0] = tile_max

    @pl.when(jnp.logical_not(first))
    def _():
        kmax_ref[0] = jnp.maximum(kmax_ref[0], tile_max)


def _in_proj(x, sc, sh, g, w, layer, cos, sin, gq, gk, bdq, bdk):
    B, S, D = x.shape
    tm = IN_TILE
    bf = jnp.bfloat16
    tok = lambda width: pl.BlockSpec((1, tm, width), lambda b, i: (b, i, 0))
    vec = pl.BlockSpec((1, 1, D), lambda b, i: (b, 0, 0))
    out_shape = (
        jax.ShapeDtypeStruct((B, S, NA_W), bf), jax.ShapeDtypeStruct((B, S, NA_W), bf),
        jax.ShapeDtypeStruct((B, S // LANES, NA_W, LANES), bf),
        jax.ShapeDtypeStruct((B, S, SW_QW), bf), jax.ShapeDtypeStruct((B, S, SW_KW), bf),
        jax.ShapeDtypeStruct((B, S // SW_BLOCK, SW_KW, SW_BLOCK), bf),
        jax.ShapeDtypeStruct((B, AX_QW, S), bf),
        jax.ShapeDtypeStruct((B, AX_KV_HEADS, S, LANES), bf),
        jax.ShapeDtypeStruct((B, S // AX_TK, AX_KW, AX_TK), bf),
        jax.ShapeDtypeStruct((B, 8, S), jnp.float32),
        jax.ShapeDtypeStruct((B, 1, AX_KW), jnp.float32),
    )
    out_specs = (
        tok(NA_W), tok(NA_W),
        pl.BlockSpec((1, tm // LANES, NA_W, LANES), lambda b, i: (b, i, 0, 0)),
        tok(SW_QW), tok(SW_KW),
        pl.BlockSpec((1, tm // SW_BLOCK, SW_KW, SW_BLOCK), lambda b, i: (b, i, 0, 0)),
        pl.BlockSpec((1, AX_QW, tm), lambda b, i: (b, 0, i)),
        pl.BlockSpec((1, AX_KV_HEADS, tm, LANES), lambda b, i: (b, 0, i, 0)),
        pl.BlockSpec((1, tm // AX_TK, AX_KW, AX_TK), lambda b, i: (b, i, 0, 0)),
        pl.BlockSpec((1, 8, tm), lambda b, i: (b, 0, i)),
        pl.BlockSpec((1, 1, AX_KW), lambda b, i: (b, 0, 0)),
    )
    return pl.pallas_call(
        _in_kernel,
        out_shape=out_shape,
        grid=(B, S // tm),
        in_specs=[tok(D), vec, vec, _const_spec((1, D)), _layer_spec(w, layer),
                  pl.BlockSpec((tm, LANES), lambda b, i: (i, 0)),
                  pl.BlockSpec((tm, LANES), lambda b, i: (i, 0)),
                  _const_spec((1, LANES)), _const_spec((1, LANES)),
                  _const_spec(bdq.shape), _const_spec(bdk.shape)],
        out_specs=out_specs,
        compiler_params=_cparams("arbitrary", "arbitrary"),
        name="in_proj",
    )(x, sc, sh, g, w, cos, sin, gq, gk, bdq, bdk)


def _na_build_bias(rp_ref, bias_ref, rows):
    shape = (GRID_W, LANES)
    kc = lax.broadcasted_iota(jnp.int32, shape, 0)
    lane = lax.broadcasted_iota(jnp.int32, shape, 1)
    c = lane % GRID_W
    cs = jnp.clip(c - NA_WIN_COLS // 2, 0, GRID_W - NA_WIN_COLS)
    col_ok = (kc >= cs) & (kc < cs + NA_WIN_COLS)
    left = lane < GRID_W
    neg = jnp.full(shape, NEG_INF, jnp.float32)
    tq = NA_QROWS * GRID_W
    for h in range(NA_HEADS):
        pair = []
        for a in range(2 * NA_WIN_ROWS):
            x = jnp.broadcast_to(rp_ref[h, a:a + 1, :] * LOG2E, shape)
            t = pltpu.roll(x, LANES - (NA_WIN_COLS - 1), 1, stride=1, stride_axis=0)
            pair.append(jnp.where(col_ok, t, neg))
        for variant, qr0 in enumerate((0, 2 * NA_QROWS, rows - NA_QROWS)):
            bs = int(np.clip(qr0 - NA_WIN_ROWS // 2, 0, rows - NA_BAND))
            for kj in range(NA_BAND):
                kr = bs + kj
                for u in range(NA_QROWS // 2):
                    r0 = qr0 + 2 * u
                    ok = [int(np.clip(r - NA_WIN_ROWS // 2, 0, rows - NA_WIN_ROWS)) <= kr
                          < int(np.clip(r - NA_WIN_ROWS // 2, 0, rows - NA_WIN_ROWS)) + NA_WIN_ROWS
                          for r in (r0, r0 + 1)]
                    a0 = kr - r0 + NA_WIN_ROWS - 1
                    if ok[0] and ok[1]:
                        tile = pair[a0]
                    elif ok[0]:
                        tile = jnp.where(left, pair[a0], neg)
                    elif ok[1]:
                        tile = jnp.where(left, neg, pair[a0])
                    else:
                        tile = neg
                    bias_ref[variant, kj * GRID_W:(kj + 1) * GRID_W,
                             h * tq + u * LANES:h * tq + (u + 1) * LANES] = tile


def _na_kernel(q_ref, k_ref, vt_ref, rp_ref, o_ref, bias_ref, *, rows):
    i = pl.program_id(1)
    tq = NA_QROWS * GRID_W
    nblk = NA_TQ // tq
    nblk_total = rows // NA_QROWS
    bf = jnp.bfloat16

    @pl.when((pl.program_id(0) == 0) & (i == 0))
    def _():
        _na_build_bias(rp_ref, bias_ref, rows)

    head_of_lane = lax.broadcasted_iota(jnp.int32, (tq, NA_W), 1) // HEAD_DIM
    ones = jnp.ones((16, NA_BAND * GRID_W), bf)

    def band_row(blk):
        return jnp.clip((i * nblk + blk) * NA_QROWS - NA_WIN_ROWS // 2, 0, rows - NA_BAND)

    def scores(blk):
        q = q_ref[0, blk * tq:(blk + 1) * tq, :]
        stack = jnp.concatenate([jnp.where(head_of_lane == h, q, jnp.zeros_like(q))
                                 for h in range(NA_HEADS)], axis=0)
        start = pl.multiple_of(band_row(blk) * GRID_W, NA_QROWS * GRID_W)
        kb = k_ref[0, pl.ds(start, NA_BAND * GRID_W), :]
        return lax.dot_general(kb, stack, (((1,), (1,)), ((), ())),
                               preferred_element_type=jnp.float32)

    pending = [scores(blk) for blk in range(min(NA_LOOKAHEAD, nblk))]
    for blk in range(nblk):
        s = pending.pop(0)
        if blk + NA_LOOKAHEAD < nblk:
            pending.append(scores(blk + NA_LOOKAHEAD))
        n = i * nblk + blk
        variant = jnp.where(n == 0, 0, jnp.where(n == nblk_total - 1, 2, 1))
        s = s + bias_ref[variant]
        m = jnp.max(s, axis=0, keepdims=True)
        p = jnp.exp2(s - m).astype(bf)
        first = band_row(blk) * GRID_W // LANES
        vt = jnp.concatenate([vt_ref[0, first + t] for t in range(NA_BAND * GRID_W // LANES)],
                             axis=1)
        outs = []
        for h in range(NA_HEADS):
            vt_ext = jnp.concatenate([vt[h * HEAD_DIM:(h + 1) * HEAD_DIM], ones], axis=0)
            o = jnp.dot(vt_ext, p[:, h * tq:(h + 1) * tq], preferred_element_type=jnp.float32)
            outs.append(o[:HEAD_DIM] / o[HEAD_DIM:HEAD_DIM + 1])
        o_ref[0, blk * tq:(blk + 1) * tq, :] = jnp.concatenate(outs, axis=0).T


def _na_attention(q, k, vt, rp):
    B, S, W = q.shape
    rows = S // GRID_W
    tq = NA_QROWS * GRID_W
    return pl.pallas_call(
        functools.partial(_na_kernel, rows=rows),
        out_shape=jax.ShapeDtypeStruct((B, S, W), jnp.float32),
        grid=(B, S // NA_TQ),
        in_specs=[pl.BlockSpec((1, NA_TQ, W), lambda b, j: (b, j, 0)),
                  pl.BlockSpec((1, S, W), lambda b, j: (b, 0, 0), pipeline_mode=pl.Buffered(1)),
                  pl.BlockSpec((1, S // LANES, W, LANES), lambda b, j: (b, 0, 0, 0),
                               pipeline_mode=pl.Buffered(1)),
                  _const_spec(rp.shape)],
        out_specs=pl.BlockSpec((1, NA_TQ, W), lambda b, j: (b, j, 0)),
        scratch_shapes=[pltpu.VMEM((3, NA_BAND * GRID_W, NA_HEADS * tq), jnp.float32)],
        compiler_params=_cparams("arbitrary", "arbitrary"),
        name="na_attn",
    )(q, k, vt, rp)


def _na_rpb_rows(rpb):
    n_col = rpb.shape[-1]
    p = jnp.pad(rpb.astype(jnp.float32)[:, :, ::-1], ((0, 0), (1, 1), (0, GRID_W - n_col)))
    return jnp.concatenate([p[:, 1:], p[:, :-1]], axis=-1)


def _sw_kernel(sink_ref, t5_ref, q_ref, k_ref, vt_ref, bucket_ref, o_ref, bias_ref, *, seq):
    i = pl.program_id(1)
    nblk_total = seq // SW_BLOCK

    @pl.when((pl.program_id(0) == 0) & (i == 0))
    def _():
        bucket = bucket_ref[...]
        for h in range(SW_HEADS):
            acc = jnp.full(bucket.shape, NEG_INF, jnp.float32)
            for b in range(T5_BUCKETS):
                acc = jnp.where(bucket == b, t5_ref[b * SW_HEADS + h] * LOG2E, acc)
            bias_ref[:, h * SW_BLOCK:(h + 1) * SW_BLOCK] = acc

    nblk = SW_TQ // SW_BLOCK
    rep = SW_HEADS // SW_KV_HEADS
    bf = jnp.bfloat16
    width = SW_HEADS * SW_BLOCK
    lane_group = lax.broadcasted_iota(jnp.int32, (SW_BLOCK, LANES), 1) // HEAD_DIM
    row_group = lax.broadcasted_iota(jnp.int32, (SW_BLOCK, LANES), 0) // HEAD_DIM
    key_row = lax.broadcasted_iota(jnp.int32, (3 * SW_BLOCK, width), 0)
    sink = jnp.concatenate([jnp.full((1, SW_BLOCK), sink_ref[h] * LOG2E, jnp.float32)
                            for h in range(SW_HEADS)], axis=1)
    ones = jnp.ones((16, 3 * SW_BLOCK), bf)

    def neighbours(blk):
        n = i * nblk + blk
        return jnp.maximum(n - 1, 0), n, jnp.minimum(n + 1, nblk_total - 1)

    def scores(blk):
        rows = slice(blk * SW_BLOCK, (blk + 1) * SW_BLOCK)
        stack = []
        for g in range(SW_KV_HEADS):
            for r in range(rep):
                qcol = q_ref[0, rows, r * LANES:(r + 1) * LANES]
                stack.append(jnp.where(lane_group == g, qcol, jnp.zeros_like(qcol)))
        kw = jnp.concatenate([k_ref[0, pl.ds(pl.multiple_of(nb * SW_BLOCK, SW_BLOCK), SW_BLOCK), :]
                              for nb in neighbours(blk)], axis=0)
        return lax.dot_general(kw, jnp.concatenate(stack, axis=0), (((1,), (1,)), ((), ())),
                               preferred_element_type=jnp.float32)

    pending = [scores(blk) for blk in range(min(SW_LOOKAHEAD, nblk))]
    for blk in range(nblk):
        s = pending.pop(0)
        if blk + SW_LOOKAHEAD < nblk:
            pending.append(scores(blk + SW_LOOKAHEAD))
        s = s + bias_ref[...]
        left, n, right = neighbours(blk)
        if blk in (0, nblk - 1):
            outside = (((n == 0) & (key_row < SW_BLOCK))
                       | ((n == nblk_total - 1) & (key_row >= 2 * SW_BLOCK)))
            s = jnp.where(outside, NEG_INF, s)
        m = jnp.maximum(jnp.max(s, axis=0, keepdims=True), sink)
        p = jnp.exp2(s - m).astype(bf)
        vt_ext = jnp.concatenate(
            [jnp.concatenate([vt_ref[0, left], vt_ref[0, n], vt_ref[0, right]], axis=1), ones], axis=0)
        o = jnp.dot(vt_ext, p, preferred_element_type=jnp.float32)
        res = o[:LANES] / (o[LANES:LANES + 1] + jnp.exp2(sink - m))
        for r in range(rep):
            yt = jnp.where(row_group == 0, res[:, r * SW_BLOCK:(r + 1) * SW_BLOCK],
                           res[:, (rep + r) * SW_BLOCK:(rep + r + 1) * SW_BLOCK])
            o_ref[0, blk * SW_BLOCK:(blk + 1) * SW_BLOCK, r * LANES:(r + 1) * LANES] = yt.T


def _sw_attention(q, k, vt, bucket, t5_flat, sink):
    B, S, _ = q.shape
    nb = S // SW_BLOCK
    return pl.pallas_call(
        functools.partial(_sw_kernel, seq=S),
        out_shape=jax.ShapeDtypeStruct((B, S, SW_QW), jnp.float32),
        grid_spec=pltpu.PrefetchScalarGridSpec(
            num_scalar_prefetch=2,
            grid=(B, S // SW_TQ),
            in_specs=[pl.BlockSpec((1, SW_TQ, SW_QW), lambda b, i, *_: (b, i, 0)),
                      pl.BlockSpec((1, S, SW_KW), lambda b, i, *_: (b, 0, 0),
                                   pipeline_mode=pl.Buffered(1)),
                      pl.BlockSpec((1, nb, SW_KW, SW_BLOCK), lambda b, i, *_: (b, 0, 0, 0),
                                   pipeline_mode=pl.Buffered(1)),
                      pl.BlockSpec(bucket.shape, lambda b, i, *_: (0, 0),
                                   pipeline_mode=pl.Buffered(1))],
            out_specs=pl.BlockSpec((1, SW_TQ, SW_QW), lambda b, i, *_: (b, i, 0)),
            scratch_shapes=[pltpu.VMEM((3 * SW_BLOCK, SW_HEADS * SW_BLOCK), jnp.float32)],
        ),
        compiler_params=_cparams("arbitrary", "arbitrary"),
        name="sw_attn",
    )(sink, t5_flat, q, k, vt, bucket)


def _t5_bucket(rel):
    nb = T5_BUCKETS // 2
    ret = (rel > 0).astype(jnp.int32) * nb
    n = jnp.abs(rel)
    max_exact = nb // 2
    nf = jnp.maximum(n, max_exact).astype(jnp.float32)
    large = max_exact + (jnp.log(nf / max_exact) / math.log(T5_MAX_DIST / max_exact)
                         * (nb - max_exact)).astype(jnp.int32)
    large = jnp.minimum(large, nb - 1)
    return ret + jnp.where(n < max_exact, n, large)


def _sw_bucket_tile():
    qpos = jnp.arange(SW_BLOCK)
    kpos = jnp.arange(3 * SW_BLOCK) - SW_BLOCK
    rel = kpos[:, None] - qpos[None, :]
    return jnp.where(jnp.abs(rel) <= SW_RADIUS, _t5_bucket(rel), -1).astype(jnp.int32)


def _ax_kernel(qt_ref, qnorm_ref, kmax_ref, k_ref, vt_ref, o_ref, qpad_ref, m_ref, l_ref, acc_ref,
               s0_ref, *, nkt, bounded):
    tq = qt_ref.shape[2]
    tk = AX_TK
    rep = AX_HEADS // AX_KV_HEADS
    bf = jnp.bfloat16
    row = lax.broadcasted_iota(jnp.int32, (16, tq), 0)
    for h in range(AX_HEADS):
        g = h // rep
        qth = qt_ref[0, h * HEAD_DIM:(h + 1) * HEAD_DIM, :]
        if bounded:
            shift = qnorm_ref[0, h:h + 1, :] * jnp.sqrt(kmax_ref[0, :, g * HEAD_DIM:g * HEAD_DIM + 1])
            extra = jnp.where(row == 0, -shift, 0.0).astype(bf)
        else:
            extra = jnp.zeros((16, tq), bf)
        qpad_ref[h] = jnp.concatenate(
            [qth, extra, jnp.zeros((LANES - HEAD_DIM - 16, tq), bf)], axis=0)
    if not bounded:
        m_ref[...] = jnp.full(m_ref.shape, -jnp.inf, jnp.float32)
    l_ref[...] = jnp.zeros(l_ref.shape, jnp.float32)
    acc_ref[...] = jnp.zeros(acc_ref.shape, jnp.float32)

    def score(kt, h):
        ks = pl.multiple_of(kt * tk, tk)
        return jnp.dot(k_ref[0, h // rep, pl.ds(ks, tk), :], qpad_ref[h],
                       preferred_element_type=jnp.float32)

    for h in range(AX_LOOKAHEAD):
        s0_ref[h] = score(0, h)

    def body(kt, carry):
        pending = [s0_ref[h] for h in range(AX_LOOKAHEAD)]
        for h in range(AX_HEADS):
            g = h // rep
            s = pending.pop(0)
            ahead = h + AX_LOOKAHEAD
            if ahead < AX_HEADS:
                pending.append(score(kt, ahead))
            else:
                s0_ref[ahead - AX_HEADS] = score(jnp.minimum(kt + 1, nkt - 1), ahead - AX_HEADS)
            vt = vt_ref[0, kt, g * HEAD_DIM:(g + 1) * HEAD_DIM, :]
            if bounded:
                p = jnp.exp2(s)
                l_ref[h] += jnp.sum(p.reshape(tk // 8, 8, tq), axis=0)
                acc_ref[h] += jnp.dot(vt, p.astype(bf), preferred_element_type=jnp.float32)
            else:
                m_old = m_ref[h]
                m_new = jnp.maximum(m_old, jnp.max(s, axis=0, keepdims=True))
                alpha = jnp.exp2(m_old - m_new)
                p = jnp.exp2(s - m_new)
                l_ref[h] = alpha * l_ref[h] + jnp.sum(p.reshape(tk // 8, 8, tq), axis=0)
                m_ref[h] = m_new
                pv = jnp.dot(vt, p.astype(bf), preferred_element_type=jnp.float32)
                acc_ref[h] = alpha * acc_ref[h] + pv
        return carry

    lax.fori_loop(0, nkt, body, 0, unroll=AX_UNROLL)
    out_t = jnp.concatenate([acc_ref[h] / jnp.sum(l_ref[h], axis=0, keepdims=True)
                             for h in range(AX_HEADS)], axis=0)
    o_ref[0] = out_t.T


def _ax_attention(qt, qnorm, kmax, k, vt, bounded):
    B, W, S = qt.shape
    nkt = vt.shape[1]
    return pl.pallas_call(
        functools.partial(_ax_kernel, nkt=nkt, bounded=bounded),
        out_shape=jax.ShapeDtypeStruct((B, S, W), jnp.float32),
        grid=(B, S // AX_TQ),
        in_specs=[pl.BlockSpec((1, W, AX_TQ), lambda b, i: (b, 0, i)),
                  pl.BlockSpec((1, 8, AX_TQ), lambda b, i: (b, 0, i)),
                  pl.BlockSpec((1, 1, AX_KW), lambda b, i: (b, 0, 0)),
                  pl.BlockSpec((1, AX_KV_HEADS, S, LANES), lambda b, i: (b, 0, 0, 0),
                               pipeline_mode=pl.Buffered(1)),
                  pl.BlockSpec((1, nkt, AX_KW, AX_TK), lambda b, i: (b, 0, 0, 0),
                               pipeline_mode=pl.Buffered(1))],
        out_specs=pl.BlockSpec((1, AX_TQ, W), lambda b, i: (b, i, 0)),
        scratch_shapes=[pltpu.VMEM((AX_HEADS, LANES, AX_TQ), jnp.bfloat16),
                        pltpu.VMEM((AX_HEADS, 1, AX_TQ), jnp.float32),
                        pltpu.VMEM((AX_HEADS, 8, AX_TQ), jnp.float32),
                        pltpu.VMEM((AX_HEADS, HEAD_DIM, AX_TQ), jnp.float32),
                        pltpu.VMEM((AX_LOOKAHEAD, AX_TK, AX_TQ), jnp.float32)],
        compiler_params=_cparams("arbitrary", "arbitrary"),
        name="ax_attn_bounded" if bounded else "ax_attn_online",
    )(qt, qnorm, kmax, k, vt)


def _ax_dispatch(qt, qnorm, kmax, k, vt):
    rep = AX_HEADS // AX_KV_HEADS
    knorm = jnp.sqrt(kmax[:, 0, ::HEAD_DIM])
    shift_max = jnp.max(qnorm[:, :AX_HEADS], axis=-1) * jnp.repeat(knorm, rep, axis=1)
    safe = jnp.max(shift_max) <= AX_MAX_SHIFT
    return lax.cond(safe,
                    functools.partial(_ax_attention, bounded=True),
                    functools.partial(_ax_attention, bounded=False),
                    qt, qnorm, kmax, k, vt)


def _post_kernel(x_ref, ya_ref, yb_ref, yc_ref, gg_ref, wo_ref, gta_ref, gf_ref, scf_ref, shf_ref,
                 gtf_ref, wgu_ref, wd_ref, gfin_ref, o_ref, *, final):
    bf = jnp.bfloat16
    gg = gg_ref[...]
    y = jnp.concatenate([
        _rms(ya_ref[0], gg[:, 0:NA_W]).astype(bf),
        _rms(yb_ref[0], gg[:, NA_W:NA_W + SW_QW]).astype(bf),
        _rms(yc_ref[0], gg[:, NA_W + SW_QW:]).astype(bf)], axis=1)
    x1 = x_ref[0] + gta_ref[0] * jnp.dot(y, wo_ref[...], preferred_element_type=jnp.float32)

    h = (_rms(x1, gf_ref[...]) * (1.0 + scf_ref[0]) + shf_ref[0]).astype(bf)
    acc = jnp.zeros(x1.shape, jnp.float32)
    for c0 in range(0, FFN_HIDDEN, FFN_CHUNK):
        c1 = min(c0 + FFN_CHUNK, FFN_HIDDEN)
        gate = jnp.dot(h, wgu_ref[:, c0:c1], preferred_element_type=jnp.float32)
        up = jnp.dot(h, wgu_ref[:, FFN_HIDDEN + c0:FFN_HIDDEN + c1],
                     preferred_element_type=jnp.float32)
        act = (gate * (1.0 / (1.0 + jnp.exp(-gate))) * up).astype(bf)
        acc = acc + jnp.dot(act, wd_ref[c0:c1, :], preferred_element_type=jnp.float32)
    x2 = x1 + gtf_ref[0] * acc
    if final:
        x2 = _rms(x2, gfin_ref[...])
    o_ref[0] = x2


def _post(x, ya, yb, yc, gg, wo, gta, gf, scf, shf, gtf, wgu, wd, gfin, layer, final):
    B, S, D = x.shape
    tm = TOK_TILE
    tok = lambda width: pl.BlockSpec((1, tm, width), lambda b, i: (b, i, 0))
    vec = pl.BlockSpec((1, 1, D), lambda b, i: (b, 0, 0))
    return pl.pallas_call(
        functools.partial(_post_kernel, final=final),
        out_shape=jax.ShapeDtypeStruct((B, S, D), jnp.float32),
        grid=(B, S // tm),
        in_specs=[tok(D), tok(NA_W), tok(SW_QW), tok(AX_QW), _const_spec((1, D)),
                  _layer_spec(wo, layer), vec, _const_spec((1, D)), vec, vec, vec,
                  _layer_spec(wgu, layer), _layer_spec(wd, layer), _const_spec((1, D))],
        out_specs=tok(D),
        compiler_params=_cparams("arbitrary", "arbitrary"),
        name="out_proj_ffn",
    )(x, ya, yb, yc, gg, wo, gta, gf, scf, shf, gtf, wgu, wd, gfin)


def _rope_tables(S):
    t = jnp.arange(S)
    row = (t // GRID_W).astype(jnp.float32)
    col = (t % GRID_W).astype(jnp.float32)
    axis_dim = HEAD_DIM // 2
    freqs = ROPE_THETA ** (-jnp.arange(0, axis_dim, 2, dtype=jnp.float32) / axis_dim)
    ang = jnp.stack([row[:, None] * freqs, col[:, None] * freqs], axis=1)
    cos = jnp.cos(ang)
    sin = jnp.sin(ang)
    cos_h = jnp.concatenate([cos, cos], axis=-1).reshape(S, HEAD_DIM)
    sin_h = jnp.concatenate([-sin, sin], axis=-1).reshape(S, HEAD_DIM)
    return jnp.tile(cos_h, (1, LANES // HEAD_DIM)), jnp.tile(sin_h, (1, LANES // HEAD_DIM))


def _block_diag_ones(width):
    idx = np.arange(width) // HEAD_DIM
    return jnp.asarray(idx[:, None] == idx[None, :], dtype=jnp.bfloat16)


def _sw_relabel(t, start, axis):
    rep = SW_HEADS // SW_KV_HEADS
    seg = lax.slice_in_dim(t, start, start + SW_QW, axis=axis)
    shp = seg.shape
    seg = seg.reshape(shp[:axis] + (SW_KV_HEADS, rep, HEAD_DIM) + shp[axis + 1:])
    seg = jnp.swapaxes(seg, axis, axis + 1).reshape(shp)
    return jnp.concatenate([lax.slice_in_dim(t, 0, start, axis=axis), seg,
                            lax.slice_in_dim(t, start + SW_QW, t.shape[axis], axis=axis)], axis=axis)


def kernel(x, c, w_mod, b_mod, g_attn, w_in, rpb_na, sink_sw, t5_table, gq_ax, gk_ax, g_group,
           w_o, g_ffn, w_gu, w_down, g_final):
    B, S, D = x.shape
    L = w_mod.shape[0]
    bf = jnp.bfloat16

    c_pad = jnp.pad(c, ((0, 8 - B), (0, 0)))
    mod = _modulation(c_pad, w_mod, b_mod)[:, :B]
    mod = mod.reshape(L, B, 6, 1, D)

    cos, sin = _rope_tables(S)
    bdq = _block_diag_ones(AX_QW)
    bdk = _block_diag_ones(AX_KW)
    sw_bucket = _sw_bucket_tile()
    t5_flat = t5_table.astype(jnp.float32).reshape(-1)

    w_in_p = _sw_relabel(w_in, OFF_QB, axis=2).astype(bf)
    w_o_p = _sw_relabel(w_o, NA_W, axis=1).astype(bf)
    g_group_p = _sw_relabel(g_group, NA_W, axis=1)
    w_gu_b = w_gu.astype(bf)
    w_down_b = w_down.astype(bf)

    for l in range(L):
        sh_a, sc_a, gt_a, sh_f, sc_f, gt_f = [mod[l, :, i] for i in range(6)]
        qa, ka, va, qb, kb, vb, qct, kc, vct, qnorm, kmax = _in_proj(
            x, sc_a, sh_a, g_attn[l].reshape(1, D), w_in_p, l, cos, sin,
            jnp.tile(gq_ax[l], LANES // HEAD_DIM).reshape(1, LANES),
            jnp.tile(gk_ax[l], LANES // HEAD_DIM).reshape(1, LANES), bdq, bdk)
        ya = _na_attention(qa, ka, va, _na_rpb_rows(rpb_na[l]))
        yb = _sw_attention(qb, kb, vb, sw_bucket, t5_flat, sink_sw[l])
        yc = _ax_dispatch(qct, qnorm, kmax, kc, vct)
        x = _post(x, ya, yb, yc, g_group_p[l].reshape(1, D), w_o_p, gt_a, g_ffn[l].reshape(1, D),
                  sc_f, sh_f, gt_f, w_gu_b, w_down_b, g_final.reshape(1, D), layer=l,
                  final=(l == L - 1))
    return x
```

```python
import functools
import math

import jax
import jax.numpy as jnp
import numpy as np
from jax import lax
from jax.experimental import pallas as pl
from jax.experimental.pallas import tpu as pltpu

D_MODEL = 1024
HEAD_DIM = 64
GRID_W = 64
NA_HEADS = 4
SW_HEADS = 6
SW_KV_HEADS = 2
AX_HEADS = 6
AX_KV_HEADS = 2
NA_WIN_ROWS = 8
NA_WIN_COLS = 16
SW_RADIUS = 128
SW_BLOCK = 128
T5_BUCKETS = 32
T5_MAX_DIST = 128
ROPE_THETA = 10000.0
FFN_HIDDEN = 2816
EPS = 1e-6
NEG_INF = -1e30

NA_W = NA_HEADS * HEAD_DIM
SW_QW = SW_HEADS * HEAD_DIM
SW_KW = SW_KV_HEADS * HEAD_DIM
AX_QW = AX_HEADS * HEAD_DIM
AX_KW = AX_KV_HEADS * HEAD_DIM
IN_WIDTH = 3 * NA_W + SW_QW + 2 * SW_KW + AX_QW + 2 * AX_KW
OFF_QA, OFF_KA, OFF_VA = 0, NA_W, 2 * NA_W
OFF_QB = 3 * NA_W
OFF_KB = OFF_QB + SW_QW
OFF_VB = OFF_KB + SW_KW
OFF_QC = OFF_VB + SW_KW
OFF_KC = OFF_QC + AX_QW
OFF_VC = OFF_KC + AX_KW

LANES = 128
VMEM_LIMIT = 56 * 1024 * 1024

TOK_TILE = 512
IN_TILE = 1024
IN_SUB = 256
AX_TQ = 256
AX_TK = 512
AX_LOOKAHEAD = 2
AX_UNROLL = 8
NA_QROWS = 4
NA_BAND = NA_QROWS + NA_WIN_ROWS
NA_TQ = 1024
NA_LOOKAHEAD = 1
SW_TQ = 1024
SW_LOOKAHEAD = 2
FFN_CHUNK = 512

QK_SCALE = HEAD_DIM ** -0.5
F8 = jnp.float8_e4m3fn
F8_MAX = 448.0
FP8_Q_HI, FP8_K_HI = 2.0, 0.5
FP8_Q_LO, FP8_K_LO = 16.0, 1.0 / 16.0
LOG2E = math.log2(math.e)
Q_SCALE_EXP2 = QK_SCALE * LOG2E
AX_MAX_SHIFT = 60.0


def _cparams(*sem):
    return pltpu.CompilerParams(dimension_semantics=sem, vmem_limit_bytes=VMEM_LIMIT)


def _const_spec(shape):
    n = len(shape)
    return pl.BlockSpec(shape, lambda *_: (0,) * n, pipeline_mode=pl.Buffered(1))


def _layer_spec(stacked, layer):
    n = stacked.ndim - 1
    return pl.BlockSpec((None,) + stacked.shape[1:], lambda *_: (layer,) + (0,) * n,
                        pipeline_mode=pl.Buffered(1))


def _rms(x, g):
    return x * lax.rsqrt(jnp.mean(x * x, axis=-1, keepdims=True) + EPS) * g


def _mod_kernel(c_ref, w_ref, b_ref, o_ref):
    c = c_ref[...]
    cond = c * (1.0 / (1.0 + jnp.exp(-c)))
    o_ref[0] = jnp.dot(cond, w_ref[0], preferred_element_type=jnp.float32,
                       precision=lax.Precision.HIGHEST) + b_ref[0]


def _modulation(c_pad, w_mod, b_mod):
    L, D, N = w_mod.shape
    tn = 1536
    return pl.pallas_call(
        _mod_kernel,
        out_shape=jax.ShapeDtypeStruct((L, c_pad.shape[0], N), jnp.float32),
        grid=(L, N // tn),
        in_specs=[pl.BlockSpec(c_pad.shape, lambda l, j: (0, 0)),
                  pl.BlockSpec((1, D, tn), lambda l, j: (l, 0, j)),
                  pl.BlockSpec((1, 1, tn), lambda l, j: (l, 0, j))],
        out_specs=pl.BlockSpec((1, c_pad.shape[0], tn), lambda l, j: (l, 0, j)),
        compiler_params=_cparams("arbitrary", "arbitrary"),
        name="adaln_mod",
    )(c_pad, w_mod, b_mod.reshape(L, 1, N))


def _head_sumsq(t, bd):
    t2 = t * t
    hi = t2.astype(jnp.bfloat16)
    lo = (t2 - hi.astype(jnp.float32)).astype(jnp.bfloat16)
    return (jnp.dot(hi, bd, preferred_element_type=jnp.float32)
            + jnp.dot(lo, bd, preferred_element_type=jnp.float32))


def _rope_chunk(t, cos, sin_signed, first_half):
    swapped = jnp.where(first_half, pltpu.roll(t, LANES - 16, 1), pltpu.roll(t, 16, 1))
    return t * cos + swapped * sin_signed


def _in_kernel(x_ref, sc_ref, sh_ref, g_ref, w_ref, cos_ref, sin_ref, gq_ref, gk_ref,
               bdq_ref, bdk_ref,
               qa_ref, ka_ref, vat_ref, qb_ref, kb_ref, vbt_ref, qct_ref, kc_ref, vct_ref,
               qnorm_ref, kmax_ref, q8_ref, k8_ref):
    bf = jnp.bfloat16
    n_sub = x_ref.shape[1] // IN_SUB
    lane = lax.broadcasted_iota(jnp.int32, (IN_SUB, LANES), 1)
    first_half = (lane % 32) < 16

    def normed(j):
        x = x_ref[0, j * IN_SUB:(j + 1) * IN_SUB, :]
        return (_rms(x, g_ref[...]) * (1.0 + sc_ref[0]) + sh_ref[0]).astype(bf)

    def project(h):
        return jnp.dot(h, w_ref[...], preferred_element_type=jnp.float32)

    def head_stats(proj):
        return (_head_sumsq(proj[:, OFF_QC:OFF_QC + AX_QW], bdq_ref[...]),
                _head_sumsq(proj[:, OFF_KC:OFF_KC + AX_KW], bdk_ref[...]))

    def finish(j, proj, stats):
        rows = slice(j * IN_SUB, (j + 1) * IN_SUB)
        qa_ref[0, rows] = (proj[:, OFF_QA:OFF_QA + NA_W] * Q_SCALE_EXP2).astype(bf)
        ka_ref[0, rows] = proj[:, OFF_KA:OFF_KA + NA_W].astype(bf)
        qb_ref[0, rows] = (proj[:, OFF_QB:OFF_QB + SW_QW] * Q_SCALE_EXP2).astype(bf)
        kb_ref[0, rows] = proj[:, OFF_KB:OFF_KB + SW_KW].astype(bf)
        per_sub = IN_SUB // LANES
        for t in range(per_sub):
            blk = slice(t * LANES, (t + 1) * LANES)
            vat_ref[0, j * per_sub + t] = proj[blk, OFF_VA:OFF_VA + NA_W].T.astype(bf)
            vbt_ref[0, j * per_sub + t] = proj[blk, OFF_VB:OFF_VB + SW_KW].T.astype(bf)

        cos = cos_ref[rows, :]
        sin = sin_ref[rows, :]
        qss, kss = stats
        qn = proj[:, OFF_QC:OFF_QC + AX_QW] * lax.rsqrt(qss * (1.0 / HEAD_DIM) + EPS)
        chunks = []
        for c in range(AX_QW // LANES):
            t = qn[:, c * LANES:(c + 1) * LANES] * gq_ref[...]
            chunks.append(_rope_chunk(t, cos, sin, first_half) * Q_SCALE_EXP2)
        f32 = jnp.float32
        qt = jnp.concatenate(chunks, axis=1).T
        qct_ref[0, :, rows] = qt.astype(bf)
        a1 = (qt * FP8_Q_HI).astype(F8).astype(f32)
        a2 = ((qt - a1 * (1.0 / FP8_Q_HI)) * FP8_Q_LO).astype(F8).astype(f32)
        a3 = qt.astype(F8).astype(f32)
        q8_ref[0, 0, :, rows] = a1.astype(F8)
        q8_ref[0, 1, :, rows] = a2.astype(F8)
        q8_ref[0, 2, :, rows] = a3.astype(F8)
        qsq = a1 * a1 + a2 * a2 + a3 * a3
        norms = [jnp.sqrt(jnp.sum(qsq[hh * HEAD_DIM:(hh + 1) * HEAD_DIM], axis=0, keepdims=True))
                 for hh in range(AX_HEADS)]
        qnorm_ref[0, :, rows] = jnp.concatenate(
            norms + [jnp.zeros_like(norms[0])] * (8 - AX_HEADS), axis=0)

        kn = (proj[:, OFF_KC:OFF_KC + AX_KW] * lax.rsqrt(kss * (1.0 / HEAD_DIM) + EPS)
              * gk_ref[...])
        kr = _rope_chunk(kn, cos, sin, first_half)
        kb = kr.astype(bf)
        zero_bf = jnp.zeros_like(kb)
        kc_ref[0, 0, rows] = jnp.where(lane < HEAD_DIM, kb, zero_bf)
        kc_ref[0, 1, rows] = jnp.where(lane < HEAD_DIM, pltpu.roll(kr, HEAD_DIM, 1).astype(bf), zero_bf)
        x1 = kr * FP8_K_HI
        x2 = kr * FP8_K_LO
        x3 = kr - x1.astype(F8).astype(f32) * (1.0 / FP8_K_HI)
        ones_f = jnp.where(lane == HEAD_DIM, 1.0, 0.0)
        norms2 = []
        for g in range(AX_KV_HEADS):
            if g == 0:
                c0 = jnp.where(lane < HEAD_DIM, x1, pltpu.roll(x2, HEAD_DIM, 1))
                c1 = jnp.where(lane < HEAD_DIM, x3, ones_f)
            else:
                c0 = jnp.where(lane < HEAD_DIM, pltpu.roll(x1, HEAD_DIM, 1), x2)
                c1 = jnp.where(lane < HEAD_DIM, pltpu.roll(x3, HEAD_DIM, 1), ones_f)
            c0 = c0.astype(F8)
            c1 = c1.astype(F8)
            k8_ref[0, g, rows] = jnp.concatenate([c0, c1], axis=1)
            c0f = c0.astype(f32)
            c1f = jnp.where(lane < HEAD_DIM, c1.astype(f32), 0.0)
            norms2.append(jnp.max(jnp.sum(c0f * c0f + c1f * c1f, axis=1, keepdims=True),
                                  axis=0, keepdims=True))
        first_tk = (j * IN_SUB) // AX_TK
        off = (j * IN_SUB) % AX_TK
        vct_ref[0, first_tk, :, off:off + IN_SUB] = proj[:, OFF_VC:OFF_VC + AX_KW].T.astype(bf)
        return jnp.where(lane[:1] < HEAD_DIM, norms2[0], norms2[1])

    proj = project(normed(0))
    tile_max = None
    for j in range(n_sub):
        if j + 1 < n_sub:
            h_next = normed(j + 1)
        stats = head_stats(proj)
        if j + 1 < n_sub:
            proj_next = project(h_next)
        sub_max = finish(j, proj, stats)
        tile_max = sub_max if tile_max is None else jnp.maximum(tile_max, sub_max)
        if j + 1 < n_sub:
            proj = proj_next

    first = pl.program_id(1) == 0

    @pl.when(first)
    def _():
        kmax_ref[0] = tile_max

    @pl.when(jnp.logical_not(first))
    def _():
        kmax_ref[0] = jnp.maximum(kmax_ref[0], tile_max)


def _in_proj(x, sc, sh, g, w, layer, cos, sin, gq, gk, bdq, bdk):
    B, S, D = x.shape
    tm = IN_TILE
    bf = jnp.bfloat16
    tok = lambda width: pl.BlockSpec((1, tm, width), lambda b, i: (b, i, 0))
    vec = pl.BlockSpec((1, 1, D), lambda b, i: (b, 0, 0))
    out_shape = (
        jax.ShapeDtypeStruct((B, S, NA_W), bf), jax.ShapeDtypeStruct((B, S, NA_W), bf),
        jax.ShapeDtypeStruct((B, S // LANES, NA_W, LANES), bf),
        jax.ShapeDtypeStruct((B, S, SW_QW), bf), jax.ShapeDtypeStruct((B, S, SW_KW), bf),
        jax.ShapeDtypeStruct((B, S // SW_BLOCK, SW_KW, SW_BLOCK), bf),
        jax.ShapeDtypeStruct((B, AX_QW, S), bf),
        jax.ShapeDtypeStruct((B, AX_KV_HEADS, S, LANES), bf),
        jax.ShapeDtypeStruct((B, S // AX_TK, AX_KW, AX_TK), bf),
        jax.ShapeDtypeStruct((B, 8, S), jnp.float32),
        jax.ShapeDtypeStruct((B, 1, AX_KW), jnp.float32),
        jax.ShapeDtypeStruct((B, 3, AX_QW, S), F8),
        jax.ShapeDtypeStruct((B, AX_KV_HEADS, S, 2 * LANES), F8),
    )
    out_specs = (
        tok(NA_W), tok(NA_W),
        pl.BlockSpec((1, tm // LANES, NA_W, LANES), lambda b, i: (b, i, 0, 0)),
        tok(SW_QW), tok(SW_KW),
        pl.BlockSpec((1, tm // SW_BLOCK, SW_KW, SW_BLOCK), lambda b, i: (b, i, 0, 0)),
        pl.BlockSpec((1, AX_QW, tm), lambda b, i: (b, 0, i)),
        pl.BlockSpec((1, AX_KV_HEADS, tm, LANES), lambda b, i: (b, 0, i, 0)),
        pl.BlockSpec((1, tm // AX_TK, AX_KW, AX_TK), lambda b, i: (b, i, 0, 0)),
        pl.BlockSpec((1, 8, tm), lambda b, i: (b, 0, i)),
        pl.BlockSpec((1, 1, AX_KW), lambda b, i: (b, 0, 0)),
        pl.BlockSpec((1, 3, AX_QW, tm), lambda b, i: (b, 0, 0, i)),
        pl.BlockSpec((1, AX_KV_HEADS, tm, 2 * LANES), lambda b, i: (b, 0, i, 0)),
    )
    return pl.pallas_call(
        _in_kernel,
        out_shape=out_shape,
        grid=(B, S // tm),
        in_specs=[tok(D), vec, vec, _const_spec((1, D)), _layer_spec(w, layer),
                  pl.BlockSpec((tm, LANES), lambda b, i: (i, 0)),
                  pl.BlockSpec((tm, LANES), lambda b, i: (i, 0)),
                  _const_spec((1, LANES)), _const_spec((1, LANES)),
                  _const_spec(bdq.shape), _const_spec(bdk.shape)],
        out_specs=out_specs,
        compiler_params=_cparams("arbitrary", "arbitrary"),
        name="in_proj",
    )(x, sc, sh, g, w, cos, sin, gq, gk, bdq, bdk)


def _na_build_bias(rp_ref, bias_ref, rows):
    shape = (GRID_W, LANES)
    kc = lax.broadcasted_iota(jnp.int32, shape, 0)
    lane = lax.broadcasted_iota(jnp.int32, shape, 1)
    c = lane % GRID_W
    cs = jnp.clip(c - NA_WIN_COLS // 2, 0, GRID_W - NA_WIN_COLS)
    col_ok = (kc >= cs) & (kc < cs + NA_WIN_COLS)
    left = lane < GRID_W
    neg = jnp.full(shape, NEG_INF, jnp.float32)
    tq = NA_QROWS * GRID_W
    for h in range(NA_HEADS):
        pair = []
        for a in range(2 * NA_WIN_ROWS):
            x = jnp.broadcast_to(rp_ref[h, a:a + 1, :] * LOG2E, shape)
            t = pltpu.roll(x, LANES - (NA_WIN_COLS - 1), 1, stride=1, stride_axis=0)
            pair.append(jnp.where(col_ok, t, neg))
        for variant, qr0 in enumerate((0, 2 * NA_QROWS, rows - NA_QROWS)):
            bs = int(np.clip(qr0 - NA_WIN_ROWS // 2, 0, rows - NA_BAND))
            for kj in range(NA_BAND):
                kr = bs + kj
                for u in range(NA_QROWS // 2):
                    r0 = qr0 + 2 * u
                    ok = [int(np.clip(r - NA_WIN_ROWS // 2, 0, rows - NA_WIN_ROWS)) <= kr
                          < int(np.clip(r - NA_WIN_ROWS // 2, 0, rows - NA_WIN_ROWS)) + NA_WIN_ROWS
                          for r in (r0, r0 + 1)]
                    a0 = kr - r0 + NA_WIN_ROWS - 1
                    if ok[0] and ok[1]:
                        tile = pair[a0]
                    elif ok[0]:
                        tile = jnp.where(left, pair[a0], neg)
                    elif ok[1]:
                        tile = jnp.where(left, neg, pair[a0])
                    else:
                        tile = neg
                    bias_ref[variant, kj * GRID_W:(kj + 1) * GRID_W,
                             h * tq + u * LANES:h * tq + (u + 1) * LANES] = tile


def _na_kernel(q_ref, k_ref, vt_ref, rp_ref, o_ref, bias_ref, *, rows):
    i = pl.program_id(1)
    tq = NA_QROWS * GRID_W
    nblk = NA_TQ // tq
    nblk_total = rows // NA_QROWS
    bf = jnp.bfloat16

    @pl.when((pl.program_id(0) == 0) & (i == 0))
    def _():
        _na_build_bias(rp_ref, bias_ref, rows)

    head_of_lane = lax.broadcasted_iota(jnp.int32, (tq, NA_W), 1) // HEAD_DIM
    ones = jnp.ones((16, NA_BAND * GRID_W), bf)

    def band_row(blk):
        return jnp.clip((i * nblk + blk) * NA_QROWS - NA_WIN_ROWS // 2, 0, rows - NA_BAND)

    def scores(blk):
        q = q_ref[0, blk * tq:(blk + 1) * tq, :]
        stack = jnp.concatenate([jnp.where(head_of_lane == h, q, jnp.zeros_like(q))
                                 for h in range(NA_HEADS)], axis=0)
        start = pl.multiple_of(band_row(blk) * GRID_W, NA_QROWS * GRID_W)
        kb = k_ref[0, pl.ds(start, NA_BAND * GRID_W), :]
        return lax.dot_general(kb, stack, (((1,), (1,)), ((), ())),
                               preferred_element_type=jnp.float32)

    pending = [scores(blk) for blk in range(min(NA_LOOKAHEAD, nblk))]
    for blk in range(nblk):
        s = pending.pop(0)
        if blk + NA_LOOKAHEAD < nblk:
            pending.append(scores(blk + NA_LOOKAHEAD))
        n = i * nblk + blk
        variant = jnp.where(n == 0, 0, jnp.where(n == nblk_total - 1, 2, 1))
        s = s + bias_ref[variant]
        m = jnp.max(s, axis=0, keepdims=True)
        p = jnp.exp2(s - m).astype(bf)
        first = band_row(blk) * GRID_W // LANES
        vt = jnp.concatenate([vt_ref[0, first + t] for t in range(NA_BAND * GRID_W // LANES)],
                             axis=1)
        outs = []
        for h in range(NA_HEADS):
            vt_ext = jnp.concatenate([vt[h * HEAD_DIM:(h + 1) * HEAD_DIM], ones], axis=0)
            o = jnp.dot(vt_ext, p[:, h * tq:(h + 1) * tq], preferred_element_type=jnp.float32)
            outs.append(o[:HEAD_DIM] / o[HEAD_DIM:HEAD_DIM + 1])
        o_ref[0, blk * tq:(blk + 1) * tq, :] = jnp.concatenate(outs, axis=0).T


def _na_attention(q, k, vt, rp):
    B, S, W = q.shape
    rows = S // GRID_W
    tq = NA_QROWS * GRID_W
    return pl.pallas_call(
        functools.partial(_na_kernel, rows=rows),
        out_shape=jax.ShapeDtypeStruct((B, S, W), jnp.float32),
        grid=(B, S // NA_TQ),
        in_specs=[pl.BlockSpec((1, NA_TQ, W), lambda b, j: (b, j, 0)),
                  pl.BlockSpec((1, S, W), lambda b, j: (b, 0, 0), pipeline_mode=pl.Buffered(1)),
                  pl.BlockSpec((1, S // LANES, W, LANES), lambda b, j: (b, 0, 0, 0),
                               pipeline_mode=pl.Buffered(1)),
                  _const_spec(rp.shape)],
        out_specs=pl.BlockSpec((1, NA_TQ, W), lambda b, j: (b, j, 0)),
        scratch_shapes=[pltpu.VMEM((3, NA_BAND * GRID_W, NA_HEADS * tq), jnp.float32)],
        compiler_params=_cparams("arbitrary", "arbitrary"),
        name="na_attn",
    )(q, k, vt, rp)


def _na_rpb_rows(rpb):
    n_col = rpb.shape[-1]
    p = jnp.pad(rpb.astype(jnp.float32)[:, :, ::-1], ((0, 0), (1, 1), (0, GRID_W - n_col)))
    return jnp.concatenate([p[:, 1:], p[:, :-1]], axis=-1)


def _sw_kernel(sink_ref, t5_ref, q_ref, k_ref, vt_ref, bucket_ref, o_ref, bias_ref, *, seq):
    i = pl.program_id(1)
    nblk_total = seq // SW_BLOCK

    @pl.when((pl.program_id(0) == 0) & (i == 0))
    def _():
        bucket = bucket_ref[...]
        for h in range(SW_HEADS):
            acc = jnp.full(bucket.shape, NEG_INF, jnp.float32)
            for b in range(T5_BUCKETS):
                acc = jnp.where(bucket == b, t5_ref[b * SW_HEADS + h] * LOG2E, acc)
            bias_ref[:, h * SW_BLOCK:(h + 1) * SW_BLOCK] = acc

    nblk = SW_TQ // SW_BLOCK
    rep = SW_HEADS // SW_KV_HEADS
    bf = jnp.bfloat16
    width = SW_HEADS * SW_BLOCK
    lane_group = lax.broadcasted_iota(jnp.int32, (SW_BLOCK, LANES), 1) // HEAD_DIM
    row_group = lax.broadcasted_iota(jnp.int32, (SW_BLOCK, LANES), 0) // HEAD_DIM
    key_row = lax.broadcasted_iota(jnp.int32, (3 * SW_BLOCK, width), 0)
    sink = jnp.concatenate([jnp.full((1, SW_BLOCK), sink_ref[h] * LOG2E, jnp.float32)
                            for h in range(SW_HEADS)], axis=1)
    ones = jnp.ones((16, 3 * SW_BLOCK), bf)

    def neighbours(blk):
        n = i * nblk + blk
        return jnp.maximum(n - 1, 0), n, jnp.minimum(n + 1, nblk_total - 1)

    def scores(blk):
        rows = slice(blk * SW_BLOCK, (blk + 1) * SW_BLOCK)
        stack = []
        for g in range(SW_KV_HEADS):
            for r in range(rep):
                qcol = q_ref[0, rows, r * LANES:(r + 1) * LANES]
                stack.append(jnp.where(lane_group == g, qcol, jnp.zeros_like(qcol)))
        kw = jnp.concatenate([k_ref[0, pl.ds(pl.multiple_of(nb * SW_BLOCK, SW_BLOCK), SW_BLOCK), :]
                              for nb in neighbours(blk)], axis=0)
        return lax.dot_general(kw, jnp.concatenate(stack, axis=0), (((1,), (1,)), ((), ())),
                               preferred_element_type=jnp.float32)

    pending = [scores(blk) for blk in range(min(SW_LOOKAHEAD, nblk))]
    for blk in range(nblk):
        s = pending.pop(0)
        if blk + SW_LOOKAHEAD < nblk:
            pending.append(scores(blk + SW_LOOKAHEAD))
        s = s + bias_ref[...]
        left, n, right = neighbours(blk)
        if blk in (0, nblk - 1):
            outside = (((n == 0) & (key_row < SW_BLOCK))
                       | ((n == nblk_total - 1) & (key_row >= 2 * SW_BLOCK)))
            s = jnp.where(outside, NEG_INF, s)
        m = jnp.maximum(jnp.max(s, axis=0, keepdims=True), sink)
        p = jnp.exp2(s - m).astype(bf)
        vt_ext = jnp.concatenate(
            [jnp.concatenate([vt_ref[0, left], vt_ref[0, n], vt_ref[0, right]], axis=1), ones], axis=0)
        o = jnp.dot(vt_ext, p, preferred_element_type=jnp.float32)
        res = o[:LANES] / (o[LANES:LANES + 1] + jnp.exp2(sink - m))
        for r in range(rep):
            yt = jnp.where(row_group == 0, res[:, r * SW_BLOCK:(r + 1) * SW_BLOCK],
                           res[:, (rep + r) * SW_BLOCK:(rep + r + 1) * SW_BLOCK])
            o_ref[0, blk * SW_BLOCK:(blk + 1) * SW_BLOCK, r * LANES:(r + 1) * LANES] = yt.T


def _sw_attention(q, k, vt, bucket, t5_flat, sink):
    B, S, _ = q.shape
    nb = S // SW_BLOCK
    return pl.pallas_call(
        functools.partial(_sw_kernel, seq=S),
        out_shape=jax.ShapeDtypeStruct((B, S, SW_QW), jnp.float32),
        grid_spec=pltpu.PrefetchScalarGridSpec(
            num_scalar_prefetch=2,
            grid=(B, S // SW_TQ),
            in_specs=[pl.BlockSpec((1, SW_TQ, SW_QW), lambda b, i, *_: (b, i, 0)),
                      pl.BlockSpec((1, S, SW_KW), lambda b, i, *_: (b, 0, 0),
                                   pipeline_mode=pl.Buffered(1)),
                      pl.BlockSpec((1, nb, SW_KW, SW_BLOCK), lambda b, i, *_: (b, 0, 0, 0),
                                   pipeline_mode=pl.Buffered(1)),
                      pl.BlockSpec(bucket.shape, lambda b, i, *_: (0, 0),
                                   pipeline_mode=pl.Buffered(1))],
            out_specs=pl.BlockSpec((1, SW_TQ, SW_QW), lambda b, i, *_: (b, i, 0)),
            scratch_shapes=[pltpu.VMEM((3 * SW_BLOCK, SW_HEADS * SW_BLOCK), jnp.float32)],
        ),
        compiler_params=_cparams("arbitrary", "arbitrary"),
        name="sw_attn",
    )(sink, t5_flat, q, k, vt, bucket)


def _t5_bucket(rel):
    nb = T5_BUCKETS // 2
    ret = (rel > 0).astype(jnp.int32) * nb
    n = jnp.abs(rel)
    max_exact = nb // 2
    nf = jnp.maximum(n, max_exact).astype(jnp.float32)
    large = max_exact + (jnp.log(nf / max_exact) / math.log(T5_MAX_DIST / max_exact)
                         * (nb - max_exact)).astype(jnp.int32)
    large = jnp.minimum(large, nb - 1)
    return ret + jnp.where(n < max_exact, n, large)


def _sw_bucket_tile():
    qpos = jnp.arange(SW_BLOCK)
    kpos = jnp.arange(3 * SW_BLOCK) - SW_BLOCK
    rel = kpos[:, None] - qpos[None, :]
    return jnp.where(jnp.abs(rel) <= SW_RADIUS, _t5_bucket(rel), -1).astype(jnp.int32)


def _ax_kernel(q_ref, qnorm_ref, kmax_ref, k_ref, vt_ref, inv_ref, o_ref, qpad_ref, m_ref, l_ref,
               acc_ref, s0_ref, *, nkt, bounded):
    tq = q_ref.shape[-1]
    tk = AX_TK
    rep = AX_HEADS // AX_KV_HEADS
    bf = jnp.bfloat16
    inv = inv_ref[...]
    for h in range(AX_HEADS):
        g = h // rep
        head = slice(h * HEAD_DIM, (h + 1) * HEAD_DIM)
        if bounded:
            row = lax.broadcasted_iota(jnp.int32, (32, tq), 0)
            shift = qnorm_ref[0, h:h + 1, :] * jnp.sqrt(kmax_ref[0, :, g * HEAD_DIM:g * HEAD_DIM + 1])
            extra = jnp.where(row == 0, -shift, 0.0).astype(F8)
            qpad_ref[h] = jnp.concatenate(
                [q_ref[0, 0, head, :], q_ref[0, 1, head, :], q_ref[0, 2, head, :], extra,
                 jnp.zeros((32, tq), F8)], axis=0)
        else:
            qpad_ref[h] = jnp.concatenate(
                [q_ref[0, head, :], jnp.zeros((LANES - HEAD_DIM, tq), bf)], axis=0)
    if not bounded:
        m_ref[...] = jnp.full(m_ref.shape, -jnp.inf, jnp.float32)
    l_ref[...] = jnp.zeros(l_ref.shape, jnp.float32)
    acc_ref[...] = jnp.zeros(acc_ref.shape, jnp.float32)

    def score(kt, h):
        ks = pl.multiple_of(kt * tk, tk)
        return jnp.dot(k_ref[0, h // rep, pl.ds(ks, tk), :], qpad_ref[h],
                       preferred_element_type=jnp.float32)

    for h in range(AX_LOOKAHEAD):
        s0_ref[h] = score(0, h)

    def body(kt, carry):
        pending = [s0_ref[h] for h in range(AX_LOOKAHEAD)]
        for h in range(AX_HEADS):
            g = h // rep
            s = pending.pop(0)
            ahead = h + AX_LOOKAHEAD
            if ahead < AX_HEADS:
                pending.append(score(kt, ahead))
            else:
                s0_ref[ahead - AX_HEADS] = score(jnp.minimum(kt + 1, nkt - 1), ahead - AX_HEADS)
            vt = vt_ref[0, kt, g * HEAD_DIM:(g + 1) * HEAD_DIM, :]
            s = s * inv
            if bounded:
                p = jnp.exp2(s)
                l_ref[h] += jnp.sum(p.reshape(tk // 8, 8, tq), axis=0)
                acc_ref[h] += jnp.dot(vt, p.astype(bf), preferred_element_type=jnp.float32)
            else:
                m_old = m_ref[h]
                m_new = jnp.maximum(m_old, jnp.max(s, axis=0, keepdims=True))
                alpha = jnp.exp2(m_old - m_new)
                p = jnp.exp2(s - m_new)
                l_ref[h] = alpha * l_ref[h] + jnp.sum(p.reshape(tk // 8, 8, tq), axis=0)
                m_ref[h] = m_new
                pv = jnp.dot(vt, p.astype(bf), preferred_element_type=jnp.float32)
                acc_ref[h] = alpha * acc_ref[h] + pv
        return carry

    lax.fori_loop(0, nkt, body, 0, unroll=AX_UNROLL)
    out_t = jnp.concatenate([acc_ref[h] / jnp.sum(l_ref[h], axis=0, keepdims=True)
                             for h in range(AX_HEADS)], axis=0)
    o_ref[0] = out_t.T


def _ax_attention(qt, k, q8, k8, qnorm, kmax, vt, inv, bounded):
    B, W, S = qt.shape
    nkt = vt.shape[1]
    if bounded:
        q, keys = q8, k8
        q_spec = pl.BlockSpec((1, 3, W, AX_TQ), lambda b, i: (b, 0, 0, i))
        qpad = pltpu.VMEM((AX_HEADS, 2 * LANES, AX_TQ), F8)
    else:
        q, keys = qt, k
        q_spec = pl.BlockSpec((1, W, AX_TQ), lambda b, i: (b, 0, i))
        qpad = pltpu.VMEM((AX_HEADS, LANES, AX_TQ), jnp.bfloat16)
    return pl.pallas_call(
        functools.partial(_ax_kernel, nkt=nkt, bounded=bounded),
        out_shape=jax.ShapeDtypeStruct((B, S, W), jnp.float32),
        grid=(B, S // AX_TQ),
        in_specs=[q_spec,
                  pl.BlockSpec((1, 8, AX_TQ), lambda b, i: (b, 0, i)),
                  pl.BlockSpec((1, 1, AX_KW), lambda b, i: (b, 0, 0)),
                  pl.BlockSpec((1,) + keys.shape[1:], lambda b, i: (b, 0, 0, 0),
                               pipeline_mode=pl.Buffered(1)),
                  pl.BlockSpec((1, nkt, AX_KW, AX_TK), lambda b, i: (b, 0, 0, 0),
                               pipeline_mode=pl.Buffered(1)),
                  _const_spec((1, 1))],
        out_specs=pl.BlockSpec((1, AX_TQ, W), lambda b, i: (b, i, 0)),
        scratch_shapes=[qpad,
                        pltpu.VMEM((AX_HEADS, 1, AX_TQ), jnp.float32),
                        pltpu.VMEM((AX_HEADS, 8, AX_TQ), jnp.float32),
                        pltpu.VMEM((AX_HEADS, HEAD_DIM, AX_TQ), jnp.float32),
                        pltpu.VMEM((AX_LOOKAHEAD, AX_TK, AX_TQ), jnp.float32)],
        compiler_params=_cparams("arbitrary", "arbitrary"),
        name="ax_attn_bounded" if bounded else "ax_attn_online",
    )(q, qnorm, kmax, keys, vt, inv)


def _ax_dispatch(qt, k, q8, k8, qnorm, kmax, vt, inv, gain_max):
    rep = AX_HEADS // AX_KV_HEADS
    knorm = jnp.sqrt(kmax[:, 0, ::HEAD_DIM])
    shift_max = jnp.max(jnp.max(qnorm[:, :AX_HEADS], axis=-1) * jnp.repeat(knorm, rep, axis=1))
    entry_max = HEAD_DIM ** 0.5 * gain_max * jnp.array([FP8_Q_HI * Q_SCALE_EXP2, 1.0])
    safe = ((shift_max * inv[0, 0] <= AX_MAX_SHIFT) & (shift_max <= F8_MAX / 2)
            & jnp.all(entry_max <= F8_MAX / 2))
    return lax.cond(safe,
                    functools.partial(_ax_attention, bounded=True),
                    functools.partial(_ax_attention, bounded=False),
                    qt, k, q8, k8, qnorm, kmax, vt, inv)


def _post_kernel(x_ref, ya_ref, yb_ref, yc_ref, gg_ref, wo_ref, gta_ref, gf_ref, scf_ref, shf_ref,
                 gtf_ref, wgu_ref, wd_ref, gfin_ref, o_ref, *, final):
    bf = jnp.bfloat16
    gg = gg_ref[...]
    y = jnp.concatenate([
        _rms(ya_ref[0], gg[:, 0:NA_W]).astype(bf),
        _rms(yb_ref[0], gg[:, NA_W:NA_W + SW_QW]).astype(bf),
        _rms(yc_ref[0], gg[:, NA_W + SW_QW:]).astype(bf)], axis=1)
    x1 = x_ref[0] + gta_ref[0] * jnp.dot(y, wo_ref[...], preferred_element_type=jnp.float32)

    h = (_rms(x1, gf_ref[...]) * (1.0 + scf_ref[0]) + shf_ref[0]).astype(bf)
    acc = jnp.zeros(x1.shape, jnp.float32)
    for c0 in range(0, FFN_HIDDEN, FFN_CHUNK):
        c1 = min(c0 + FFN_CHUNK, FFN_HIDDEN)
        gate = jnp.dot(h, wgu_ref[:, c0:c1], preferred_element_type=jnp.float32)
        up = jnp.dot(h, wgu_ref[:, FFN_HIDDEN + c0:FFN_HIDDEN + c1],
                     preferred_element_type=jnp.float32)
        act = (gate * (1.0 / (1.0 + jnp.exp(-gate))) * up).astype(bf)
        acc = acc + jnp.dot(act, wd_ref[c0:c1, :], preferred_element_type=jnp.float32)
    x2 = x1 + gtf_ref[0] * acc
    if final:
        x2 = _rms(x2, gfin_ref[...])
    o_ref[0] = x2


def _post(x, ya, yb, yc, gg, wo, gta, gf, scf, shf, gtf, wgu, wd, gfin, layer, final):
    B, S, D = x.shape
    tm = TOK_TILE
    tok = lambda width: pl.BlockSpec((1, tm, width), lambda b, i: (b, i, 0))
    vec = pl.BlockSpec((1, 1, D), lambda b, i: (b, 0, 0))
    return pl.pallas_call(
        functools.partial(_post_kernel, final=final),
        out_shape=jax.ShapeDtypeStruct((B, S, D), jnp.float32),
        grid=(B, S // tm),
        in_specs=[tok(D), tok(NA_W), tok(SW_QW), tok(AX_QW), _const_spec((1, D)),
                  _layer_spec(wo, layer), vec, _const_spec((1, D)), vec, vec, vec,
                  _layer_spec(wgu, layer), _layer_spec(wd, layer), _const_spec((1, D))],
        out_specs=tok(D),
        compiler_params=_cparams("arbitrary", "arbitrary"),
        name="out_proj_ffn",
    )(x, ya, yb, yc, gg, wo, gta, gf, scf, shf, gtf, wgu, wd, gfin)


def _rope_tables(S):
    t = jnp.arange(S)
    row = (t // GRID_W).astype(jnp.float32)
    col = (t % GRID_W).astype(jnp.float32)
    axis_dim = HEAD_DIM // 2
    freqs = ROPE_THETA ** (-jnp.arange(0, axis_dim, 2, dtype=jnp.float32) / axis_dim)
    ang = jnp.stack([row[:, None] * freqs, col[:, None] * freqs], axis=1)
    cos = jnp.cos(ang)
    sin = jnp.sin(ang)
    cos_h = jnp.concatenate([cos, cos], axis=-1).reshape(S, HEAD_DIM)
    sin_h = jnp.concatenate([-sin, sin], axis=-1).reshape(S, HEAD_DIM)
    return jnp.tile(cos_h, (1, LANES // HEAD_DIM)), jnp.tile(sin_h, (1, LANES // HEAD_DIM))


def _pow2_normaliser(g):
    r = jnp.sqrt(jnp.mean(g.astype(jnp.float32) ** 2))
    return jnp.exp2(-jnp.round(jnp.log2(jnp.maximum(r, 1e-30))))


def _block_diag_ones(width):
    idx = np.arange(width) // HEAD_DIM
    return jnp.asarray(idx[:, None] == idx[None, :], dtype=jnp.bfloat16)


def _sw_relabel(t, start, axis):
    rep = SW_HEADS // SW_KV_HEADS
    seg = lax.slice_in_dim(t, start, start + SW_QW, axis=axis)
    shp = seg.shape
    seg = seg.reshape(shp[:axis] + (SW_KV_HEADS, rep, HEAD_DIM) + shp[axis + 1:])
    seg = jnp.swapaxes(seg, axis, axis + 1).reshape(shp)
    return jnp.concatenate([lax.slice_in_dim(t, 0, start, axis=axis), seg,
                            lax.slice_in_dim(t, start + SW_QW, t.shape[axis], axis=axis)], axis=axis)


def kernel(x, c, w_mod, b_mod, g_attn, w_in, rpb_na, sink_sw, t5_table, gq_ax, gk_ax, g_group,
           w_o, g_ffn, w_gu, w_down, g_final):
    B, S, D = x.shape
    L = w_mod.shape[0]
    bf = jnp.bfloat16

    c_pad = jnp.pad(c, ((0, 8 - B), (0, 0)))
    mod = _modulation(c_pad, w_mod, b_mod)[:, :B]
    mod = mod.reshape(L, B, 6, 1, D)

    cos, sin = _rope_tables(S)
    bdq = _block_diag_ones(AX_QW)
    bdk = _block_diag_ones(AX_KW)
    sw_bucket = _sw_bucket_tile()
    t5_flat = t5_table.astype(jnp.float32).reshape(-1)

    w_in_p = _sw_relabel(w_in, OFF_QB, axis=2).astype(bf)
    w_o_p = _sw_relabel(w_o, NA_W, axis=1).astype(bf)
    g_group_p = _sw_relabel(g_group, NA_W, axis=1)
    w_gu_b = w_gu.astype(bf)
    w_down_b = w_down.astype(bf)

    for l in range(L):
        sh_a, sc_a, gt_a, sh_f, sc_f, gt_f = [mod[l, :, i] for i in range(6)]
        gq = gq_ax[l] * _pow2_normaliser(gq_ax[l])
        gk = gk_ax[l] * _pow2_normaliser(gk_ax[l])
        inv = (1.0 / (_pow2_normaliser(gq_ax[l]) * _pow2_normaliser(gk_ax[l]))).reshape(1, 1)
        gain_max = jnp.stack([jnp.max(jnp.abs(gq)), jnp.max(jnp.abs(gk))])
        qa, ka, va, qb, kb, vb, qct, kc, vct, qnorm, kmax, q8, k8 = _in_proj(
            x, sc_a, sh_a, g_attn[l].reshape(1, D), w_in_p, l, cos, sin,
            jnp.tile(gq, LANES // HEAD_DIM).reshape(1, LANES),
            jnp.tile(gk, LANES // HEAD_DIM).reshape(1, LANES), bdq, bdk)
        ya = _na_attention(qa, ka, va, _na_rpb_rows(rpb_na[l]))
        yb = _sw_attention(qb, kb, vb, sw_bucket, t5_flat, sink_sw[l])
        yc = _ax_dispatch(qct, kc, q8, k8, qnorm, kmax, vct, inv, gain_max)
        x = _post(x, ya, yb, yc, g_group_p[l].reshape(1, D), w_o_p, gt_a, g_ffn[l].reshape(1, D),
                  sc_f, sh_f, gt_f, w_gu_b, w_down_b, g_final.reshape(1, D), layer=l,
                  final=(l == L - 1))
    return x
```

```python
import functools
import math

import jax
import jax.numpy as jnp
import numpy as np
from jax import lax
from jax.experimental import pallas as pl
from jax.experimental.pallas import tpu as pltpu

D_MODEL = 1024
HEAD_DIM = 64
GRID_W = 64
NA_HEADS = 4
SW_HEADS = 6
SW_KV_HEADS = 2
AX_HEADS = 6
AX_KV_HEADS = 2
NA_WIN_ROWS = 8
NA_WIN_COLS = 16
SW_RADIUS = 128
SW_BLOCK = 128
T5_BUCKETS = 32
T5_MAX_DIST = 128
ROPE_THETA = 10000.0
FFN_HIDDEN = 2816
EPS = 1e-6
NEG_INF = -1e30

NA_W = NA_HEADS * HEAD_DIM
SW_QW = SW_HEADS * HEAD_DIM
SW_KW = SW_KV_HEADS * HEAD_DIM
AX_QW = AX_HEADS * HEAD_DIM
AX_KW = AX_KV_HEADS * HEAD_DIM
IN_WIDTH = 3 * NA_W + SW_QW + 2 * SW_KW + AX_QW + 2 * AX_KW
OFF_QA, OFF_KA, OFF_VA = 0, NA_W, 2 * NA_W
OFF_QB = 3 * NA_W
OFF_KB = OFF_QB + SW_QW
OFF_VB = OFF_KB + SW_KW
OFF_QC = OFF_VB + SW_KW
OFF_KC = OFF_QC + AX_QW
OFF_VC = OFF_KC + AX_KW

LANES = 128
VMEM_LIMIT = 56 * 1024 * 1024

TOK_TILE = 512
IN_TILE = 1024
IN_SUB = 256
AX_TQ = 256
AX_TK = 512
AX_LOOKAHEAD = 2
AX_UNROLL = 8
AX_ONES_ROWS = 16
NA_QROWS = 4
NA_BAND = NA_QROWS + NA_WIN_ROWS
NA_TQ = 1024
NA_LOOKAHEAD = 1
SW_TQ = 1024
SW_LOOKAHEAD = 2
FFN_CHUNK = 512

QK_SCALE = HEAD_DIM ** -0.5
F8 = jnp.float8_e4m3fn
F8_MAX = 448.0
FP8_Q_HI, FP8_K_HI = 2.0, 0.5
FP8_Q_LO, FP8_K_LO = 16.0, 1.0 / 16.0
LOG2E = math.log2(math.e)
Q_SCALE_EXP2 = QK_SCALE * LOG2E
AX_MAX_SHIFT = 60.0


def _cparams(*sem):
    return pltpu.CompilerParams(dimension_semantics=sem, vmem_limit_bytes=VMEM_LIMIT)


def _const_spec(shape):
    n = len(shape)
    return pl.BlockSpec(shape, lambda *_: (0,) * n, pipeline_mode=pl.Buffered(1))


def _layer_spec(stacked, layer):
    n = stacked.ndim - 1
    return pl.BlockSpec((None,) + stacked.shape[1:], lambda *_: (layer,) + (0,) * n,
                        pipeline_mode=pl.Buffered(1))


def _rms(x, g):
    return x * lax.rsqrt(jnp.mean(x * x, axis=-1, keepdims=True) + EPS) * g


def _mod_kernel(c_ref, w_ref, b_ref, o_ref):
    c = c_ref[...]
    cond = c * (1.0 / (1.0 + jnp.exp(-c)))
    o_ref[0] = jnp.dot(cond, w_ref[0], preferred_element_type=jnp.float32,
                       precision=lax.Precision.HIGHEST) + b_ref[0]


def _modulation(c_pad, w_mod, b_mod):
    L, D, N = w_mod.shape
    tn = 1536
    return pl.pallas_call(
        _mod_kernel,
        out_shape=jax.ShapeDtypeStruct((L, c_pad.shape[0], N), jnp.float32),
        grid=(L, N // tn),
        in_specs=[pl.BlockSpec(c_pad.shape, lambda l, j: (0, 0)),
                  pl.BlockSpec((1, D, tn), lambda l, j: (l, 0, j)),
                  pl.BlockSpec((1, 1, tn), lambda l, j: (l, 0, j))],
        out_specs=pl.BlockSpec((1, c_pad.shape[0], tn), lambda l, j: (l, 0, j)),
        compiler_params=_cparams("arbitrary", "arbitrary"),
        name="adaln_mod",
    )(c_pad, w_mod, b_mod.reshape(L, 1, N))


def _head_sumsq(t, bd):
    t2 = (t * t).astype(jnp.bfloat16)
    return jnp.concatenate(
        [jnp.dot(t2[:, c:c + LANES], bd, preferred_element_type=jnp.float32)
         for c in range(0, t.shape[1], LANES)], axis=1)


def _rope_chunk(t, cos, sin_signed, first_half):
    swapped = jnp.where(first_half, pltpu.roll(t, LANES - 16, 1), pltpu.roll(t, 16, 1))
    return t * cos + swapped * sin_signed


def _in_kernel(x_ref, sc_ref, sh_ref, g_ref, w_ref, cos_ref, sin_ref, gq_ref, gk_ref,
               bd_ref,
               qa_ref, ka_ref, vat_ref, qb_ref, kb_ref, vbt_ref, qct_ref, kc_ref, vct_ref,
               qnorm_ref, kmax_ref, q8_ref, k8_ref):
    bf = jnp.bfloat16
    n_sub = x_ref.shape[1] // IN_SUB
    lane = lax.broadcasted_iota(jnp.int32, (IN_SUB, LANES), 1)
    first_half = (lane % 32) < 16

    def normed(j):
        x = x_ref[0, j * IN_SUB:(j + 1) * IN_SUB, :]
        return (_rms(x, g_ref[...]) * (1.0 + sc_ref[0]) + sh_ref[0]).astype(bf)

    def project(h):
        return jnp.dot(h, w_ref[...], preferred_element_type=jnp.float32)

    def head_stats(proj):
        return (_head_sumsq(proj[:, OFF_QC:OFF_QC + AX_QW], bd_ref[...]),
                _head_sumsq(proj[:, OFF_KC:OFF_KC + AX_KW], bd_ref[...]))

    def finish(j, proj, stats):
        rows = slice(j * IN_SUB, (j + 1) * IN_SUB)
        qa_ref[0, rows] = (proj[:, OFF_QA:OFF_QA + NA_W] * Q_SCALE_EXP2).astype(bf)
        ka_ref[0, rows] = proj[:, OFF_KA:OFF_KA + NA_W].astype(bf)
        qb_ref[0, rows] = (proj[:, OFF_QB:OFF_QB + SW_QW] * Q_SCALE_EXP2).astype(bf)
        kb_ref[0, rows] = proj[:, OFF_KB:OFF_KB + SW_KW].astype(bf)
        per_sub = IN_SUB // LANES
        for t in range(per_sub):
            blk = slice(t * LANES, (t + 1) * LANES)
            vat_ref[0, j * per_sub + t] = proj[blk, OFF_VA:OFF_VA + NA_W].T.astype(bf)
            vbt_ref[0, j * per_sub + t] = proj[blk, OFF_VB:OFF_VB + SW_KW].T.astype(bf)

        cos = cos_ref[rows, :]
        sin = sin_ref[rows, :]
        qss, kss = stats
        qn = proj[:, OFF_QC:OFF_QC + AX_QW] * lax.rsqrt(qss * (1.0 / HEAD_DIM) + EPS)
        chunks = []
        for c in range(AX_QW // LANES):
            t = qn[:, c * LANES:(c + 1) * LANES] * gq_ref[...]
            chunks.append(_rope_chunk(t, cos, sin, first_half) * Q_SCALE_EXP2)
        f32 = jnp.float32
        qt = jnp.concatenate(chunks, axis=1).T
        qct_ref[0, :, rows] = qt.astype(bf)
        a1 = (qt * FP8_Q_HI).astype(F8).astype(f32)
        a2 = ((qt - a1 * (1.0 / FP8_Q_HI)) * FP8_Q_LO).astype(F8).astype(f32)
        a3 = qt.astype(F8).astype(f32)
        q8_ref[0, 0, :, rows] = a1.astype(F8)
        q8_ref[0, 1, :, rows] = a2.astype(F8)
        q8_ref[0, 2, :, rows] = a3.astype(F8)
        qsq = a1 * a1 + a2 * a2 + a3 * a3
        norms = [jnp.sqrt(jnp.sum(qsq[hh * HEAD_DIM:(hh + 1) * HEAD_DIM], axis=0, keepdims=True))
                 for hh in range(AX_HEADS)]
        qnorm_ref[0, :, rows] = jnp.concatenate(
            norms + [jnp.zeros_like(norms[0])] * (8 - AX_HEADS), axis=0)

        kn = (proj[:, OFF_KC:OFF_KC + AX_KW] * lax.rsqrt(kss * (1.0 / HEAD_DIM) + EPS)
              * gk_ref[...])
        kr = _rope_chunk(kn, cos, sin, first_half)
        kb = kr.astype(bf)
        zero_bf = jnp.zeros_like(kb)
        kc_ref[0, 0, rows] = jnp.where(lane < HEAD_DIM, kb, zero_bf)
        kc_ref[0, 1, rows] = jnp.where(lane < HEAD_DIM, pltpu.roll(kr, HEAD_DIM, 1).astype(bf), zero_bf)
        x1 = kr * FP8_K_HI
        x2 = kr * FP8_K_LO
        x3 = kr - x1.astype(F8).astype(f32) * (1.0 / FP8_K_HI)
        ones_f = jnp.where(lane == HEAD_DIM, 1.0, 0.0)
        norms2 = []
        for g in range(AX_KV_HEADS):
            if g == 0:
                c0 = jnp.where(lane < HEAD_DIM, x1, pltpu.roll(x2, HEAD_DIM, 1))
                c1 = jnp.where(lane < HEAD_DIM, x3, ones_f)
            else:
                c0 = jnp.where(lane < HEAD_DIM, pltpu.roll(x1, HEAD_DIM, 1), x2)
                c1 = jnp.where(lane < HEAD_DIM, pltpu.roll(x3, HEAD_DIM, 1), ones_f)
            c0 = c0.astype(F8)
            c1 = c1.astype(F8)
            k8_ref[0, g, rows] = jnp.concatenate([c0, c1], axis=1)
            c0f = c0.astype(f32)
            c1f = jnp.where(lane < HEAD_DIM, c1.astype(f32), 0.0)
            norms2.append(jnp.max(jnp.sum(c0f * c0f + c1f * c1f, axis=1, keepdims=True),
                                  axis=0, keepdims=True))
        first_tk = (j * IN_SUB) // AX_TK
        off = (j * IN_SUB) % AX_TK
        vct_ref[0, first_tk, :, off:off + IN_SUB] = proj[:, OFF_VC:OFF_VC + AX_KW].T.astype(bf)
        return jnp.where(lane[:1] < HEAD_DIM, norms2[0], norms2[1])

    proj = project(normed(0))
    tile_max = None
    for j in range(n_sub):
        if j + 1 < n_sub:
            h_next = normed(j + 1)
        stats = head_stats(proj)
        if j + 1 < n_sub:
            proj_next = project(h_next)
        sub_max = finish(j, proj, stats)
        tile_max = sub_max if tile_max is None else jnp.maximum(tile_max, sub_max)
        if j + 1 < n_sub:
            proj = proj_next

    first = pl.program_id(1) == 0

    @pl.when(first)
    def _():
        kmax_ref[0] = tile_max

    @pl.when(jnp.logical_not(first))
    def _():
        kmax_ref[0] = jnp.maximum(kmax_ref[0], tile_max)


def _in_proj(x, sc, sh, g, w, layer, cos, sin, gq, gk, bd):
    B, S, D = x.shape
    tm = IN_TILE
    bf = jnp.bfloat16
    tok = lambda width: pl.BlockSpec((1, tm, width), lambda b, i: (b, i, 0))
    vec = pl.BlockSpec((1, 1, D), lambda b, i: (b, 0, 0))
    out_shape = (
        jax.ShapeDtypeStruct((B, S, NA_W), bf), jax.ShapeDtypeStruct((B, S, NA_W), bf),
        jax.ShapeDtypeStruct((B, S // LANES, NA_W, LANES), bf),
        jax.ShapeDtypeStruct((B, S, SW_QW), bf), jax.ShapeDtypeStruct((B, S, SW_KW), bf),
        jax.ShapeDtypeStruct((B, S // SW_BLOCK, SW_KW, SW_BLOCK), bf),
        jax.ShapeDtypeStruct((B, AX_QW, S), bf),
        jax.ShapeDtypeStruct((B, AX_KV_HEADS, S, LANES), bf),
        jax.ShapeDtypeStruct((B, S // AX_TK, AX_KW, AX_TK), bf),
        jax.ShapeDtypeStruct((B, 8, S), jnp.float32),
        jax.ShapeDtypeStruct((B, 1, AX_KW), jnp.float32),
        jax.ShapeDtypeStruct((B, 3, AX_QW, S), F8),
        jax.ShapeDtypeStruct((B, AX_KV_HEADS, S, 2 * LANES), F8),
    )
    out_specs = (
        tok(NA_W), tok(NA_W),
        pl.BlockSpec((1, tm // LANES, NA_W, LANES), lambda b, i: (b, i, 0, 0)),
        tok(SW_QW), tok(SW_KW),
        pl.BlockSpec((1, tm // SW_BLOCK, SW_KW, SW_BLOCK), lambda b, i: (b, i, 0, 0)),
        pl.BlockSpec((1, AX_QW, tm), lambda b, i: (b, 0, i)),
        pl.BlockSpec((1, AX_KV_HEADS, tm, LANES), lambda b, i: (b, 0, i, 0)),
        pl.BlockSpec((1, tm // AX_TK, AX_KW, AX_TK), lambda b, i: (b, i, 0, 0)),
        pl.BlockSpec((1, 8, tm), lambda b, i: (b, 0, i)),
        pl.BlockSpec((1, 1, AX_KW), lambda b, i: (b, 0, 0)),
        pl.BlockSpec((1, 3, AX_QW, tm), lambda b, i: (b, 0, 0, i)),
        pl.BlockSpec((1, AX_KV_HEADS, tm, 2 * LANES), lambda b, i: (b, 0, i, 0)),
    )
    return pl.pallas_call(
        _in_kernel,
        out_shape=out_shape,
        grid=(B, S // tm),
        in_specs=[tok(D), vec, vec, _const_spec((1, D)), _layer_spec(w, layer),
                  pl.BlockSpec((tm, LANES), lambda b, i: (i, 0)),
                  pl.BlockSpec((tm, LANES), lambda b, i: (i, 0)),
                  _const_spec((1, LANES)), _const_spec((1, LANES)),
                  _const_spec(bd.shape)],
        out_specs=out_specs,
        compiler_params=_cparams("arbitrary", "arbitrary"),
        name="in_proj",
    )(x, sc, sh, g, w, cos, sin, gq, gk, bd)


def _na_build_bias(rp_ref, bias_ref, rows):
    shape = (GRID_W, LANES)
    kc = lax.broadcasted_iota(jnp.int32, shape, 0)
    lane = lax.broadcasted_iota(jnp.int32, shape, 1)
    c = lane % GRID_W
    cs = jnp.clip(c - NA_WIN_COLS // 2, 0, GRID_W - NA_WIN_COLS)
    col_ok = (kc >= cs) & (kc < cs + NA_WIN_COLS)
    left = lane < GRID_W
    neg = jnp.full(shape, NEG_INF, jnp.float32)
    tq = NA_QROWS * GRID_W
    for h in range(NA_HEADS):
        pair = []
        for a in range(2 * NA_WIN_ROWS):
            x = jnp.broadcast_to(rp_ref[h, a:a + 1, :] * LOG2E, shape)
            t = pltpu.roll(x, LANES - (NA_WIN_COLS - 1), 1, stride=1, stride_axis=0)
            pair.append(jnp.where(col_ok, t, neg))
        for variant, qr0 in enumerate((0, 2 * NA_QROWS, rows - NA_QROWS)):
            bs = int(np.clip(qr0 - NA_WIN_ROWS // 2, 0, rows - NA_BAND))
            for kj in range(NA_BAND):
                kr = bs + kj
                for u in range(NA_QROWS // 2):
                    r0 = qr0 + 2 * u
                    ok = [int(np.clip(r - NA_WIN_ROWS // 2, 0, rows - NA_WIN_ROWS)) <= kr
                          < int(np.clip(r - NA_WIN_ROWS // 2, 0, rows - NA_WIN_ROWS)) + NA_WIN_ROWS
                          for r in (r0, r0 + 1)]
                    a0 = kr - r0 + NA_WIN_ROWS - 1
                    if ok[0] and ok[1]:
                        tile = pair[a0]
                    elif ok[0]:
                        tile = jnp.where(left, pair[a0], neg)
                    elif ok[1]:
                        tile = jnp.where(left, neg, pair[a0])
                    else:
                        tile = neg
                    bias_ref[variant, kj * GRID_W:(kj + 1) * GRID_W,
                             h * tq + u * LANES:h * tq + (u + 1) * LANES] = tile


def _na_kernel(q_ref, k_ref, vt_ref, rp_ref, o_ref, bias_ref, *, rows):
    i = pl.program_id(1)
    tq = NA_QROWS * GRID_W
    nblk = NA_TQ // tq
    nblk_total = rows // NA_QROWS
    bf = jnp.bfloat16

    @pl.when((pl.program_id(0) == 0) & (i == 0))
    def _():
        _na_build_bias(rp_ref, bias_ref, rows)

    head_of_lane = lax.broadcasted_iota(jnp.int32, (tq, NA_W), 1) // HEAD_DIM
    ones = jnp.ones((16, NA_BAND * GRID_W), bf)

    def band_row(blk):
        return jnp.clip((i * nblk + blk) * NA_QROWS - NA_WIN_ROWS // 2, 0, rows - NA_BAND)

    def scores(blk):
        q = q_ref[0, blk * tq:(blk + 1) * tq, :]
        stack = jnp.concatenate([jnp.where(head_of_lane == h, q, jnp.zeros_like(q))
                                 for h in range(NA_HEADS)], axis=0)
        start = pl.multiple_of(band_row(blk) * GRID_W, NA_QROWS * GRID_W)
        kb = k_ref[0, pl.ds(start, NA_BAND * GRID_W), :]
        return lax.dot_general(kb, stack, (((1,), (1,)), ((), ())),
                               preferred_element_type=jnp.float32)

    pending = [scores(blk) for blk in range(min(NA_LOOKAHEAD, nblk))]
    for blk in range(nblk):
        s = pending.pop(0)
        if blk + NA_LOOKAHEAD < nblk:
            pending.append(scores(blk + NA_LOOKAHEAD))
        n = i * nblk + blk
        variant = jnp.where(n == 0, 0, jnp.where(n == nblk_total - 1, 2, 1))
        s = s + bias_ref[variant]
        m = jnp.max(s, axis=0, keepdims=True)
        p = jnp.exp2(s - m).astype(bf)
        first = band_row(blk) * GRID_W // LANES
        vt = jnp.concatenate([vt_ref[0, first + t] for t in range(NA_BAND * GRID_W // LANES)],
                             axis=1)
        outs = []
        for h in range(NA_HEADS):
            vt_ext = jnp.concatenate([vt[h * HEAD_DIM:(h + 1) * HEAD_DIM], ones], axis=0)
            o = jnp.dot(vt_ext, p[:, h * tq:(h + 1) * tq], preferred_element_type=jnp.float32)
            outs.append(o[:HEAD_DIM] / o[HEAD_DIM:HEAD_DIM + 1])
        o_ref[0, blk * tq:(blk + 1) * tq, :] = jnp.concatenate(outs, axis=0).T


def _na_attention(q, k, vt, rp):
    B, S, W = q.shape
    rows = S // GRID_W
    tq = NA_QROWS * GRID_W
    return pl.pallas_call(
        functools.partial(_na_kernel, rows=rows),
        out_shape=jax.ShapeDtypeStruct((B, S, W), jnp.float32),
        grid=(B, S // NA_TQ),
        in_specs=[pl.BlockSpec((1, NA_TQ, W), lambda b, j: (b, j, 0)),
                  pl.BlockSpec((1, S, W), lambda b, j: (b, 0, 0), pipeline_mode=pl.Buffered(1)),
                  pl.BlockSpec((1, S // LANES, W, LANES), lambda b, j: (b, 0, 0, 0),
                               pipeline_mode=pl.Buffered(1)),
                  _const_spec(rp.shape)],
        out_specs=pl.BlockSpec((1, NA_TQ, W), lambda b, j: (b, j, 0)),
        scratch_shapes=[pltpu.VMEM((3, NA_BAND * GRID_W, NA_HEADS * tq), jnp.float32)],
        compiler_params=_cparams("arbitrary", "arbitrary"),
        name="na_attn",
    )(q, k, vt, rp)


def _na_rpb_rows(rpb):
    n_col = rpb.shape[-1]
    p = jnp.pad(rpb.astype(jnp.float32)[:, :, ::-1], ((0, 0), (1, 1), (0, GRID_W - n_col)))
    return jnp.concatenate([p[:, 1:], p[:, :-1]], axis=-1)


def _sw_kernel(sink_ref, t5_ref, q_ref, k_ref, vt_ref, bucket_ref, o_ref, bias_ref, *, seq):
    i = pl.program_id(1)
    nblk_total = seq // SW_BLOCK

    @pl.when((pl.program_id(0) == 0) & (i == 0))
    def _():
        bucket = bucket_ref[...]
        for h in range(SW_HEADS):
            acc = jnp.full(bucket.shape, NEG_INF, jnp.float32)
            for b in range(T5_BUCKETS):
                acc = jnp.where(bucket == b, t5_ref[b * SW_HEADS + h] * LOG2E, acc)
            bias_ref[:, h * SW_BLOCK:(h + 1) * SW_BLOCK] = acc

    nblk = SW_TQ // SW_BLOCK
    rep = SW_HEADS // SW_KV_HEADS
    bf = jnp.bfloat16
    width = SW_HEADS * SW_BLOCK
    lane_group = lax.broadcasted_iota(jnp.int32, (SW_BLOCK, LANES), 1) // HEAD_DIM
    row_group = lax.broadcasted_iota(jnp.int32, (SW_BLOCK, LANES), 0) // HEAD_DIM
    key_row = lax.broadcasted_iota(jnp.int32, (3 * SW_BLOCK, width), 0)
    sink = jnp.concatenate([jnp.full((1, SW_BLOCK), sink_ref[h] * LOG2E, jnp.float32)
                            for h in range(SW_HEADS)], axis=1)
    ones = jnp.ones((16, 3 * SW_BLOCK), bf)

    def neighbours(blk):
        n = i * nblk + blk
        return jnp.maximum(n - 1, 0), n, jnp.minimum(n + 1, nblk_total - 1)

    def scores(blk):
        rows = slice(blk * SW_BLOCK, (blk + 1) * SW_BLOCK)
        stack = []
        for g in range(SW_KV_HEADS):
            for r in range(rep):
                qcol = q_ref[0, rows, r * LANES:(r + 1) * LANES]
                stack.append(jnp.where(lane_group == g, qcol, jnp.zeros_like(qcol)))
        kw = jnp.concatenate([k_ref[0, pl.ds(pl.multiple_of(nb * SW_BLOCK, SW_BLOCK), SW_BLOCK), :]
                              for nb in neighbours(blk)], axis=0)
        return lax.dot_general(kw, jnp.concatenate(stack, axis=0), (((1,), (1,)), ((), ())),
                               preferred_element_type=jnp.float32)

    pending = [scores(blk) for blk in range(min(SW_LOOKAHEAD, nblk))]
    for blk in range(nblk):
        s = pending.pop(0)
        if blk + SW_LOOKAHEAD < nblk:
            pending.append(scores(blk + SW_LOOKAHEAD))
        s = s + bias_ref[...]
        left, n, right = neighbours(blk)
        if blk in (0, nblk - 1):
            outside = (((n == 0) & (key_row < SW_BLOCK))
                       | ((n == nblk_total - 1) & (key_row >= 2 * SW_BLOCK)))
            s = jnp.where(outside, NEG_INF, s)
        m = jnp.maximum(jnp.max(s, axis=0, keepdims=True), sink)
        p = jnp.exp2(s - m).astype(bf)
        vt_ext = jnp.concatenate(
            [jnp.concatenate([vt_ref[0, left], vt_ref[0, n], vt_ref[0, right]], axis=1), ones], axis=0)
        o = jnp.dot(vt_ext, p, preferred_element_type=jnp.float32)
        res = o[:LANES] / (o[LANES:LANES + 1] + jnp.exp2(sink - m))
        for r in range(rep):
            yt = jnp.where(row_group == 0, res[:, r * SW_BLOCK:(r + 1) * SW_BLOCK],
                           res[:, (rep + r) * SW_BLOCK:(rep + r + 1) * SW_BLOCK])
            o_ref[0, blk * SW_BLOCK:(blk + 1) * SW_BLOCK, r * LANES:(r + 1) * LANES] = yt.T


def _sw_attention(q, k, vt, bucket, t5_flat, sink):
    B, S, _ = q.shape
    nb = S // SW_BLOCK
    return pl.pallas_call(
        functools.partial(_sw_kernel, seq=S),
        out_shape=jax.ShapeDtypeStruct((B, S, SW_QW), jnp.float32),
        grid_spec=pltpu.PrefetchScalarGridSpec(
            num_scalar_prefetch=2,
            grid=(B, S // SW_TQ),
            in_specs=[pl.BlockSpec((1, SW_TQ, SW_QW), lambda b, i, *_: (b, i, 0)),
                      pl.BlockSpec((1, S, SW_KW), lambda b, i, *_: (b, 0, 0),
                                   pipeline_mode=pl.Buffered(1)),
                      pl.BlockSpec((1, nb, SW_KW, SW_BLOCK), lambda b, i, *_: (b, 0, 0, 0),
                                   pipeline_mode=pl.Buffered(1)),
                      pl.BlockSpec(bucket.shape, lambda b, i, *_: (0, 0),
                                   pipeline_mode=pl.Buffered(1))],
            out_specs=pl.BlockSpec((1, SW_TQ, SW_QW), lambda b, i, *_: (b, i, 0)),
            scratch_shapes=[pltpu.VMEM((3 * SW_BLOCK, SW_HEADS * SW_BLOCK), jnp.float32)],
        ),
        compiler_params=_cparams("arbitrary", "arbitrary"),
        name="sw_attn",
    )(sink, t5_flat, q, k, vt, bucket)


def _t5_bucket(rel):
    nb = T5_BUCKETS // 2
    ret = (rel > 0).astype(jnp.int32) * nb
    n = jnp.abs(rel)
    max_exact = nb // 2
    nf = jnp.maximum(n, max_exact).astype(jnp.float32)
    large = max_exact + (jnp.log(nf / max_exact) / math.log(T5_MAX_DIST / max_exact)
                         * (nb - max_exact)).astype(jnp.int32)
    large = jnp.minimum(large, nb - 1)
    return ret + jnp.where(n < max_exact, n, large)


def _sw_bucket_tile():
    qpos = jnp.arange(SW_BLOCK)
    kpos = jnp.arange(3 * SW_BLOCK) - SW_BLOCK
    rel = kpos[:, None] - qpos[None, :]
    return jnp.where(jnp.abs(rel) <= SW_RADIUS, _t5_bucket(rel), -1).astype(jnp.int32)


def _ax_kernel(q_ref, qnorm_ref, kmax_ref, k_ref, vt_ref, inv_ref, o_ref, qpad_ref, m_ref, l_ref,
               acc_ref, s0_ref, *, nkt, bounded):
    tq = q_ref.shape[-1]
    tk = AX_TK
    rep = AX_HEADS // AX_KV_HEADS
    bf = jnp.bfloat16
    inv = inv_ref[...]
    for h in range(AX_HEADS):
        g = h // rep
        head = slice(h * HEAD_DIM, (h + 1) * HEAD_DIM)
        if bounded:
            row = lax.broadcasted_iota(jnp.int32, (32, tq), 0)
            shift = qnorm_ref[0, h:h + 1, :] * jnp.sqrt(kmax_ref[0, :, g * HEAD_DIM:g * HEAD_DIM + 1])
            extra = jnp.where(row == 0, -shift, 0.0).astype(F8)
            qpad_ref[h] = jnp.concatenate(
                [q_ref[0, 0, head, :], q_ref[0, 1, head, :], q_ref[0, 2, head, :], extra,
                 jnp.zeros((32, tq), F8)], axis=0)
        else:
            qpad_ref[h] = jnp.concatenate(
                [q_ref[0, head, :], jnp.zeros((LANES - HEAD_DIM, tq), bf)], axis=0)
    if not bounded:
        m_ref[...] = jnp.full(m_ref.shape, -jnp.inf, jnp.float32)
        l_ref[...] = jnp.zeros(l_ref.shape, jnp.float32)
    acc_ref[...] = jnp.zeros(acc_ref.shape, jnp.float32)

    def score(kt, h):
        ks = pl.multiple_of(kt * tk, tk)
        return jnp.dot(k_ref[0, h // rep, pl.ds(ks, tk), :], qpad_ref[h],
                       preferred_element_type=jnp.float32)

    for h in range(AX_LOOKAHEAD):
        s0_ref[h] = score(0, h)

    def body(kt, carry):
        pending = [s0_ref[h] for h in range(AX_LOOKAHEAD)]
        for h in range(AX_HEADS):
            g = h // rep
            s = pending.pop(0)
            ahead = h + AX_LOOKAHEAD
            if ahead < AX_HEADS:
                pending.append(score(kt, ahead))
            else:
                s0_ref[ahead - AX_HEADS] = score(jnp.minimum(kt + 1, nkt - 1), ahead - AX_HEADS)
            vt = vt_ref[0, kt, g * HEAD_DIM:(g + 1) * HEAD_DIM, :]
            s = s * inv
            if bounded:
                p = jnp.exp2(s)
                vt_ext = jnp.concatenate([vt, jnp.ones((AX_ONES_ROWS, tk), bf)], axis=0)
                acc_ref[h] += jnp.dot(vt_ext, p.astype(bf), preferred_element_type=jnp.float32)
            else:
                m_old = m_ref[h]
                m_new = jnp.maximum(m_old, jnp.max(s, axis=0, keepdims=True))
                alpha = jnp.exp2(m_old - m_new)
                p = jnp.exp2(s - m_new)
                l_ref[h] = alpha * l_ref[h] + jnp.sum(p.reshape(tk // 8, 8, tq), axis=0)
                m_ref[h] = m_new
                pv = jnp.dot(vt, p.astype(bf), preferred_element_type=jnp.float32)
                acc_ref[h, :HEAD_DIM] = alpha * acc_ref[h, :HEAD_DIM] + pv
        return carry

    lax.fori_loop(0, nkt, body, 0, unroll=AX_UNROLL)
    if bounded:
        denom = [acc_ref[h, HEAD_DIM:HEAD_DIM + 1] for h in range(AX_HEADS)]
    else:
        denom = [jnp.sum(l_ref[h], axis=0, keepdims=True) for h in range(AX_HEADS)]
    out_t = jnp.concatenate([acc_ref[h, :HEAD_DIM] / denom[h]
                             for h in range(AX_HEADS)], axis=0)
    o_ref[0] = out_t.T


def _ax_attention(qt, k, q8, k8, qnorm, kmax, vt, inv, bounded):
    B, W, S = qt.shape
    nkt = vt.shape[1]
    if bounded:
        q, keys = q8, k8
        q_spec = pl.BlockSpec((1, 3, W, AX_TQ), lambda b, i: (b, 0, 0, i))
        qpad = pltpu.VMEM((AX_HEADS, 2 * LANES, AX_TQ), F8)
    else:
        q, keys = qt, k
        q_spec = pl.BlockSpec((1, W, AX_TQ), lambda b, i: (b, 0, i))
        qpad = pltpu.VMEM((AX_HEADS, LANES, AX_TQ), jnp.bfloat16)
    return pl.pallas_call(
        functools.partial(_ax_kernel, nkt=nkt, bounded=bounded),
        out_shape=jax.ShapeDtypeStruct((B, S, W), jnp.float32),
        grid=(B, S // AX_TQ),
        in_specs=[q_spec,
                  pl.BlockSpec((1, 8, AX_TQ), lambda b, i: (b, 0, i)),
                  pl.BlockSpec((1, 1, AX_KW), lambda b, i: (b, 0, 0)),
                  pl.BlockSpec((1,) + keys.shape[1:], lambda b, i: (b, 0, 0, 0),
                               pipeline_mode=pl.Buffered(1)),
                  pl.BlockSpec((1, nkt, AX_KW, AX_TK), lambda b, i: (b, 0, 0, 0),
                               pipeline_mode=pl.Buffered(1)),
                  _const_spec((1, 1))],
        out_specs=pl.BlockSpec((1, AX_TQ, W), lambda b, i: (b, i, 0)),
        scratch_shapes=[qpad,
                        pltpu.VMEM((AX_HEADS, 1, AX_TQ), jnp.float32),
                        pltpu.VMEM((AX_HEADS, 8, AX_TQ), jnp.float32),
                        pltpu.VMEM((AX_HEADS, HEAD_DIM + AX_ONES_ROWS, AX_TQ), jnp.float32),
                        pltpu.VMEM((AX_LOOKAHEAD, AX_TK, AX_TQ), jnp.float32)],
        compiler_params=_cparams("arbitrary", "arbitrary"),
        name="ax_attn_bounded" if bounded else "ax_attn_online",
    )(q, qnorm, kmax, keys, vt, inv)


def _ax_dispatch(qt, k, q8, k8, qnorm, kmax, vt, inv, gain_max):
    rep = AX_HEADS // AX_KV_HEADS
    knorm = jnp.sqrt(kmax[:, 0, ::HEAD_DIM])
    shift_max = jnp.max(jnp.max(qnorm[:, :AX_HEADS], axis=-1) * jnp.repeat(knorm, rep, axis=1))
    entry_max = HEAD_DIM ** 0.5 * gain_max * jnp.array([FP8_Q_HI * Q_SCALE_EXP2, 1.0])
    safe = ((shift_max * inv[0, 0] <= AX_MAX_SHIFT) & (shift_max <= F8_MAX / 2)
            & jnp.all(entry_max <= F8_MAX / 2))
    return lax.cond(safe,
                    functools.partial(_ax_attention, bounded=True),
                    functools.partial(_ax_attention, bounded=False),
                    qt, k, q8, k8, qnorm, kmax, vt, inv)


def _post_kernel(x_ref, ya_ref, yb_ref, yc_ref, gg_ref, wo_ref, gta_ref, gf_ref, scf_ref, shf_ref,
                 gtf_ref, wgu_ref, wd_ref, gfin_ref, o_ref, *, final):
    bf = jnp.bfloat16
    gg = gg_ref[...]
    y = jnp.concatenate([
        _rms(ya_ref[0], gg[:, 0:NA_W]).astype(bf),
        _rms(yb_ref[0], gg[:, NA_W:NA_W + SW_QW]).astype(bf),
        _rms(yc_ref[0], gg[:, NA_W + SW_QW:]).astype(bf)], axis=1)
    x1 = x_ref[0] + gta_ref[0] * jnp.dot(y, wo_ref[...], preferred_element_type=jnp.float32)

    h = (_rms(x1, gf_ref[...]) * (1.0 + scf_ref[0]) + shf_ref[0]).astype(bf)
    acc = jnp.zeros(x1.shape, jnp.float32)
    for c0 in range(0, FFN_HIDDEN, FFN_CHUNK):
        c1 = min(c0 + FFN_CHUNK, FFN_HIDDEN)
        gate = jnp.dot(h, wgu_ref[:, c0:c1], preferred_element_type=jnp.float32)
        up = jnp.dot(h, wgu_ref[:, FFN_HIDDEN + c0:FFN_HIDDEN + c1],
                     preferred_element_type=jnp.float32)
        act = (gate * (1.0 / (1.0 + jnp.exp(-gate))) * up).astype(bf)
        acc = acc + jnp.dot(act, wd_ref[c0:c1, :], preferred_element_type=jnp.float32)
    x2 = x1 + gtf_ref[0] * acc
    if final:
        x2 = _rms(x2, gfin_ref[...])
    o_ref[0] = x2


def _post(x, ya, yb, yc, gg, wo, gta, gf, scf, shf, gtf, wgu, wd, gfin, layer, final):
    B, S, D = x.shape
    tm = TOK_TILE
    tok = lambda width: pl.BlockSpec((1, tm, width), lambda b, i: (b, i, 0))
    vec = pl.BlockSpec((1, 1, D), lambda b, i: (b, 0, 0))
    return pl.pallas_call(
        functools.partial(_post_kernel, final=final),
        out_shape=jax.ShapeDtypeStruct((B, S, D), jnp.float32),
        grid=(B, S // tm),
        in_specs=[tok(D), tok(NA_W), tok(SW_QW), tok(AX_QW), _const_spec((1, D)),
                  _layer_spec(wo, layer), vec, _const_spec((1, D)), vec, vec, vec,
                  _layer_spec(wgu, layer), _layer_spec(wd, layer), _const_spec((1, D))],
        out_specs=tok(D),
        compiler_params=_cparams("arbitrary", "arbitrary"),
        name="out_proj_ffn",
    )(x, ya, yb, yc, gg, wo, gta, gf, scf, shf, gtf, wgu, wd, gfin)


def _rope_tables(S):
    t = jnp.arange(S)
    row = (t // GRID_W).astype(jnp.float32)
    col = (t % GRID_W).astype(jnp.float32)
    axis_dim = HEAD_DIM // 2
    freqs = ROPE_THETA ** (-jnp.arange(0, axis_dim, 2, dtype=jnp.float32) / axis_dim)
    ang = jnp.stack([row[:, None] * freqs, col[:, None] * freqs], axis=1)
    cos = jnp.cos(ang)
    sin = jnp.sin(ang)
    cos_h = jnp.concatenate([cos, cos], axis=-1).reshape(S, HEAD_DIM)
    sin_h = jnp.concatenate([-sin, sin], axis=-1).reshape(S, HEAD_DIM)
    return jnp.tile(cos_h, (1, LANES // HEAD_DIM)), jnp.tile(sin_h, (1, LANES // HEAD_DIM))


def _pow2_normaliser(g):
    r = jnp.sqrt(jnp.mean(g.astype(jnp.float32) ** 2))
    return jnp.exp2(-jnp.round(jnp.log2(jnp.maximum(r, 1e-30))))


def _block_diag_ones(width):
    idx = np.arange(width) // HEAD_DIM
    return jnp.asarray(idx[:, None] == idx[None, :], dtype=jnp.bfloat16)


def _sw_relabel(t, start, axis):
    rep = SW_HEADS // SW_KV_HEADS
    seg = lax.slice_in_dim(t, start, start + SW_QW, axis=axis)
    shp = seg.shape
    seg = seg.reshape(shp[:axis] + (SW_KV_HEADS, rep, HEAD_DIM) + shp[axis + 1:])
    seg = jnp.swapaxes(seg, axis, axis + 1).reshape(shp)
    return jnp.concatenate([lax.slice_in_dim(t, 0, start, axis=axis), seg,
                            lax.slice_in_dim(t, start + SW_QW, t.shape[axis], axis=axis)], axis=axis)


def kernel(x, c, w_mod, b_mod, g_attn, w_in, rpb_na, sink_sw, t5_table, gq_ax, gk_ax, g_group,
           w_o, g_ffn, w_gu, w_down, g_final):
    B, S, D = x.shape
    L = w_mod.shape[0]
    bf = jnp.bfloat16

    c_pad = jnp.pad(c, ((0, 8 - B), (0, 0)))
    mod = _modulation(c_pad, w_mod, b_mod)[:, :B]
    mod = mod.reshape(L, B, 6, 1, D)

    cos, sin = _rope_tables(S)
    bd = _block_diag_ones(LANES)
    sw_bucket = _sw_bucket_tile()
    t5_flat = t5_table.astype(jnp.float32).reshape(-1)

    w_in_p = _sw_relabel(w_in, OFF_QB, axis=2).astype(bf)
    w_o_p = _sw_relabel(w_o, NA_W, axis=1).astype(bf)
    g_group_p = _sw_relabel(g_group, NA_W, axis=1)
    w_gu_b = w_gu.astype(bf)
    w_down_b = w_down.astype(bf)

    for l in range(L):
        sh_a, sc_a, gt_a, sh_f, sc_f, gt_f = [mod[l, :, i] for i in range(6)]
        gq = gq_ax[l] * _pow2_normaliser(gq_ax[l])
        gk = gk_ax[l] * _pow2_normaliser(gk_ax[l])
        inv = (1.0 / (_pow2_normaliser(gq_ax[l]) * _pow2_normaliser(gk_ax[l]))).reshape(1, 1)
        gain_max = jnp.stack([jnp.max(jnp.abs(gq)), jnp.max(jnp.abs(gk))])
        qa, ka, va, qb, kb, vb, qct, kc, vct, qnorm, kmax, q8, k8 = _in_proj(
            x, sc_a, sh_a, g_attn[l].reshape(1, D), w_in_p, l, cos, sin,
            jnp.tile(gq, LANES // HEAD_DIM).reshape(1, LANES),
            jnp.tile(gk, LANES // HEAD_DIM).reshape(1, LANES), bd)
        ya = _na_attention(qa, ka, va, _na_rpb_rows(rpb_na[l]))
        yb = _sw_attention(qb, kb, vb, sw_bucket, t5_flat, sink_sw[l])
        yc = _ax_dispatch(qct, kc, q8, k8, qnorm, kmax, vct, inv, gain_max)
        x = _post(x, ya, yb, yc, g_group_p[l].reshape(1, D), w_o_p, gt_a, g_ffn[l].reshape(1, D),
                  sc_f, sh_f, gt_f, w_gu_b, w_down_b, g_final.reshape(1, D), layer=l,
                  final=(l == L - 1))
    return x
```

```python
import functools
import math

import jax
import jax.numpy as jnp
import numpy as np
from jax import lax
from jax.experimental import pallas as pl
from jax.experimental.pallas import tpu as pltpu

D_MODEL = 1024
HEAD_DIM = 64
GRID_W = 64
NA_HEADS = 4
SW_HEADS = 6
SW_KV_HEADS = 2
AX_HEADS = 6
AX_KV_HEADS = 2
NA_WIN_ROWS = 8
NA_WIN_COLS = 16
SW_RADIUS = 128
SW_BLOCK = 128
T5_BUCKETS = 32
T5_MAX_DIST = 128
ROPE_THETA = 10000.0
FFN_HIDDEN = 2816
EPS = 1e-6
NEG_INF = -1e30

NA_W = NA_HEADS * HEAD_DIM
SW_QW = SW_HEADS * HEAD_DIM
SW_KW = SW_KV_HEADS * HEAD_DIM
AX_QW = AX_HEADS * HEAD_DIM
AX_KW = AX_KV_HEADS * HEAD_DIM
IN_WIDTH = 3 * NA_W + SW_QW + 2 * SW_KW + AX_QW + 2 * AX_KW
OFF_QA, OFF_KA, OFF_VA = 0, NA_W, 2 * NA_W
OFF_QB = 3 * NA_W
OFF_KB = OFF_QB + SW_QW
OFF_VB = OFF_KB + SW_KW
OFF_QC = OFF_VB + SW_KW
OFF_KC = OFF_QC + AX_QW
OFF_VC = OFF_KC + AX_KW

LANES = 128
VMEM_LIMIT = 56 * 1024 * 1024

TOK_TILE = 512
IN_TILE = 1024
IN_SUB = 256
AX_TQ = 256
AX_TK = 512
AX_LOOKAHEAD = 2
AX_UNROLL = 8
AX_ONES_ROWS = 16
NA_QROWS = 4
NA_BAND = NA_QROWS + NA_WIN_ROWS
NA_TQ = 2048
NA_LOOKAHEAD = 2
NA_UNROLL = 2
SW_TQ = 2048
SW_LOOKAHEAD = 2
SW_UNROLL = 4
FFN_CHUNK = 512

QK_SCALE = HEAD_DIM ** -0.5
F8 = jnp.float8_e4m3fn
F8_MAX = 448.0
FP8_Q_HI, FP8_K_HI = 2.0, 0.5
FP8_Q_LO, FP8_K_LO = 16.0, 1.0 / 16.0
LOG2E = math.log2(math.e)
Q_SCALE_EXP2 = QK_SCALE * LOG2E
AX_MAX_SHIFT = 60.0


def _cparams(*sem):
    return pltpu.CompilerParams(dimension_semantics=sem, vmem_limit_bytes=VMEM_LIMIT)


def _const_spec(shape):
    n = len(shape)
    return pl.BlockSpec(shape, lambda *_: (0,) * n, pipeline_mode=pl.Buffered(1))


def _layer_spec(stacked, layer):
    n = stacked.ndim - 1
    return pl.BlockSpec((None,) + stacked.shape[1:], lambda *_: (layer,) + (0,) * n,
                        pipeline_mode=pl.Buffered(1))


def _rms(x, g):
    return x * lax.rsqrt(jnp.mean(x * x, axis=-1, keepdims=True) + EPS) * g


def _mod_kernel(c_ref, w_ref, b_ref, o_ref):
    c = c_ref[...]
    cond = c * (1.0 / (1.0 + jnp.exp(-c)))
    o_ref[0] = jnp.dot(cond, w_ref[0], preferred_element_type=jnp.float32,
                       precision=lax.Precision.HIGHEST) + b_ref[0]


def _modulation(c_pad, w_mod, b_mod):
    L, D, N = w_mod.shape
    tn = 1536
    return pl.pallas_call(
        _mod_kernel,
        out_shape=jax.ShapeDtypeStruct((L, c_pad.shape[0], N), jnp.float32),
        grid=(L, N // tn),
        in_specs=[pl.BlockSpec(c_pad.shape, lambda l, j: (0, 0)),
                  pl.BlockSpec((1, D, tn), lambda l, j: (l, 0, j)),
                  pl.BlockSpec((1, 1, tn), lambda l, j: (l, 0, j))],
        out_specs=pl.BlockSpec((1, c_pad.shape[0], tn), lambda l, j: (l, 0, j)),
        compiler_params=_cparams("arbitrary", "arbitrary"),
        name="adaln_mod",
    )(c_pad, w_mod, b_mod.reshape(L, 1, N))


def _head_sumsq(t, bd):
    t2 = (t * t).astype(jnp.bfloat16)
    return jnp.concatenate(
        [jnp.dot(t2[:, c:c + LANES], bd, preferred_element_type=jnp.float32)
         for c in range(0, t.shape[1], LANES)], axis=1)


def _rope_chunk(t, cos, sin_signed, first_half):
    swapped = jnp.where(first_half, pltpu.roll(t, LANES - 16, 1), pltpu.roll(t, 16, 1))
    return t * cos + swapped * sin_signed


def _in_kernel(x_ref, sc_ref, sh_ref, g_ref, w_ref, cos_ref, sin_ref, gq_ref, gk_ref,
               bd_ref,
               qa_ref, ka_ref, vat_ref, qb_ref, kb_ref, vbt_ref, qct_ref, kc_ref, vct_ref,
               qnorm_ref, kmax_ref, q8_ref, k8_ref):
    bf = jnp.bfloat16
    n_sub = x_ref.shape[1] // IN_SUB
    lane = lax.broadcasted_iota(jnp.int32, (IN_SUB, LANES), 1)
    first_half = (lane % 32) < 16

    def normed(j):
        x = x_ref[0, j * IN_SUB:(j + 1) * IN_SUB, :]
        return (_rms(x, g_ref[...]) * (1.0 + sc_ref[0]) + sh_ref[0]).astype(bf)

    def project(h):
        return jnp.dot(h, w_ref[...], preferred_element_type=jnp.float32)

    def head_stats(proj):
        return (_head_sumsq(proj[:, OFF_QC:OFF_QC + AX_QW], bd_ref[...]),
                _head_sumsq(proj[:, OFF_KC:OFF_KC + AX_KW], bd_ref[...]))

    def finish(j, proj, stats):
        rows = slice(j * IN_SUB, (j + 1) * IN_SUB)
        qa_ref[0, rows] = (proj[:, OFF_QA:OFF_QA + NA_W] * Q_SCALE_EXP2).astype(bf)
        ka_ref[0, rows] = proj[:, OFF_KA:OFF_KA + NA_W].astype(bf)
        qb_ref[0, rows] = (proj[:, OFF_QB:OFF_QB + SW_QW] * Q_SCALE_EXP2).astype(bf)
        kb_ref[0, rows] = proj[:, OFF_KB:OFF_KB + SW_KW].astype(bf)
        per_sub = IN_SUB // LANES
        for t in range(per_sub):
            blk = slice(t * LANES, (t + 1) * LANES)
            vat_ref[0, j * per_sub + t] = proj[blk, OFF_VA:OFF_VA + NA_W].T.astype(bf)
            vbt_ref[0, j * per_sub + t] = proj[blk, OFF_VB:OFF_VB + SW_KW].T.astype(bf)

        cos = cos_ref[rows, :]
        sin = sin_ref[rows, :]
        qss, kss = stats
        qn = proj[:, OFF_QC:OFF_QC + AX_QW] * lax.rsqrt(qss * (1.0 / HEAD_DIM) + EPS)
        chunks = []
        for c in range(AX_QW // LANES):
            t = qn[:, c * LANES:(c + 1) * LANES] * gq_ref[...]
            chunks.append(_rope_chunk(t, cos, sin, first_half) * Q_SCALE_EXP2)
        f32 = jnp.float32
        qt = jnp.concatenate(chunks, axis=1).T
        qct_ref[0, :, rows] = qt.astype(bf)
        a1 = (qt * FP8_Q_HI).astype(F8).astype(f32)
        a2 = ((qt - a1 * (1.0 / FP8_Q_HI)) * FP8_Q_LO).astype(F8).astype(f32)
        a3 = qt.astype(F8).astype(f32)
        q8_ref[0, 0, :, rows] = a1.astype(F8)
        q8_ref[0, 1, :, rows] = a2.astype(F8)
        q8_ref[0, 2, :, rows] = a3.astype(F8)
        qsq = a1 * a1 + a2 * a2 + a3 * a3
        norms = [jnp.sqrt(jnp.sum(qsq[hh * HEAD_DIM:(hh + 1) * HEAD_DIM], axis=0, keepdims=True))
                 for hh in range(AX_HEADS)]
        qnorm_ref[0, :, rows] = jnp.concatenate(
            norms + [jnp.zeros_like(norms[0])] * (8 - AX_HEADS), axis=0)

        kn = (proj[:, OFF_KC:OFF_KC + AX_KW] * lax.rsqrt(kss * (1.0 / HEAD_DIM) + EPS)
              * gk_ref[...])
        kr = _rope_chunk(kn, cos, sin, first_half)
        kb = kr.astype(bf)
        zero_bf = jnp.zeros_like(kb)
        kc_ref[0, 0, rows] = jnp.where(lane < HEAD_DIM, kb, zero_bf)
        kc_ref[0, 1, rows] = jnp.where(lane < HEAD_DIM, pltpu.roll(kr, HEAD_DIM, 1).astype(bf), zero_bf)
        x1 = kr * FP8_K_HI
        x2 = kr * FP8_K_LO
        x3 = kr - x1.astype(F8).astype(f32) * (1.0 / FP8_K_HI)
        ones_f = jnp.where(lane == HEAD_DIM, 1.0, 0.0)
        norms2 = []
        for g in range(AX_KV_HEADS):
            if g == 0:
                c0 = jnp.where(lane < HEAD_DIM, x1, pltpu.roll(x2, HEAD_DIM, 1))
                c1 = jnp.where(lane < HEAD_DIM, x3, ones_f)
            else:
                c0 = jnp.where(lane < HEAD_DIM, pltpu.roll(x1, HEAD_DIM, 1), x2)
                c1 = jnp.where(lane < HEAD_DIM, pltpu.roll(x3, HEAD_DIM, 1), ones_f)
            c0 = c0.astype(F8)
            c1 = c1.astype(F8)
            k8_ref[0, g, rows] = jnp.concatenate([c0, c1], axis=1)
            c0f = c0.astype(f32)
            c1f = jnp.where(lane < HEAD_DIM, c1.astype(f32), 0.0)
            norms2.append(jnp.max(jnp.sum(c0f * c0f + c1f * c1f, axis=1, keepdims=True),
                                  axis=0, keepdims=True))
        first_tk = (j * IN_SUB) // AX_TK
        off = (j * IN_SUB) % AX_TK
        vct_ref[0, first_tk, :, off:off + IN_SUB] = proj[:, OFF_VC:OFF_VC + AX_KW].T.astype(bf)
        return jnp.where(lane[:1] < HEAD_DIM, norms2[0], norms2[1])

    proj = project(normed(0))
    tile_max = None
    for j in range(n_sub):
        if j + 1 < n_sub:
            h_next = normed(j + 1)
        stats = head_stats(proj)
        if j + 1 < n_sub:
            proj_next = project(h_next)
        sub_max = finish(j, proj, stats)
        tile_max = sub_max if tile_max is None else jnp.maximum(tile_max, sub_max)
        if j + 1 < n_sub:
            proj = proj_next

    first = pl.program_id(1) == 0

    @pl.when(first)
    def _():
        kmax_ref[0] = tile_max

    @pl.when(jnp.logical_not(first))
    def _():
        kmax_ref[0] = jnp.maximum(kmax_ref[0], tile_max)


def _in_proj(x, sc, sh, g, w, layer, cos, sin, gq, gk, bd):
    B, S, D = x.shape
    tm = IN_TILE
    bf = jnp.bfloat16
    tok = lambda width: pl.BlockSpec((1, tm, width), lambda b, i: (b, i, 0))
    vec = pl.BlockSpec((1, 1, D), lambda b, i: (b, 0, 0))
    out_shape = (
        jax.ShapeDtypeStruct((B, S, NA_W), bf), jax.ShapeDtypeStruct((B, S, NA_W), bf),
        jax.ShapeDtypeStruct((B, S // LANES, NA_W, LANES), bf),
        jax.ShapeDtypeStruct((B, S, SW_QW), bf), jax.ShapeDtypeStruct((B, S, SW_KW), bf),
        jax.ShapeDtypeStruct((B, S // SW_BLOCK, SW_KW, SW_BLOCK), bf),
        jax.ShapeDtypeStruct((B, AX_QW, S), bf),
        jax.ShapeDtypeStruct((B, AX_KV_HEADS, S, LANES), bf),
        jax.ShapeDtypeStruct((B, S // AX_TK, AX_KW, AX_TK), bf),
        jax.ShapeDtypeStruct((B, 8, S), jnp.float32),
        jax.ShapeDtypeStruct((B, 1, AX_KW), jnp.float32),
        jax.ShapeDtypeStruct((B, 3, AX_QW, S), F8),
        jax.ShapeDtypeStruct((B, AX_KV_HEADS, S, 2 * LANES), F8),
    )
    out_specs = (
        tok(NA_W), tok(NA_W),
        pl.BlockSpec((1, tm // LANES, NA_W, LANES), lambda b, i: (b, i, 0, 0)),
        tok(SW_QW), tok(SW_KW),
        pl.BlockSpec((1, tm // SW_BLOCK, SW_KW, SW_BLOCK), lambda b, i: (b, i, 0, 0)),
        pl.BlockSpec((1, AX_QW, tm), lambda b, i: (b, 0, i)),
        pl.BlockSpec((1, AX_KV_HEADS, tm, LANES), lambda b, i: (b, 0, i, 0)),
        pl.BlockSpec((1, tm // AX_TK, AX_KW, AX_TK), lambda b, i: (b, i, 0, 0)),
        pl.BlockSpec((1, 8, tm), lambda b, i: (b, 0, i)),
        pl.BlockSpec((1, 1, AX_KW), lambda b, i: (b, 0, 0)),
        pl.BlockSpec((1, 3, AX_QW, tm), lambda b, i: (b, 0, 0, i)),
        pl.BlockSpec((1, AX_KV_HEADS, tm, 2 * LANES), lambda b, i: (b, 0, i, 0)),
    )
    return pl.pallas_call(
        _in_kernel,
        out_shape=out_shape,
        grid=(B, S // tm),
        in_specs=[tok(D), vec, vec, _const_spec((1, D)), _layer_spec(w, layer),
                  pl.BlockSpec((tm, LANES), lambda b, i: (i, 0)),
                  pl.BlockSpec((tm, LANES), lambda b, i: (i, 0)),
                  _const_spec((1, LANES)), _const_spec((1, LANES)),
                  _const_spec(bd.shape)],
        out_specs=out_specs,
        compiler_params=_cparams("arbitrary", "arbitrary"),
        name="in_proj",
    )(x, sc, sh, g, w, cos, sin, gq, gk, bd)


def _na_build_bias(rp_ref, bias_ref, rows):
    shape = (GRID_W, LANES)
    kc = lax.broadcasted_iota(jnp.int32, shape, 0)
    lane = lax.broadcasted_iota(jnp.int32, shape, 1)
    c = lane % GRID_W
    cs = jnp.clip(c - NA_WIN_COLS // 2, 0, GRID_W - NA_WIN_COLS)
    col_ok = (kc >= cs) & (kc < cs + NA_WIN_COLS)
    left = lane < GRID_W
    neg = jnp.full(shape, NEG_INF, jnp.float32)
    tq = NA_QROWS * GRID_W
    for h in range(NA_HEADS):
        pair = []
        for a in range(2 * NA_WIN_ROWS):
            x = jnp.broadcast_to(rp_ref[h, a:a + 1, :] * LOG2E, shape)
            t = pltpu.roll(x, LANES - (NA_WIN_COLS - 1), 1, stride=1, stride_axis=0)
            pair.append(jnp.where(col_ok, t, neg))
        for variant, qr0 in enumerate((0, 2 * NA_QROWS, rows - NA_QROWS)):
            bs = int(np.clip(qr0 - NA_WIN_ROWS // 2, 0, rows - NA_BAND))
            for kj in range(NA_BAND):
                kr = bs + kj
                for u in range(NA_QROWS // 2):
                    r0 = qr0 + 2 * u
                    ok = [int(np.clip(r - NA_WIN_ROWS // 2, 0, rows - NA_WIN_ROWS)) <= kr
                          < int(np.clip(r - NA_WIN_ROWS // 2, 0, rows - NA_WIN_ROWS)) + NA_WIN_ROWS
                          for r in (r0, r0 + 1)]
                    a0 = kr - r0 + NA_WIN_ROWS - 1
                    if ok[0] and ok[1]:
                        tile = pair[a0]
                    elif ok[0]:
                        tile = jnp.where(left, pair[a0], neg)
                    elif ok[1]:
                        tile = jnp.where(left, neg, pair[a0])
                    else:
                        tile = neg
                    bias_ref[variant, kj * GRID_W:(kj + 1) * GRID_W,
                             h * tq + u * LANES:h * tq + (u + 1) * LANES] = tile


def _na_kernel(q_ref, k_ref, vt_ref, rp_ref, o_ref, bias_ref, s_ref, *, rows):
    i = pl.program_id(1)
    tq = NA_QROWS * GRID_W
    nblk = NA_TQ // tq
    nblk_total = rows // NA_QROWS
    bf = jnp.bfloat16

    @pl.when((pl.program_id(0) == 0) & (i == 0))
    def _():
        _na_build_bias(rp_ref, bias_ref, rows)

    head_of_lane = lax.broadcasted_iota(jnp.int32, (tq, NA_W), 1) // HEAD_DIM
    ones = jnp.ones((16, NA_BAND * GRID_W), bf)

    def band_row(blk):
        return jnp.clip((i * nblk + blk) * NA_QROWS - NA_WIN_ROWS // 2, 0, rows - NA_BAND)

    def scores(blk):
        q = q_ref[0, pl.ds(pl.multiple_of(blk * tq, tq), tq), :]
        stack = jnp.concatenate([jnp.where(head_of_lane == h, q, jnp.zeros_like(q))
                                 for h in range(NA_HEADS)], axis=0)
        start = pl.multiple_of(band_row(blk) * GRID_W, NA_QROWS * GRID_W)
        kb = k_ref[0, pl.ds(start, NA_BAND * GRID_W), :]
        return lax.dot_general(kb, stack, (((1,), (1,)), ((), ())),
                               preferred_element_type=jnp.float32)

    for t in range(NA_LOOKAHEAD):
        s_ref[t] = scores(t)

    def body(trip, carry):
        for t in range(NA_LOOKAHEAD):
            blk = trip * NA_LOOKAHEAD + t
            s = s_ref[t]
            s_ref[t] = scores(jnp.minimum(blk + NA_LOOKAHEAD, nblk - 1))
            n = i * nblk + blk
            variant = jnp.where(n == 0, 0, jnp.where(n == nblk_total - 1, 2, 1))
            s = s + bias_ref[variant]
            m = jnp.max(s, axis=0, keepdims=True)
            p = jnp.exp2(s - m).astype(bf)
            first = band_row(blk) * GRID_W // LANES
            vt = jnp.concatenate([vt_ref[0, first + u] for u in range(NA_BAND * GRID_W // LANES)],
                                 axis=1)
            outs = []
            for h in range(NA_HEADS):
                vt_ext = jnp.concatenate([vt[h * HEAD_DIM:(h + 1) * HEAD_DIM], ones], axis=0)
                o = jnp.dot(vt_ext, p[:, h * tq:(h + 1) * tq], preferred_element_type=jnp.float32)
                outs.append(o[:HEAD_DIM] / o[HEAD_DIM:HEAD_DIM + 1])
            o_ref[0, pl.ds(pl.multiple_of(blk * tq, tq), tq), :] = jnp.concatenate(outs, axis=0).T
        return carry

    lax.fori_loop(0, nblk // NA_LOOKAHEAD, body, 0, unroll=NA_UNROLL)


def _na_attention(q, k, vt, rp):
    B, S, W = q.shape
    rows = S // GRID_W
    tq = NA_QROWS * GRID_W
    return pl.pallas_call(
        functools.partial(_na_kernel, rows=rows),
        out_shape=jax.ShapeDtypeStruct((B, S, W), jnp.float32),
        grid=(B, S // NA_TQ),
        in_specs=[pl.BlockSpec((1, NA_TQ, W), lambda b, j: (b, j, 0)),
                  pl.BlockSpec((1, S, W), lambda b, j: (b, 0, 0), pipeline_mode=pl.Buffered(1)),
                  pl.BlockSpec((1, S // LANES, W, LANES), lambda b, j: (b, 0, 0, 0),
                               pipeline_mode=pl.Buffered(1)),
                  _const_spec(rp.shape)],
        out_specs=pl.BlockSpec((1, NA_TQ, W), lambda b, j: (b, j, 0)),
        scratch_shapes=[pltpu.VMEM((3, NA_BAND * GRID_W, NA_HEADS * tq), jnp.float32),
                        pltpu.VMEM((NA_LOOKAHEAD, NA_BAND * GRID_W, NA_HEADS * tq), jnp.float32)],
        compiler_params=_cparams("arbitrary", "arbitrary"),
        name="na_attn",
    )(q, k, vt, rp)


def _na_rpb_rows(rpb):
    n_col = rpb.shape[-1]
    p = jnp.pad(rpb.astype(jnp.float32)[:, :, ::-1], ((0, 0), (1, 1), (0, GRID_W - n_col)))
    return jnp.concatenate([p[:, 1:], p[:, :-1]], axis=-1)


def _sw_kernel(sink_ref, t5_ref, q_ref, k_ref, vt_ref, bucket_ref, o_ref, bias_ref, s_ref, *, seq):
    i = pl.program_id(1)
    nblk_total = seq // SW_BLOCK

    @pl.when((pl.program_id(0) == 0) & (i == 0))
    def _():
        bucket = bucket_ref[...]
        key_row = lax.broadcasted_iota(jnp.int32, bucket.shape, 0)
        for h in range(SW_HEADS):
            acc = jnp.full(bucket.shape, NEG_INF, jnp.float32)
            for b in range(T5_BUCKETS):
                acc = jnp.where(bucket == b, t5_ref[b * SW_HEADS + h] * LOG2E, acc)
            cols = slice(h * SW_BLOCK, (h + 1) * SW_BLOCK)
            bias_ref[0, :, cols] = acc
            bias_ref[1, :, cols] = jnp.where(key_row < SW_BLOCK, NEG_INF, acc)
            bias_ref[2, :, cols] = jnp.where(key_row >= 2 * SW_BLOCK, NEG_INF, acc)

    nblk = SW_TQ // SW_BLOCK
    rep = SW_HEADS // SW_KV_HEADS
    bf = jnp.bfloat16
    lane_group = lax.broadcasted_iota(jnp.int32, (SW_BLOCK, LANES), 1) // HEAD_DIM
    row_group = lax.broadcasted_iota(jnp.int32, (SW_BLOCK, LANES), 0) // HEAD_DIM
    sink = jnp.concatenate([jnp.full((1, SW_BLOCK), sink_ref[h] * LOG2E, jnp.float32)
                            for h in range(SW_HEADS)], axis=1)
    ones = jnp.ones((16, 3 * SW_BLOCK), bf)

    def neighbours(blk):
        n = i * nblk + blk
        return jnp.maximum(n - 1, 0), n, jnp.minimum(n + 1, nblk_total - 1)

    def scores(blk):
        rows = pl.ds(pl.multiple_of(blk * SW_BLOCK, SW_BLOCK), SW_BLOCK)
        stack = []
        for g in range(SW_KV_HEADS):
            for r in range(rep):
                qcol = q_ref[0, rows, r * LANES:(r + 1) * LANES]
                stack.append(jnp.where(lane_group == g, qcol, jnp.zeros_like(qcol)))
        kw = jnp.concatenate([k_ref[0, pl.ds(pl.multiple_of(nb * SW_BLOCK, SW_BLOCK), SW_BLOCK), :]
                              for nb in neighbours(blk)], axis=0)
        return lax.dot_general(kw, jnp.concatenate(stack, axis=0), (((1,), (1,)), ((), ())),
                               preferred_element_type=jnp.float32)

    for t in range(SW_LOOKAHEAD):
        s_ref[t] = scores(t)

    def body(trip, carry):
        for t in range(SW_LOOKAHEAD):
            blk = trip * SW_LOOKAHEAD + t
            s = s_ref[t]
            s_ref[t] = scores(jnp.minimum(blk + SW_LOOKAHEAD, nblk - 1))
            left, n, right = neighbours(blk)
            edge = jnp.where(n == 0, 1, jnp.where(n == nblk_total - 1, 2, 0))
            s = s + bias_ref[edge]
            m = jnp.maximum(jnp.max(s, axis=0, keepdims=True), sink)
            p = jnp.exp2(s - m).astype(bf)
            vt_ext = jnp.concatenate(
                [jnp.concatenate([vt_ref[0, left], vt_ref[0, n], vt_ref[0, right]], axis=1), ones],
                axis=0)
            o = jnp.dot(vt_ext, p, preferred_element_type=jnp.float32)
            res = o[:LANES] / (o[LANES:LANES + 1] + jnp.exp2(sink - m))
            rows = pl.ds(pl.multiple_of(blk * SW_BLOCK, SW_BLOCK), SW_BLOCK)
            for r in range(rep):
                yt = jnp.where(row_group == 0, res[:, r * SW_BLOCK:(r + 1) * SW_BLOCK],
                               res[:, (rep + r) * SW_BLOCK:(rep + r + 1) * SW_BLOCK])
                o_ref[0, rows, r * LANES:(r + 1) * LANES] = yt.T
        return carry

    lax.fori_loop(0, nblk // SW_LOOKAHEAD, body, 0, unroll=SW_UNROLL)


def _sw_attention(q, k, vt, bucket, t5_flat, sink):
    B, S, _ = q.shape
    nb = S // SW_BLOCK
    return pl.pallas_call(
        functools.partial(_sw_kernel, seq=S),
        out_shape=jax.ShapeDtypeStruct((B, S, SW_QW), jnp.float32),
        grid_spec=pltpu.PrefetchScalarGridSpec(
            num_scalar_prefetch=2,
            grid=(B, S // SW_TQ),
            in_specs=[pl.BlockSpec((1, SW_TQ, SW_QW), lambda b, i, *_: (b, i, 0)),
                      pl.BlockSpec((1, S, SW_KW), lambda b, i, *_: (b, 0, 0),
                                   pipeline_mode=pl.Buffered(1)),
                      pl.BlockSpec((1, nb, SW_KW, SW_BLOCK), lambda b, i, *_: (b, 0, 0, 0),
                                   pipeline_mode=pl.Buffered(1)),
                      pl.BlockSpec(bucket.shape, lambda b, i, *_: (0, 0),
                                   pipeline_mode=pl.Buffered(1))],
            out_specs=pl.BlockSpec((1, SW_TQ, SW_QW), lambda b, i, *_: (b, i, 0)),
            scratch_shapes=[pltpu.VMEM((3, 3 * SW_BLOCK, SW_HEADS * SW_BLOCK), jnp.float32),
                            pltpu.VMEM((SW_LOOKAHEAD, 3 * SW_BLOCK, SW_HEADS * SW_BLOCK),
                                       jnp.float32)],
        ),
        compiler_params=_cparams("arbitrary", "arbitrary"),
        name="sw_attn",
    )(sink, t5_flat, q, k, vt, bucket)


def _t5_bucket(rel):
    nb = T5_BUCKETS // 2
    ret = (rel > 0).astype(jnp.int32) * nb
    n = jnp.abs(rel)
    max_exact = nb // 2
    nf = jnp.maximum(n, max_exact).astype(jnp.float32)
    large = max_exact + (jnp.log(nf / max_exact) / math.log(T5_MAX_DIST / max_exact)
                         * (nb - max_exact)).astype(jnp.int32)
    large = jnp.minimum(large, nb - 1)
    return ret + jnp.where(n < max_exact, n, large)


def _sw_bucket_tile():
    qpos = jnp.arange(SW_BLOCK)
    kpos = jnp.arange(3 * SW_BLOCK) - SW_BLOCK
    rel = kpos[:, None] - qpos[None, :]
    return jnp.where(jnp.abs(rel) <= SW_RADIUS, _t5_bucket(rel), -1).astype(jnp.int32)


def _ax_kernel(q_ref, qnorm_ref, kmax_ref, k_ref, vt_ref, inv_ref, o_ref, qpad_ref, m_ref, l_ref,
               acc_ref, s0_ref, *, nkt, bounded):
    tq = q_ref.shape[-1]
    tk = AX_TK
    rep = AX_HEADS // AX_KV_HEADS
    bf = jnp.bfloat16
    inv = inv_ref[...]
    for h in range(AX_HEADS):
        g = h // rep
        head = slice(h * HEAD_DIM, (h + 1) * HEAD_DIM)
        if bounded:
            row = lax.broadcasted_iota(jnp.int32, (32, tq), 0)
            shift = qnorm_ref[0, h:h + 1, :] * jnp.sqrt(kmax_ref[0, :, g * HEAD_DIM:g * HEAD_DIM + 1])
            extra = jnp.where(row == 0, -shift, 0.0).astype(F8)
            qpad_ref[h] = jnp.concatenate(
                [q_ref[0, 0, head, :], q_ref[0, 1, head, :], q_ref[0, 2, head, :], extra,
                 jnp.zeros((32, tq), F8)], axis=0)
        else:
            qpad_ref[h] = jnp.concatenate(
                [q_ref[0, head, :], jnp.zeros((LANES - HEAD_DIM, tq), bf)], axis=0)
    if not bounded:
        m_ref[...] = jnp.full(m_ref.shape, -jnp.inf, jnp.float32)
        l_ref[...] = jnp.zeros(l_ref.shape, jnp.float32)
    acc_ref[...] = jnp.zeros(acc_ref.shape, jnp.float32)

    def score(kt, h):
        ks = pl.multiple_of(kt * tk, tk)
        return jnp.dot(k_ref[0, h // rep, pl.ds(ks, tk), :], qpad_ref[h],
                       preferred_element_type=jnp.float32)

    for h in range(AX_LOOKAHEAD):
        s0_ref[h] = score(0, h)

    def body(kt, carry):
        pending = [s0_ref[h] for h in range(AX_LOOKAHEAD)]
        for h in range(AX_HEADS):
            g = h // rep
            s = pending.pop(0)
            ahead = h + AX_LOOKAHEAD
            if ahead < AX_HEADS:
                pending.append(score(kt, ahead))
            else:
                s0_ref[ahead - AX_HEADS] = score(jnp.minimum(kt + 1, nkt - 1), ahead - AX_HEADS)
            vt = vt_ref[0, kt, g * HEAD_DIM:(g + 1) * HEAD_DIM, :]
            s = s * inv
            if bounded:
                p = jnp.exp2(s)
                vt_ext = jnp.concatenate([vt, jnp.ones((AX_ONES_ROWS, tk), bf)], axis=0)
                acc_ref[h] += jnp.dot(vt_ext, p.astype(bf), preferred_element_type=jnp.float32)
            else:
                m_old = m_ref[h]
                m_new = jnp.maximum(m_old, jnp.max(s, axis=0, keepdims=True))
                alpha = jnp.exp2(m_old - m_new)
                p = jnp.exp2(s - m_new)
                l_ref[h] = alpha * l_ref[h] + jnp.sum(p.reshape(tk // 8, 8, tq), axis=0)
                m_ref[h] = m_new
                pv = jnp.dot(vt, p.astype(bf), preferred_element_type=jnp.float32)
                acc_ref[h, :HEAD_DIM] = alpha * acc_ref[h, :HEAD_DIM] + pv
        return carry

    lax.fori_loop(0, nkt, body, 0, unroll=AX_UNROLL)
    if bounded:
        denom = [acc_ref[h, HEAD_DIM:HEAD_DIM + 1] for h in range(AX_HEADS)]
    else:
        denom = [jnp.sum(l_ref[h], axis=0, keepdims=True) for h in range(AX_HEADS)]
    out_t = jnp.concatenate([acc_ref[h, :HEAD_DIM] / denom[h]
                             for h in range(AX_HEADS)], axis=0)
    o_ref[0] = out_t.T


def _ax_attention(qt, k, q8, k8, qnorm, kmax, vt, inv, bounded):
    B, W, S = qt.shape
    nkt = vt.shape[1]
    if bounded:
        q, keys = q8, k8
        q_spec = pl.BlockSpec((1, 3, W, AX_TQ), lambda b, i: (b, 0, 0, i))
        qpad = pltpu.VMEM((AX_HEADS, 2 * LANES, AX_TQ), F8)
    else:
        q, keys = qt, k
        q_spec = pl.BlockSpec((1, W, AX_TQ), lambda b, i: (b, 0, i))
        qpad = pltpu.VMEM((AX_HEADS, LANES, AX_TQ), jnp.bfloat16)
    return pl.pallas_call(
        functools.partial(_ax_kernel, nkt=nkt, bounded=bounded),
        out_shape=jax.ShapeDtypeStruct((B, S, W), jnp.float32),
        grid=(B, S // AX_TQ),
        in_specs=[q_spec,
                  pl.BlockSpec((1, 8, AX_TQ), lambda b, i: (b, 0, i)),
                  pl.BlockSpec((1, 1, AX_KW), lambda b, i: (b, 0, 0)),
                  pl.BlockSpec((1,) + keys.shape[1:], lambda b, i: (b, 0, 0, 0),
                               pipeline_mode=pl.Buffered(1)),
                  pl.BlockSpec((1, nkt, AX_KW, AX_TK), lambda b, i: (b, 0, 0, 0),
                               pipeline_mode=pl.Buffered(1)),
                  _const_spec((1, 1))],
        out_specs=pl.BlockSpec((1, AX_TQ, W), lambda b, i: (b, i, 0)),
        scratch_shapes=[qpad,
                        pltpu.VMEM((AX_HEADS, 1, AX_TQ), jnp.float32),
                        pltpu.VMEM((AX_HEADS, 8, AX_TQ), jnp.float32),
                        pltpu.VMEM((AX_HEADS, HEAD_DIM + AX_ONES_ROWS, AX_TQ), jnp.float32),
                        pltpu.VMEM((AX_LOOKAHEAD, AX_TK, AX_TQ), jnp.float32)],
        compiler_params=_cparams("arbitrary", "arbitrary"),
        name="ax_attn_bounded" if bounded else "ax_attn_online",
    )(q, qnorm, kmax, keys, vt, inv)


def _ax_dispatch(qt, k, q8, k8, qnorm, kmax, vt, inv, gain_max):
    rep = AX_HEADS // AX_KV_HEADS
    knorm = jnp.sqrt(kmax[:, 0, ::HEAD_DIM])
    shift_max = jnp.max(jnp.max(qnorm[:, :AX_HEADS], axis=-1) * jnp.repeat(knorm, rep, axis=1))
    entry_max = HEAD_DIM ** 0.5 * gain_max * jnp.array([FP8_Q_HI * Q_SCALE_EXP2, 1.0])
    safe = ((shift_max * inv[0, 0] <= AX_MAX_SHIFT) & (shift_max <= F8_MAX / 2)
            & jnp.all(entry_max <= F8_MAX / 2))
    return lax.cond(safe,
                    functools.partial(_ax_attention, bounded=True),
                    functools.partial(_ax_attention, bounded=False),
                    qt, k, q8, k8, qnorm, kmax, vt, inv)


def _post_kernel(x_ref, ya_ref, yb_ref, yc_ref, gg_ref, wo_ref, gta_ref, gf_ref, scf_ref, shf_ref,
                 gtf_ref, wgu_ref, wd_ref, gfin_ref, o_ref, *, final):
    bf = jnp.bfloat16
    gg = gg_ref[...]
    y = jnp.concatenate([
        _rms(ya_ref[0], gg[:, 0:NA_W]).astype(bf),
        _rms(yb_ref[0], gg[:, NA_W:NA_W + SW_QW]).astype(bf),
        _rms(yc_ref[0], gg[:, NA_W + SW_QW:]).astype(bf)], axis=1)
    x1 = x_ref[0] + gta_ref[0] * jnp.dot(y, wo_ref[...], preferred_element_type=jnp.float32)

    h = (_rms(x1, gf_ref[...]) * (1.0 + scf_ref[0]) + shf_ref[0]).astype(bf)
    acc = jnp.zeros(x1.shape, jnp.float32)
    for c0 in range(0, FFN_HIDDEN, FFN_CHUNK):
        c1 = min(c0 + FFN_CHUNK, FFN_HIDDEN)
        gate = jnp.dot(h, wgu_ref[:, c0:c1], preferred_element_type=jnp.float32)
        up = jnp.dot(h, wgu_ref[:, FFN_HIDDEN + c0:FFN_HIDDEN + c1],
                     preferred_element_type=jnp.float32)
        act = (gate * (1.0 / (1.0 + jnp.exp(-gate))) * up).astype(bf)
        acc = acc + jnp.dot(act, wd_ref[c0:c1, :], preferred_element_type=jnp.float32)
    x2 = x1 + gtf_ref[0] * acc
    if final:
        x2 = _rms(x2, gfin_ref[...])
    o_ref[0] = x2


def _post(x, ya, yb, yc, gg, wo, gta, gf, scf, shf, gtf, wgu, wd, gfin, layer, final):
    B, S, D = x.shape
    tm = TOK_TILE
    tok = lambda width: pl.BlockSpec((1, tm, width), lambda b, i: (b, i, 0))
    vec = pl.BlockSpec((1, 1, D), lambda b, i: (b, 0, 0))
    return pl.pallas_call(
        functools.partial(_post_kernel, final=final),
        out_shape=jax.ShapeDtypeStruct((B, S, D), jnp.float32),
        grid=(B, S // tm),
        in_specs=[tok(D), tok(NA_W), tok(SW_QW), tok(AX_QW), _const_spec((1, D)),
                  _layer_spec(wo, layer), vec, _const_spec((1, D)), vec, vec, vec,
                  _layer_spec(wgu, layer), _layer_spec(wd, layer), _const_spec((1, D))],
        out_specs=tok(D),
        compiler_params=_cparams("arbitrary", "arbitrary"),
        name="out_proj_ffn",
    )(x, ya, yb, yc, gg, wo, gta, gf, scf, shf, gtf, wgu, wd, gfin)


def _rope_tables(S):
    t = jnp.arange(S)
    row = (t // GRID_W).astype(jnp.float32)
    col = (t % GRID_W).astype(jnp.float32)
    axis_dim = HEAD_DIM // 2
    freqs = ROPE_THETA ** (-jnp.arange(0, axis_dim, 2, dtype=jnp.float32) / axis_dim)
    ang = jnp.stack([row[:, None] * freqs, col[:, None] * freqs], axis=1)
    cos = jnp.cos(ang)
    sin = jnp.sin(ang)
    cos_h = jnp.concatenate([cos, cos], axis=-1).reshape(S, HEAD_DIM)
    sin_h = jnp.concatenate([-sin, sin], axis=-1).reshape(S, HEAD_DIM)
    return jnp.tile(cos_h, (1, LANES // HEAD_DIM)), jnp.tile(sin_h, (1, LANES // HEAD_DIM))


def _pow2_normaliser(g):
    r = jnp.sqrt(jnp.mean(g.astype(jnp.float32) ** 2))
    return jnp.exp2(-jnp.round(jnp.log2(jnp.maximum(r, 1e-30))))


def _block_diag_ones(width):
    idx = np.arange(width) // HEAD_DIM
    return jnp.asarray(idx[:, None] == idx[None, :], dtype=jnp.bfloat16)


def _sw_relabel(t, start, axis):
    rep = SW_HEADS // SW_KV_HEADS
    seg = lax.slice_in_dim(t, start, start + SW_QW, axis=axis)
    shp = seg.shape
    seg = seg.reshape(shp[:axis] + (SW_KV_HEADS, rep, HEAD_DIM) + shp[axis + 1:])
    seg = jnp.swapaxes(seg, axis, axis + 1).reshape(shp)
    return jnp.concatenate([lax.slice_in_dim(t, 0, start, axis=axis), seg,
                            lax.slice_in_dim(t, start + SW_QW, t.shape[axis], axis=axis)], axis=axis)


def kernel(x, c, w_mod, b_mod, g_attn, w_in, rpb_na, sink_sw, t5_table, gq_ax, gk_ax, g_group,
           w_o, g_ffn, w_gu, w_down, g_final):
    B, S, D = x.shape
    L = w_mod.shape[0]
    bf = jnp.bfloat16

    c_pad = jnp.pad(c, ((0, 8 - B), (0, 0)))
    mod = _modulation(c_pad, w_mod, b_mod)[:, :B]
    mod = mod.reshape(L, B, 6, 1, D)

    cos, sin = _rope_tables(S)
    bd = _block_diag_ones(LANES)
    sw_bucket = _sw_bucket_tile()
    t5_flat = t5_table.astype(jnp.float32).reshape(-1)

    w_in_p = _sw_relabel(w_in, OFF_QB, axis=2).astype(bf)
    w_o_p = _sw_relabel(w_o, NA_W, axis=1).astype(bf)
    g_group_p = _sw_relabel(g_group, NA_W, axis=1)
    w_gu_b = w_gu.astype(bf)
    w_down_b = w_down.astype(bf)

    for l in range(L):
        sh_a, sc_a, gt_a, sh_f, sc_f, gt_f = [mod[l, :, i] for i in range(6)]
        gq = gq_ax[l] * _pow2_normaliser(gq_ax[l])
        gk = gk_ax[l] * _pow2_normaliser(gk_ax[l])
        inv = (1.0 / (_pow2_normaliser(gq_ax[l]) * _pow2_normaliser(gk_ax[l]))).reshape(1, 1)
        gain_max = jnp.stack([jnp.max(jnp.abs(gq)), jnp.max(jnp.abs(gk))])
        qa, ka, va, qb, kb, vb, qct, kc, vct, qnorm, kmax, q8, k8 = _in_proj(
            x, sc_a, sh_a, g_attn[l].reshape(1, D), w_in_p, l, cos, sin,
            jnp.tile(gq, LANES // HEAD_DIM).reshape(1, LANES),
            jnp.tile(gk, LANES // HEAD_DIM).reshape(1, LANES), bd)
        ya = _na_attention(qa, ka, va, _na_rpb_rows(rpb_na[l]))
        yb = _sw_attention(qb, kb, vb, sw_bucket, t5_flat, sink_sw[l])
        yc = _ax_dispatch(qct, kc, q8, k8, qnorm, kmax, vct, inv, gain_max)
        x = _post(x, ya, yb, yc, g_group_p[l].reshape(1, D), w_o_p, gt_a, g_ffn[l].reshape(1, D),
                  sc_f, sh_f, gt_f, w_gu_b, w_down_b, g_final.reshape(1, D), layer=l,
                  final=(l == L - 1))
    return x
```

```python
import functools
import math

import jax
import jax.numpy as jnp
import numpy as np
from jax import lax
from jax.experimental import pallas as pl
from jax.experimental.pallas import tpu as pltpu

D_MODEL = 1024
HEAD_DIM = 64
GRID_W = 64
NA_HEADS = 4
SW_HEADS = 6
SW_KV_HEADS = 2
AX_HEADS = 6
AX_KV_HEADS = 2
NA_WIN_ROWS = 8
NA_WIN_COLS = 16
SW_RADIUS = 128
SW_BLOCK = 128
T5_BUCKETS = 32
T5_MAX_DIST = 128
ROPE_THETA = 10000.0
FFN_HIDDEN = 2816
EPS = 1e-6
NEG_INF = -1e30

NA_W = NA_HEADS * HEAD_DIM
SW_QW = SW_HEADS * HEAD_DIM
SW_KW = SW_KV_HEADS * HEAD_DIM
AX_QW = AX_HEADS * HEAD_DIM
AX_KW = AX_KV_HEADS * HEAD_DIM
IN_WIDTH = 3 * NA_W + SW_QW + 2 * SW_KW + AX_QW + 2 * AX_KW
OFF_QA, OFF_KA, OFF_VA = 0, NA_W, 2 * NA_W
OFF_QB = 3 * NA_W
OFF_KB = OFF_QB + SW_QW
OFF_VB = OFF_KB + SW_KW
OFF_QC = OFF_VB + SW_KW
OFF_KC = OFF_QC + AX_QW
OFF_VC = OFF_KC + AX_KW

LANES = 128
VMEM_LIMIT = 56 * 1024 * 1024

MOD_TILE = 1536
TOK_TILE = 512
IN_TILE = 1024
IN_SUB = 256
AX_TQ = 256
AX_TK = 512
AX_LOOKAHEAD = 2
AX_UNROLL = 8
AX_ONES_ROWS = 16
NA_QROWS = 4
NA_BAND = NA_QROWS + NA_WIN_ROWS
NA_TQ = 2048
NA_LOOKAHEAD = 2
NA_UNROLL = 2
SW_TQ = 2048
SW_LOOKAHEAD = 2
SW_UNROLL = 4
FFN_CHUNK = 512

QK_SCALE = HEAD_DIM ** -0.5
F8 = jnp.float8_e4m3fn
F8_MAX = 448.0
FP8_Q_HI, FP8_K_HI = 2.0, 0.5
FP8_Q_LO, FP8_K_LO = 16.0, 1.0 / 16.0
LOG2E = math.log2(math.e)
Q_SCALE_EXP2 = QK_SCALE * LOG2E
AX_MAX_SHIFT = 60.0


def _cparams(*sem):
    return pltpu.CompilerParams(dimension_semantics=sem, vmem_limit_bytes=VMEM_LIMIT)


def _const_spec(shape):
    n = len(shape)
    return pl.BlockSpec(shape, lambda *_: (0,) * n, pipeline_mode=pl.Buffered(1))


def _layer_spec(stacked, layer):
    n = stacked.ndim - 1
    return pl.BlockSpec((None,) + stacked.shape[1:], lambda *_: (layer,) + (0,) * n,
                        pipeline_mode=pl.Buffered(1))


def _rms(x, g):
    return x * lax.rsqrt(jnp.mean(x * x, axis=-1, keepdims=True) + EPS) * g


def _mod_kernel(c_ref, w_ref, b_ref, o_ref):
    c = c_ref[...]
    cond = c * (1.0 / (1.0 + jnp.exp(-c)))
    o_ref[0] = jnp.dot(cond, w_ref[0], preferred_element_type=jnp.float32,
                       precision=lax.Precision.HIGHEST) + b_ref[0]


def _modulation(c_pad, w_mod, b_mod):
    L, D, N = w_mod.shape
    tn = MOD_TILE
    return pl.pallas_call(
        _mod_kernel,
        out_shape=jax.ShapeDtypeStruct((L, c_pad.shape[0], N), jnp.float32),
        grid=(L, N // tn),
        in_specs=[pl.BlockSpec(c_pad.shape, lambda l, j: (0, 0)),
                  pl.BlockSpec((1, D, tn), lambda l, j: (l, 0, j)),
                  pl.BlockSpec((1, 1, tn), lambda l, j: (l, 0, j))],
        out_specs=pl.BlockSpec((1, c_pad.shape[0], tn), lambda l, j: (l, 0, j)),
        compiler_params=_cparams("arbitrary", "arbitrary"),
        name="adaln_mod",
    )(c_pad, w_mod, b_mod.reshape(L, 1, N))


def _head_sumsq(t, bd):
    t2 = (t * t).astype(jnp.bfloat16)
    return jnp.concatenate(
        [jnp.dot(t2[:, c:c + LANES], bd, preferred_element_type=jnp.float32)
         for c in range(0, t.shape[1], LANES)], axis=1)


def _rope_chunk(t, cos, sin_signed, first_half):
    swapped = jnp.where(first_half, pltpu.roll(t, LANES - 16, 1), pltpu.roll(t, 16, 1))
    return t * cos + swapped * sin_signed


def _in_kernel(x_ref, sc_ref, sh_ref, g_ref, w_ref, cos_ref, sin_ref, gq_ref, gk_ref,
               bd_ref,
               qa_ref, ka_ref, vat_ref, qb_ref, kb_ref, vbt_ref, qct_ref, kc_ref, vct_ref,
               qnorm_ref, kmax_ref, q8_ref, k8_ref):
    bf = jnp.bfloat16
    n_sub = x_ref.shape[1] // IN_SUB
    lane = lax.broadcasted_iota(jnp.int32, (IN_SUB, LANES), 1)
    first_half = (lane % 32) < 16

    def normed(j):
        x = x_ref[0, j * IN_SUB:(j + 1) * IN_SUB, :]
        return (_rms(x, g_ref[...]) * (1.0 + sc_ref[0]) + sh_ref[0]).astype(bf)

    def project(h):
        return jnp.dot(h, w_ref[...], preferred_element_type=jnp.float32)

    def head_stats(proj):
        return (_head_sumsq(proj[:, OFF_QC:OFF_QC + AX_QW], bd_ref[...]),
                _head_sumsq(proj[:, OFF_KC:OFF_KC + AX_KW], bd_ref[...]))

    def finish(j, proj, stats):
        rows = slice(j * IN_SUB, (j + 1) * IN_SUB)
        qa_ref[0, rows] = (proj[:, OFF_QA:OFF_QA + NA_W] * Q_SCALE_EXP2).astype(bf)
        ka_ref[0, rows] = proj[:, OFF_KA:OFF_KA + NA_W].astype(bf)
        qb_ref[0, rows] = (proj[:, OFF_QB:OFF_QB + SW_QW] * Q_SCALE_EXP2).astype(bf)
        kb_ref[0, rows] = proj[:, OFF_KB:OFF_KB + SW_KW].astype(bf)
        per_sub = IN_SUB // LANES
        for t in range(per_sub):
            blk = slice(t * LANES, (t + 1) * LANES)
            vat_ref[0, j * per_sub + t] = proj[blk, OFF_VA:OFF_VA + NA_W].T.astype(bf)
            vbt_ref[0, j * per_sub + t] = proj[blk, OFF_VB:OFF_VB + SW_KW].T.astype(bf)

        cos = cos_ref[rows, :]
        sin = sin_ref[rows, :]
        qss, kss = stats
        qn = proj[:, OFF_QC:OFF_QC + AX_QW] * lax.rsqrt(qss * (1.0 / HEAD_DIM) + EPS)
        chunks = []
        for c in range(AX_QW // LANES):
            t = qn[:, c * LANES:(c + 1) * LANES] * gq_ref[...]
            chunks.append(_rope_chunk(t, cos, sin, first_half) * Q_SCALE_EXP2)
        f32 = jnp.float32
        qt = jnp.concatenate(chunks, axis=1).T
        qct_ref[0, :, rows] = qt.astype(bf)
        a1 = (qt * FP8_Q_HI).astype(F8).astype(f32)
        a2 = ((qt - a1 * (1.0 / FP8_Q_HI)) * FP8_Q_LO).astype(F8).astype(f32)
        a3 = qt.astype(F8).astype(f32)
        q8_ref[0, 0, :, rows] = a1.astype(F8)
        q8_ref[0, 1, :, rows] = a2.astype(F8)
        q8_ref[0, 2, :, rows] = a3.astype(F8)
        qsq = a1 * a1 + a2 * a2 + a3 * a3
        norms = [jnp.sqrt(jnp.sum(qsq[hh * HEAD_DIM:(hh + 1) * HEAD_DIM], axis=0, keepdims=True))
                 for hh in range(AX_HEADS)]
        qnorm_ref[0, :, rows] = jnp.concatenate(
            norms + [jnp.zeros_like(norms[0])] * (8 - AX_HEADS), axis=0)

        kn = (proj[:, OFF_KC:OFF_KC + AX_KW] * lax.rsqrt(kss * (1.0 / HEAD_DIM) + EPS)
              * gk_ref[...])
        kr = _rope_chunk(kn, cos, sin, first_half)
        kb = kr.astype(bf)
        zero_bf = jnp.zeros_like(kb)
        kc_ref[0, 0, rows] = jnp.where(lane < HEAD_DIM, kb, zero_bf)
        kc_ref[0, 1, rows] = jnp.where(lane < HEAD_DIM, pltpu.roll(kr, HEAD_DIM, 1).astype(bf), zero_bf)
        x1 = kr * FP8_K_HI
        x2 = kr * FP8_K_LO
        x3 = kr - x1.astype(F8).astype(f32) * (1.0 / FP8_K_HI)
        ones_f = jnp.where(lane == HEAD_DIM, 1.0, 0.0)
        norms2 = []
        for g in range(AX_KV_HEADS):
            if g == 0:
                c0 = jnp.where(lane < HEAD_DIM, x1, pltpu.roll(x2, HEAD_DIM, 1))
                c1 = jnp.where(lane < HEAD_DIM, x3, ones_f)
            else:
                c0 = jnp.where(lane < HEAD_DIM, pltpu.roll(x1, HEAD_DIM, 1), x2)
                c1 = jnp.where(lane < HEAD_DIM, pltpu.roll(x3, HEAD_DIM, 1), ones_f)
            c0 = c0.astype(F8)
            c1 = c1.astype(F8)
            k8_ref[0, g, rows] = jnp.concatenate([c0, c1], axis=1)
            c0f = c0.astype(f32)
            c1f = jnp.where(lane < HEAD_DIM, c1.astype(f32), 0.0)
            norms2.append(jnp.max(jnp.sum(c0f * c0f + c1f * c1f, axis=1, keepdims=True),
                                  axis=0, keepdims=True))
        first_tk = (j * IN_SUB) // AX_TK
        off = (j * IN_SUB) % AX_TK
        vct_ref[0, first_tk, :, off:off + IN_SUB] = proj[:, OFF_VC:OFF_VC + AX_KW].T.astype(bf)
        return jnp.where(lane[:1] < HEAD_DIM, norms2[0], norms2[1])

    proj = project(normed(0))
    tile_max = None
    for j in range(n_sub):
        if j + 1 < n_sub:
            h_next = normed(j + 1)
        stats = head_stats(proj)
        if j + 1 < n_sub:
            proj_next = project(h_next)
        sub_max = finish(j, proj, stats)
        tile_max = sub_max if tile_max is None else jnp.maximum(tile_max, sub_max)
        if j + 1 < n_sub:
            proj = proj_next

    first = pl.program_id(1) == 0

    @pl.when(first)
    def _():
        kmax_ref[0] = tile_max

    @pl.when(jnp.logical_not(first))
    def _():
        kmax_ref[0] = jnp.maximum(kmax_ref[0], tile_max)


def _in_proj(x, sc, sh, g, w, layer, cos, sin, gq, gk, bd):
    B, S, D = x.shape
    tm = IN_TILE
    bf = jnp.bfloat16
    tok = lambda width: pl.BlockSpec((1, tm, width), lambda b, i: (b, i, 0))
    vec = pl.BlockSpec((1, 1, D), lambda b, i: (b, 0, 0))
    out_shape = (
        jax.ShapeDtypeStruct((B, S, NA_W), bf), jax.ShapeDtypeStruct((B, S, NA_W), bf),
        jax.ShapeDtypeStruct((B, S // LANES, NA_W, LANES), bf),
        jax.ShapeDtypeStruct((B, S, SW_QW), bf), jax.ShapeDtypeStruct((B, S, SW_KW), bf),
        jax.ShapeDtypeStruct((B, S // SW_BLOCK, SW_KW, SW_BLOCK), bf),
        jax.ShapeDtypeStruct((B, AX_QW, S), bf),
        jax.ShapeDtypeStruct((B, AX_KV_HEADS, S, LANES), bf),
        jax.ShapeDtypeStruct((B, S // AX_TK, AX_KW, AX_TK), bf),
        jax.ShapeDtypeStruct((B, 8, S), jnp.float32),
        jax.ShapeDtypeStruct((B, 1, AX_KW), jnp.float32),
        jax.ShapeDtypeStruct((B, 3, AX_QW, S), F8),
        jax.ShapeDtypeStruct((B, AX_KV_HEADS, S, 2 * LANES), F8),
    )
    out_specs = (
        tok(NA_W), tok(NA_W),
        pl.BlockSpec((1, tm // LANES, NA_W, LANES), lambda b, i: (b, i, 0, 0)),
        tok(SW_QW), tok(SW_KW),
        pl.BlockSpec((1, tm // SW_BLOCK, SW_KW, SW_BLOCK), lambda b, i: (b, i, 0, 0)),
        pl.BlockSpec((1, AX_QW, tm), lambda b, i: (b, 0, i)),
        pl.BlockSpec((1, AX_KV_HEADS, tm, LANES), lambda b, i: (b, 0, i, 0)),
        pl.BlockSpec((1, tm // AX_TK, AX_KW, AX_TK), lambda b, i: (b, i, 0, 0)),
        pl.BlockSpec((1, 8, tm), lambda b, i: (b, 0, i)),
        pl.BlockSpec((1, 1, AX_KW), lambda b, i: (b, 0, 0)),
        pl.BlockSpec((1, 3, AX_QW, tm), lambda b, i: (b, 0, 0, i)),
        pl.BlockSpec((1, AX_KV_HEADS, tm, 2 * LANES), lambda b, i: (b, 0, i, 0)),
    )
    return pl.pallas_call(
        _in_kernel,
        out_shape=out_shape,
        grid=(B, S // tm),
        in_specs=[tok(D), vec, vec, _const_spec((1, D)), _layer_spec(w, layer),
                  pl.BlockSpec((tm, LANES), lambda b, i: (i, 0)),
                  pl.BlockSpec((tm, LANES), lambda b, i: (i, 0)),
                  _const_spec((1, LANES)), _const_spec((1, LANES)),
                  _const_spec(bd.shape)],
        out_specs=out_specs,
        compiler_params=_cparams("arbitrary", "arbitrary"),
        name="in_proj",
    )(x, sc, sh, g, w, cos, sin, gq, gk, bd)


def _na_build_bias(rp_ref, bias_ref, rows):
    shape = (GRID_W, LANES)
    kc = lax.broadcasted_iota(jnp.int32, shape, 0)
    lane = lax.broadcasted_iota(jnp.int32, shape, 1)
    c = lane % GRID_W
    cs = jnp.clip(c - NA_WIN_COLS // 2, 0, GRID_W - NA_WIN_COLS)
    col_ok = (kc >= cs) & (kc < cs + NA_WIN_COLS)
    left = lane < GRID_W
    neg = jnp.full(shape, NEG_INF, jnp.float32)
    tq = NA_QROWS * GRID_W
    for h in range(NA_HEADS):
        pair = []
        for a in range(2 * NA_WIN_ROWS):
            x = jnp.broadcast_to(rp_ref[h, a:a + 1, :] * LOG2E, shape)
            t = pltpu.roll(x, LANES - (NA_WIN_COLS - 1), 1, stride=1, stride_axis=0)
            pair.append(jnp.where(col_ok, t, neg))
        for variant, qr0 in enumerate((0, 2 * NA_QROWS, rows - NA_QROWS)):
            bs = int(np.clip(qr0 - NA_WIN_ROWS // 2, 0, rows - NA_BAND))
            for kj in range(NA_BAND):
                kr = bs + kj
                for u in range(NA_QROWS // 2):
                    r0 = qr0 + 2 * u
                    ok = [int(np.clip(r - NA_WIN_ROWS // 2, 0, rows - NA_WIN_ROWS)) <= kr
                          < int(np.clip(r - NA_WIN_ROWS // 2, 0, rows - NA_WIN_ROWS)) + NA_WIN_ROWS
                          for r in (r0, r0 + 1)]
                    a0 = kr - r0 + NA_WIN_ROWS - 1
                    if ok[0] and ok[1]:
                        tile = pair[a0]
                    elif ok[0]:
                        tile = jnp.where(left, pair[a0], neg)
                    elif ok[1]:
                        tile = jnp.where(left, neg, pair[a0])
                    else:
                        tile = neg
                    bias_ref[variant, kj * GRID_W:(kj + 1) * GRID_W,
                             h * tq + u * LANES:h * tq + (u + 1) * LANES] = tile


def _na_kernel(q_ref, k_ref, vt_ref, rp_ref, o_ref, bias_ref, s_ref, *, rows):
    i = pl.program_id(1)
    tq = NA_QROWS * GRID_W
    nblk = NA_TQ // tq
    nblk_total = rows // NA_QROWS
    bf = jnp.bfloat16

    @pl.when((pl.program_id(0) == 0) & (i == 0))
    def _():
        _na_build_bias(rp_ref, bias_ref, rows)

    head_of_lane = lax.broadcasted_iota(jnp.int32, (tq, NA_W), 1) // HEAD_DIM
    ones = jnp.ones((16, NA_BAND * GRID_W), bf)

    def band_row(blk):
        return jnp.clip((i * nblk + blk) * NA_QROWS - NA_WIN_ROWS // 2, 0, rows - NA_BAND)

    def scores(blk):
        q = q_ref[0, pl.ds(pl.multiple_of(blk * tq, tq), tq), :]
        stack = jnp.concatenate([jnp.where(head_of_lane == h, q, jnp.zeros_like(q))
                                 for h in range(NA_HEADS)], axis=0)
        start = pl.multiple_of(band_row(blk) * GRID_W, NA_QROWS * GRID_W)
        kb = k_ref[0, pl.ds(start, NA_BAND * GRID_W), :]
        return lax.dot_general(kb, stack, (((1,), (1,)), ((), ())),
                               preferred_element_type=jnp.float32)

    for t in range(NA_LOOKAHEAD):
        s_ref[t] = scores(t)

    def body(trip, carry):
        for t in range(NA_LOOKAHEAD):
            blk = trip * NA_LOOKAHEAD + t
            s = s_ref[t]
            s_ref[t] = scores(jnp.minimum(blk + NA_LOOKAHEAD, nblk - 1))
            n = i * nblk + blk
            variant = jnp.where(n == 0, 0, jnp.where(n == nblk_total - 1, 2, 1))
            s = s + bias_ref[variant]
            m = jnp.max(s, axis=0, keepdims=True)
            p = jnp.exp2(s - m).astype(bf)
            first = band_row(blk) * GRID_W // LANES
            vt = jnp.concatenate([vt_ref[0, first + u] for u in range(NA_BAND * GRID_W // LANES)],
                                 axis=1)
            outs = []
            for h in range(NA_HEADS):
                vt_ext = jnp.concatenate([vt[h * HEAD_DIM:(h + 1) * HEAD_DIM], ones], axis=0)
                o = jnp.dot(vt_ext, p[:, h * tq:(h + 1) * tq], preferred_element_type=jnp.float32)
                outs.append(o[:HEAD_DIM] / o[HEAD_DIM:HEAD_DIM + 1])
            o_ref[0, pl.ds(pl.multiple_of(blk * tq, tq), tq), :] = jnp.concatenate(outs, axis=0).T
        return carry

    lax.fori_loop(0, nblk // NA_LOOKAHEAD, body, 0, unroll=NA_UNROLL)


def _na_attention(q, k, vt, rp):
    B, S, W = q.shape
    rows = S // GRID_W
    tq = NA_QROWS * GRID_W
    return pl.pallas_call(
        functools.partial(_na_kernel, rows=rows),
        out_shape=jax.ShapeDtypeStruct((B, S, W), jnp.float32),
        grid=(B, S // NA_TQ),
        in_specs=[pl.BlockSpec((1, NA_TQ, W), lambda b, j: (b, j, 0)),
                  pl.BlockSpec((1, S, W), lambda b, j: (b, 0, 0), pipeline_mode=pl.Buffered(1)),
                  pl.BlockSpec((1, S // LANES, W, LANES), lambda b, j: (b, 0, 0, 0),
                               pipeline_mode=pl.Buffered(1)),
                  _const_spec(rp.shape)],
        out_specs=pl.BlockSpec((1, NA_TQ, W), lambda b, j: (b, j, 0)),
        scratch_shapes=[pltpu.VMEM((3, NA_BAND * GRID_W, NA_HEADS * tq), jnp.float32),
                        pltpu.VMEM((NA_LOOKAHEAD, NA_BAND * GRID_W, NA_HEADS * tq), jnp.float32)],
        compiler_params=_cparams("arbitrary", "arbitrary"),
        name="na_attn",
    )(q, k, vt, rp)


def _na_rpb_rows(rpb):
    n_col = rpb.shape[-1]
    p = jnp.pad(rpb.astype(jnp.float32)[:, :, ::-1], ((0, 0), (1, 1), (0, GRID_W - n_col)))
    return jnp.concatenate([p[:, 1:], p[:, :-1]], axis=-1)


def _sw_kernel(sink_ref, t5_ref, q_ref, k_ref, vt_ref, bucket_ref, o_ref, bias_ref, s_ref, *, seq):
    i = pl.program_id(1)
    nblk_total = seq // SW_BLOCK

    @pl.when((pl.program_id(0) == 0) & (i == 0))
    def _():
        bucket = bucket_ref[...]
        key_row = lax.broadcasted_iota(jnp.int32, bucket.shape, 0)
        for h in range(SW_HEADS):
            acc = jnp.full(bucket.shape, NEG_INF, jnp.float32)
            for b in range(T5_BUCKETS):
                acc = jnp.where(bucket == b, t5_ref[b * SW_HEADS + h] * LOG2E, acc)
            cols = slice(h * SW_BLOCK, (h + 1) * SW_BLOCK)
            bias_ref[0, :, cols] = acc
            bias_ref[1, :, cols] = jnp.where(key_row < SW_BLOCK, NEG_INF, acc)
            bias_ref[2, :, cols] = jnp.where(key_row >= 2 * SW_BLOCK, NEG_INF, acc)

    nblk = SW_TQ // SW_BLOCK
    rep = SW_HEADS // SW_KV_HEADS
    bf = jnp.bfloat16
    lane_group = lax.broadcasted_iota(jnp.int32, (SW_BLOCK, LANES), 1) // HEAD_DIM
    row_group = lax.broadcasted_iota(jnp.int32, (SW_BLOCK, LANES), 0) // HEAD_DIM
    sink = jnp.concatenate([jnp.full((1, SW_BLOCK), sink_ref[h] * LOG2E, jnp.float32)
                            for h in range(SW_HEADS)], axis=1)
    ones = jnp.ones((16, 3 * SW_BLOCK), bf)

    def neighbours(blk):
        n = i * nblk + blk
        return jnp.maximum(n - 1, 0), n, jnp.minimum(n + 1, nblk_total - 1)

    def scores(blk):
        rows = pl.ds(pl.multiple_of(blk * SW_BLOCK, SW_BLOCK), SW_BLOCK)
        stack = []
        for g in range(SW_KV_HEADS):
            for r in range(rep):
                qcol = q_ref[0, rows, r * LANES:(r + 1) * LANES]
                stack.append(jnp.where(lane_group == g, qcol, jnp.zeros_like(qcol)))
        kw = jnp.concatenate([k_ref[0, pl.ds(pl.multiple_of(nb * SW_BLOCK, SW_BLOCK), SW_BLOCK), :]
                              for nb in neighbours(blk)], axis=0)
        return lax.dot_general(kw, jnp.concatenate(stack, axis=0), (((1,), (1,)), ((), ())),
                               preferred_element_type=jnp.float32)

    for t in range(SW_LOOKAHEAD):
        s_ref[t] = scores(t)

    def body(trip, carry):
        for t in range(SW_LOOKAHEAD):
            blk = trip * SW_LOOKAHEAD + t
            s = s_ref[t]
            s_ref[t] = scores(jnp.minimum(blk + SW_LOOKAHEAD, nblk - 1))
            left, n, right = neighbours(blk)
            edge = jnp.where(n == 0, 1, jnp.where(n == nblk_total - 1, 2, 0))
            s = s + bias_ref[edge]
            m = jnp.maximum(jnp.max(s, axis=0, keepdims=True), sink)
            p = jnp.exp2(s - m).astype(bf)
            vt_ext = jnp.concatenate(
                [jnp.concatenate([vt_ref[0, left], vt_ref[0, n], vt_ref[0, right]], axis=1), ones],
                axis=0)
            o = jnp.dot(vt_ext, p, preferred_element_type=jnp.float32)
            res = o[:LANES] / (o[LANES:LANES + 1] + jnp.exp2(sink - m))
            rows = pl.ds(pl.multiple_of(blk * SW_BLOCK, SW_BLOCK), SW_BLOCK)
            for r in range(rep):
                yt = jnp.where(row_group == 0, res[:, r * SW_BLOCK:(r + 1) * SW_BLOCK],
                               res[:, (rep + r) * SW_BLOCK:(rep + r + 1) * SW_BLOCK])
                o_ref[0, rows, r * LANES:(r + 1) * LANES] = yt.T
        return carry

    lax.fori_loop(0, nblk // SW_LOOKAHEAD, body, 0, unroll=SW_UNROLL)


def _sw_attention(q, k, vt, bucket, t5_flat, sink):
    B, S, _ = q.shape
    nb = S // SW_BLOCK
    return pl.pallas_call(
        functools.partial(_sw_kernel, seq=S),
        out_shape=jax.ShapeDtypeStruct((B, S, SW_QW), jnp.float32),
        grid_spec=pltpu.PrefetchScalarGridSpec(
            num_scalar_prefetch=2,
            grid=(B, S // SW_TQ),
            in_specs=[pl.BlockSpec((1, SW_TQ, SW_QW), lambda b, i, *_: (b, i, 0)),
                      pl.BlockSpec((1, S, SW_KW), lambda b, i, *_: (b, 0, 0),
                                   pipeline_mode=pl.Buffered(1)),
                      pl.BlockSpec((1, nb, SW_KW, SW_BLOCK), lambda b, i, *_: (b, 0, 0, 0),
                                   pipeline_mode=pl.Buffered(1)),
                      pl.BlockSpec(bucket.shape, lambda b, i, *_: (0, 0),
                                   pipeline_mode=pl.Buffered(1))],
            out_specs=pl.BlockSpec((1, SW_TQ, SW_QW), lambda b, i, *_: (b, i, 0)),
            scratch_shapes=[pltpu.VMEM((3, 3 * SW_BLOCK, SW_HEADS * SW_BLOCK), jnp.float32),
                            pltpu.VMEM((SW_LOOKAHEAD, 3 * SW_BLOCK, SW_HEADS * SW_BLOCK),
                                       jnp.float32)],
        ),
        compiler_params=_cparams("arbitrary", "arbitrary"),
        name="sw_attn",
    )(sink, t5_flat, q, k, vt, bucket)


def _t5_bucket(rel):
    nb = T5_BUCKETS // 2
    ret = (rel > 0).astype(jnp.int32) * nb
    n = jnp.abs(rel)
    max_exact = nb // 2
    nf = jnp.maximum(n, max_exact).astype(jnp.float32)
    large = max_exact + (jnp.log(nf / max_exact) / math.log(T5_MAX_DIST / max_exact)
                         * (nb - max_exact)).astype(jnp.int32)
    large = jnp.minimum(large, nb - 1)
    return ret + jnp.where(n < max_exact, n, large)


def _sw_bucket_tile():
    qpos = jnp.arange(SW_BLOCK)
    kpos = jnp.arange(3 * SW_BLOCK) - SW_BLOCK
    rel = kpos[:, None] - qpos[None, :]
    return jnp.where(jnp.abs(rel) <= SW_RADIUS, _t5_bucket(rel), -1).astype(jnp.int32)


def _ax_kernel(q_ref, qnorm_ref, kmax_ref, k_ref, vt_ref, inv_ref, o_ref, qpad_ref, m_ref, l_ref,
               acc_ref, s0_ref, *, nkt, bounded):
    tq = q_ref.shape[-1]
    tk = AX_TK
    rep = AX_HEADS // AX_KV_HEADS
    bf = jnp.bfloat16
    inv = inv_ref[...]
    for h in range(AX_HEADS):
        g = h // rep
        head = slice(h * HEAD_DIM, (h + 1) * HEAD_DIM)
        if bounded:
            row = lax.broadcasted_iota(jnp.int32, (32, tq), 0)
            shift = qnorm_ref[0, h:h + 1, :] * jnp.sqrt(kmax_ref[0, :, g * HEAD_DIM:g * HEAD_DIM + 1])
            extra = jnp.where(row == 0, -shift, 0.0).astype(F8)
            qpad_ref[h] = jnp.concatenate(
                [q_ref[0, 0, head, :], q_ref[0, 1, head, :], q_ref[0, 2, head, :], extra,
                 jnp.zeros((32, tq), F8)], axis=0)
        else:
            qpad_ref[h] = jnp.concatenate(
                [q_ref[0, head, :], jnp.zeros((LANES - HEAD_DIM, tq), bf)], axis=0)
    if not bounded:
        m_ref[...] = jnp.full(m_ref.shape, -jnp.inf, jnp.float32)
        l_ref[...] = jnp.zeros(l_ref.shape, jnp.float32)
    acc_ref[...] = jnp.zeros(acc_ref.shape, jnp.float32)

    def score(kt, h):
        ks = pl.multiple_of(kt * tk, tk)
        return jnp.dot(k_ref[0, h // rep, pl.ds(ks, tk), :], qpad_ref[h],
                       preferred_element_type=jnp.float32)

    for h in range(AX_LOOKAHEAD):
        s0_ref[h] = score(0, h)

    def body(kt, carry):
        pending = [s0_ref[h] for h in range(AX_LOOKAHEAD)]
        for h in range(AX_HEADS):
            g = h // rep
            s = pending.pop(0)
            ahead = h + AX_LOOKAHEAD
            if ahead < AX_HEADS:
                pending.append(score(kt, ahead))
            else:
                s0_ref[ahead - AX_HEADS] = score(jnp.minimum(kt + 1, nkt - 1), ahead - AX_HEADS)
            vt = vt_ref[0, kt, g * HEAD_DIM:(g + 1) * HEAD_DIM, :]
            s = s * inv
            if bounded:
                p = jnp.exp2(s)
                vt_ext = jnp.concatenate([vt, jnp.ones((AX_ONES_ROWS, tk), bf)], axis=0)
                acc_ref[h] += jnp.dot(vt_ext, p.astype(bf), preferred_element_type=jnp.float32)
            else:
                m_old = m_ref[h]
                m_new = jnp.maximum(m_old, jnp.max(s, axis=0, keepdims=True))
                alpha = jnp.exp2(m_old - m_new)
                p = jnp.exp2(s - m_new)
                l_ref[h] = alpha * l_ref[h] + jnp.sum(p.reshape(tk // 8, 8, tq), axis=0)
                m_ref[h] = m_new
                pv = jnp.dot(vt, p.astype(bf), preferred_element_type=jnp.float32)
                acc_ref[h, :HEAD_DIM] = alpha * acc_ref[h, :HEAD_DIM] + pv
        return carry

    lax.fori_loop(0, nkt, body, 0, unroll=AX_UNROLL)
    if bounded:
        denom = [acc_ref[h, HEAD_DIM:HEAD_DIM + 1] for h in range(AX_HEADS)]
    else:
        denom = [jnp.sum(l_ref[h], axis=0, keepdims=True) for h in range(AX_HEADS)]
    out_t = jnp.concatenate([acc_ref[h, :HEAD_DIM] / denom[h]
                             for h in range(AX_HEADS)], axis=0)
    o_ref[0] = out_t.T


def _ax_attention(qt, k, q8, k8, qnorm, kmax, vt, inv, bounded):
    B, W, S = qt.shape
    nkt = vt.shape[1]
    if bounded:
        q, keys = q8, k8
        q_spec = pl.BlockSpec((1, 3, W, AX_TQ), lambda b, i: (b, 0, 0, i))
        qpad = pltpu.VMEM((AX_HEADS, 2 * LANES, AX_TQ), F8)
    else:
        q, keys = qt, k
        q_spec = pl.BlockSpec((1, W, AX_TQ), lambda b, i: (b, 0, i))
        qpad = pltpu.VMEM((AX_HEADS, LANES, AX_TQ), jnp.bfloat16)
    return pl.pallas_call(
        functools.partial(_ax_kernel, nkt=nkt, bounded=bounded),
        out_shape=jax.ShapeDtypeStruct((B, S, W), jnp.float32),
        grid=(B, S // AX_TQ),
        in_specs=[q_spec,
                  pl.BlockSpec((1, 8, AX_TQ), lambda b, i: (b, 0, i)),
                  pl.BlockSpec((1, 1, AX_KW), lambda b, i: (b, 0, 0)),
                  pl.BlockSpec((1,) + keys.shape[1:], lambda b, i: (b, 0, 0, 0),
                               pipeline_mode=pl.Buffered(1)),
                  pl.BlockSpec((1, nkt, AX_KW, AX_TK), lambda b, i: (b, 0, 0, 0),
                               pipeline_mode=pl.Buffered(1)),
                  _const_spec((1, 1))],
        out_specs=pl.BlockSpec((1, AX_TQ, W), lambda b, i: (b, i, 0)),
        scratch_shapes=[qpad,
                        pltpu.VMEM((AX_HEADS, 1, AX_TQ), jnp.float32),
                        pltpu.VMEM((AX_HEADS, 8, AX_TQ), jnp.float32),
                        pltpu.VMEM((AX_HEADS, HEAD_DIM + AX_ONES_ROWS, AX_TQ), jnp.float32),
                        pltpu.VMEM((AX_LOOKAHEAD, AX_TK, AX_TQ), jnp.float32)],
        compiler_params=_cparams("arbitrary", "arbitrary"),
        name="ax_attn_bounded" if bounded else "ax_attn_online",
    )(q, qnorm, kmax, keys, vt, inv)


def _ax_dispatch(qt, k, q8, k8, qnorm, kmax, vt, inv, gain_max):
    rep = AX_HEADS // AX_KV_HEADS
    knorm = jnp.sqrt(kmax[:, 0, ::HEAD_DIM])
    shift_max = jnp.max(jnp.max(qnorm[:, :AX_HEADS], axis=-1) * jnp.repeat(knorm, rep, axis=1))
    entry_max = HEAD_DIM ** 0.5 * gain_max * jnp.array([FP8_Q_HI * Q_SCALE_EXP2, 1.0])
    safe = ((shift_max * inv[0, 0] <= AX_MAX_SHIFT) & (shift_max <= F8_MAX / 2)
            & jnp.all(entry_max <= F8_MAX / 2))
    return lax.cond(safe,
                    functools.partial(_ax_attention, bounded=True),
                    functools.partial(_ax_attention, bounded=False),
                    qt, k, q8, k8, qnorm, kmax, vt, inv)


def _post_kernel(x_ref, ya_ref, yb_ref, yc_ref, gg_ref, wo_ref, gta_ref, gf_ref, scf_ref, shf_ref,
                 gtf_ref, wgu_ref, wd_ref, gfin_ref, o_ref, *, final):
    bf = jnp.bfloat16
    gg = gg_ref[...]
    y = jnp.concatenate([
        _rms(ya_ref[0], gg[:, 0:NA_W]).astype(bf),
        _rms(yb_ref[0], gg[:, NA_W:NA_W + SW_QW]).astype(bf),
        _rms(yc_ref[0], gg[:, NA_W + SW_QW:]).astype(bf)], axis=1)
    x1 = x_ref[0] + gta_ref[0] * jnp.dot(y, wo_ref[...], preferred_element_type=jnp.float32)

    h = (_rms(x1, gf_ref[...]) * (1.0 + scf_ref[0]) + shf_ref[0]).astype(bf)
    acc = jnp.zeros(x1.shape, jnp.float32)
    for c0 in range(0, FFN_HIDDEN, FFN_CHUNK):
        c1 = min(c0 + FFN_CHUNK, FFN_HIDDEN)
        gate = jnp.dot(h, wgu_ref[:, c0:c1], preferred_element_type=jnp.float32)
        up = jnp.dot(h, wgu_ref[:, FFN_HIDDEN + c0:FFN_HIDDEN + c1],
                     preferred_element_type=jnp.float32)
        act = (gate * (1.0 / (1.0 + jnp.exp(-gate))) * up).astype(bf)
        acc = acc + jnp.dot(act, wd_ref[c0:c1, :], preferred_element_type=jnp.float32)
    x2 = x1 + gtf_ref[0] * acc
    if final:
        x2 = _rms(x2, gfin_ref[...])
    o_ref[0] = x2


def _post(x, ya, yb, yc, gg, wo, gta, gf, scf, shf, gtf, wgu, wd, gfin, layer, final):
    B, S, D = x.shape
    tm = TOK_TILE
    tok = lambda width: pl.BlockSpec((1, tm, width), lambda b, i: (b, i, 0))
    vec = pl.BlockSpec((1, 1, D), lambda b, i: (b, 0, 0))
    return pl.pallas_call(
        functools.partial(_post_kernel, final=final),
        out_shape=jax.ShapeDtypeStruct((B, S, D), jnp.float32),
        grid=(B, S // tm),
        in_specs=[tok(D), tok(NA_W), tok(SW_QW), tok(AX_QW), _const_spec((1, D)),
                  _layer_spec(wo, layer), vec, _const_spec((1, D)), vec, vec, vec,
                  _layer_spec(wgu, layer), _layer_spec(wd, layer), _const_spec((1, D))],
        out_specs=tok(D),
        compiler_params=_cparams("arbitrary", "arbitrary"),
        name="out_proj_ffn",
    )(x, ya, yb, yc, gg, wo, gta, gf, scf, shf, gtf, wgu, wd, gfin)


def _rope_tables(S):
    t = jnp.arange(S)
    row = (t // GRID_W).astype(jnp.float32)
    col = (t % GRID_W).astype(jnp.float32)
    axis_dim = HEAD_DIM // 2
    freqs = ROPE_THETA ** (-jnp.arange(0, axis_dim, 2, dtype=jnp.float32) / axis_dim)
    ang = jnp.stack([row[:, None] * freqs, col[:, None] * freqs], axis=1)
    cos = jnp.cos(ang)
    sin = jnp.sin(ang)
    cos_h = jnp.concatenate([cos, cos], axis=-1).reshape(S, HEAD_DIM)
    sin_h = jnp.concatenate([-sin, sin], axis=-1).reshape(S, HEAD_DIM)
    return jnp.tile(cos_h, (1, LANES // HEAD_DIM)), jnp.tile(sin_h, (1, LANES // HEAD_DIM))


def _pow2_normaliser(g):
    r = jnp.sqrt(jnp.mean(g.astype(jnp.float32) ** 2))
    return jnp.exp2(-jnp.round(jnp.log2(jnp.maximum(r, 1e-30))))


def _block_diag_ones(width):
    idx = np.arange(width) // HEAD_DIM
    return jnp.asarray(idx[:, None] == idx[None, :], dtype=jnp.bfloat16)


def _sw_relabel(t, start, axis):
    rep = SW_HEADS // SW_KV_HEADS
    seg = lax.slice_in_dim(t, start, start + SW_QW, axis=axis)
    shp = seg.shape
    seg = seg.reshape(shp[:axis] + (SW_KV_HEADS, rep, HEAD_DIM) + shp[axis + 1:])
    seg = jnp.swapaxes(seg, axis, axis + 1).reshape(shp)
    return jnp.concatenate([lax.slice_in_dim(t, 0, start, axis=axis), seg,
                            lax.slice_in_dim(t, start + SW_QW, t.shape[axis], axis=axis)], axis=axis)


def kernel(x, c, w_mod, b_mod, g_attn, w_in, rpb_na, sink_sw, t5_table, gq_ax, gk_ax, g_group,
           w_o, g_ffn, w_gu, w_down, g_final):
    B, S, D = x.shape
    L = w_mod.shape[0]
    bf = jnp.bfloat16

    c_pad = jnp.pad(c, ((0, 8 - B), (0, 0)))
    mod = _modulation(c_pad, w_mod, b_mod)[:, :B]
    mod = mod.reshape(L, B, 6, 1, D)

    cos, sin = _rope_tables(S)
    bd = _block_diag_ones(LANES)
    sw_bucket = _sw_bucket_tile()
    t5_flat = t5_table.astype(jnp.float32).reshape(-1)

    w_in_p = _sw_relabel(w_in, OFF_QB, axis=2).astype(bf)
    w_o_p = _sw_relabel(w_o, NA_W, axis=1).astype(bf)
    g_group_p = _sw_relabel(g_group, NA_W, axis=1)
    w_gu_b = w_gu.astype(bf)
    w_down_b = w_down.astype(bf)

    for l in range(L):
        sh_a, sc_a, gt_a, sh_f, sc_f, gt_f = [mod[l, :, i] for i in range(6)]
        gq = gq_ax[l] * _pow2_normaliser(gq_ax[l])
        gk = gk_ax[l] * _pow2_normaliser(gk_ax[l])
        inv = (1.0 / (_pow2_normaliser(gq_ax[l]) * _pow2_normaliser(gk_ax[l]))).reshape(1, 1)
        gain_max = jnp.stack([jnp.max(jnp.abs(gq)), jnp.max(jnp.abs(gk))])
        qa, ka, va, qb, kb, vb, qct, kc, vct, qnorm, kmax, q8, k8 = _in_proj(
            x, sc_a, sh_a, g_attn[l].reshape(1, D), w_in_p, l, cos, sin,
            jnp.tile(gq, LANES // HEAD_DIM).reshape(1, LANES),
            jnp.tile(gk, LANES // HEAD_DIM).reshape(1, LANES), bd)
        ya = _na_attention(qa, ka, va, _na_rpb_rows(rpb_na[l]))
        yb = _sw_attention(qb, kb, vb, sw_bucket, t5_flat, sink_sw[l])
        yc = _ax_dispatch(qct, kc, q8, k8, qnorm, kmax, vct, inv, gain_max)
        x = _post(x, ya, yb, yc, g_group_p[l].reshape(1, D), w_o_p, gt_a, g_ffn[l].reshape(1, D),
                  sc_f, sh_f, gt_f, w_gu_b, w_down_b, g_final.reshape(1, D), layer=l,
                  final=(l == L - 1))
    return x
```

```python
import functools
import math

import jax
import jax.numpy as jnp
import numpy as np
from jax import lax
from jax.experimental import pallas as pl
from jax.experimental.pallas import tpu as pltpu

D_MODEL = 1024
HEAD_DIM = 64
GRID_W = 64
NA_HEADS = 4
SW_HEADS = 6
SW_KV_HEADS = 2
AX_HEADS = 6
AX_KV_HEADS = 2
NA_WIN_ROWS = 8
NA_WIN_COLS = 16
SW_RADIUS = 128
SW_BLOCK = 128
T5_BUCKETS = 32
T5_MAX_DIST = 128
ROPE_THETA = 10000.0
FFN_HIDDEN = 2816
EPS = 1e-6
NEG_INF = -1e30

NA_W = NA_HEADS * HEAD_DIM
SW_QW = SW_HEADS * HEAD_DIM
SW_KW = SW_KV_HEADS * HEAD_DIM
AX_QW = AX_HEADS * HEAD_DIM
AX_KW = AX_KV_HEADS * HEAD_DIM
IN_WIDTH = 3 * NA_W + SW_QW + 2 * SW_KW + AX_QW + 2 * AX_KW
OFF_QA, OFF_KA, OFF_VA = 0, NA_W, 2 * NA_W
OFF_QB = 3 * NA_W
OFF_KB = OFF_QB + SW_QW
OFF_VB = OFF_KB + SW_KW
OFF_QC = OFF_VB + SW_KW
OFF_KC = OFF_QC + AX_QW
OFF_VC = OFF_KC + AX_KW

LANES = 128
VMEM_LIMIT = 56 * 1024 * 1024

MOD_TILE = 1536
TOK_TILE = 1024
IN_TILE = 1024
IN_SUB = 256
AX_TQ = 256
AX_TK = 512
AX_LOOKAHEAD = 2
AX_UNROLL = 8
AX_ONES_ROWS = 16
NA_QROWS = 4
NA_BAND = NA_QROWS + NA_WIN_ROWS
NA_TQ = 2048
NA_LOOKAHEAD = 2
NA_UNROLL = 2
SW_TQ = 2048
SW_LOOKAHEAD = 2
SW_UNROLL = 4
FFN_CHUNK = 256

QK_SCALE = HEAD_DIM ** -0.5
F8 = jnp.float8_e4m3fn
F8_MAX = 448.0
FP8_Q_HI, FP8_K_HI = 2.0, 0.5
FP8_Q_LO, FP8_K_LO = 16.0, 1.0 / 16.0
LOG2E = math.log2(math.e)
Q_SCALE_EXP2 = QK_SCALE * LOG2E
AX_MAX_SHIFT = 60.0


def _cparams(*sem):
    return pltpu.CompilerParams(dimension_semantics=sem, vmem_limit_bytes=VMEM_LIMIT)


def _const_spec(shape):
    n = len(shape)
    return pl.BlockSpec(shape, lambda *_: (0,) * n, pipeline_mode=pl.Buffered(1))


def _layer_spec(stacked, layer):
    n = stacked.ndim - 1
    return pl.BlockSpec((None,) + stacked.shape[1:], lambda *_: (layer,) + (0,) * n,
                        pipeline_mode=pl.Buffered(1))


def _rms(x, g):
    return x * lax.rsqrt(jnp.mean(x * x, axis=-1, keepdims=True) + EPS) * g


def _mod_kernel(c_ref, w_ref, b_ref, o_ref):
    c = c_ref[...]
    cond = c * (1.0 / (1.0 + jnp.exp(-c)))
    o_ref[0] = jnp.dot(cond, w_ref[0], preferred_element_type=jnp.float32,
                       precision=lax.Precision.HIGHEST) + b_ref[0]


def _modulation(c_pad, w_mod, b_mod):
    L, D, N = w_mod.shape
    tn = MOD_TILE
    return pl.pallas_call(
        _mod_kernel,
        out_shape=jax.ShapeDtypeStruct((L, c_pad.shape[0], N), jnp.float32),
        grid=(L, N // tn),
        in_specs=[pl.BlockSpec(c_pad.shape, lambda l, j: (0, 0)),
                  pl.BlockSpec((1, D, tn), lambda l, j: (l, 0, j)),
                  pl.BlockSpec((1, 1, tn), lambda l, j: (l, 0, j))],
        out_specs=pl.BlockSpec((1, c_pad.shape[0], tn), lambda l, j: (l, 0, j)),
        compiler_params=_cparams("arbitrary", "arbitrary"),
        name="adaln_mod",
    )(c_pad, w_mod, b_mod.reshape(L, 1, N))


def _head_sumsq(t, bd):
    t2 = (t * t).astype(jnp.bfloat16)
    return jnp.concatenate(
        [jnp.dot(t2[:, c:c + LANES], bd, preferred_element_type=jnp.float32)
         for c in range(0, t.shape[1], LANES)], axis=1)


def _rope_chunk(t, cos, sin_signed, first_half):
    swapped = jnp.where(first_half, pltpu.roll(t, LANES - 16, 1), pltpu.roll(t, 16, 1))
    return t * cos + swapped * sin_signed


def _in_kernel(x_ref, sc_ref, sh_ref, g_ref, w_ref, cos_ref, sin_ref, gq_ref, gk_ref,
               bd_ref,
               qa_ref, ka_ref, vat_ref, qb_ref, kb_ref, vbt_ref, qct_ref, kc_ref, vct_ref,
               qnorm_ref, kmax_ref, q8_ref, k8_ref):
    bf = jnp.bfloat16
    n_sub = x_ref.shape[1] // IN_SUB
    lane = lax.broadcasted_iota(jnp.int32, (IN_SUB, LANES), 1)
    first_half = (lane % 32) < 16

    def normed(j):
        x = x_ref[0, j * IN_SUB:(j + 1) * IN_SUB, :]
        return (_rms(x, g_ref[...]) * (1.0 + sc_ref[0]) + sh_ref[0]).astype(bf)

    def project(h):
        return jnp.dot(h, w_ref[...], preferred_element_type=jnp.float32)

    def head_stats(proj):
        return (_head_sumsq(proj[:, OFF_QC:OFF_QC + AX_QW], bd_ref[...]),
                _head_sumsq(proj[:, OFF_KC:OFF_KC + AX_KW], bd_ref[...]))

    def finish(j, proj, stats):
        rows = slice(j * IN_SUB, (j + 1) * IN_SUB)
        qa_ref[0, rows] = (proj[:, OFF_QA:OFF_QA + NA_W] * Q_SCALE_EXP2).astype(bf)
        ka_ref[0, rows] = proj[:, OFF_KA:OFF_KA + NA_W].astype(bf)
        qb_ref[0, rows] = (proj[:, OFF_QB:OFF_QB + SW_QW] * Q_SCALE_EXP2).astype(bf)
        kb_ref[0, rows] = proj[:, OFF_KB:OFF_KB + SW_KW].astype(bf)
        per_sub = IN_SUB // LANES
        for t in range(per_sub):
            blk = slice(t * LANES, (t + 1) * LANES)
            vat_ref[0, j * per_sub + t] = proj[blk, OFF_VA:OFF_VA + NA_W].T.astype(bf)
            vbt_ref[0, j * per_sub + t] = proj[blk, OFF_VB:OFF_VB + SW_KW].T.astype(bf)

        cos = cos_ref[rows, :]
        sin = sin_ref[rows, :]
        qss, kss = stats
        qn = proj[:, OFF_QC:OFF_QC + AX_QW] * lax.rsqrt(qss * (1.0 / HEAD_DIM) + EPS)
        chunks = []
        for c in range(AX_QW // LANES):
            t = qn[:, c * LANES:(c + 1) * LANES] * gq_ref[...]
            chunks.append(_rope_chunk(t, cos, sin, first_half) * Q_SCALE_EXP2)
        f32 = jnp.float32
        qt = jnp.concatenate(chunks, axis=1).T
        qct_ref[0, :, rows] = qt.astype(bf)
        a1 = (qt * FP8_Q_HI).astype(F8).astype(f32)
        a2 = ((qt - a1 * (1.0 / FP8_Q_HI)) * FP8_Q_LO).astype(F8).astype(f32)
        a3 = qt.astype(F8).astype(f32)
        q8_ref[0, 0, :, rows] = a1.astype(F8)
        q8_ref[0, 1, :, rows] = a2.astype(F8)
        q8_ref[0, 2, :, rows] = a3.astype(F8)
        qsq = a1 * a1 + a2 * a2 + a3 * a3
        norms = [jnp.sqrt(jnp.sum(qsq[hh * HEAD_DIM:(hh + 1) * HEAD_DIM], axis=0, keepdims=True))
                 for hh in range(AX_HEADS)]
        qnorm_ref[0, :, rows] = jnp.concatenate(
            norms + [jnp.zeros_like(norms[0])] * (8 - AX_HEADS), axis=0)

        kn = (proj[:, OFF_KC:OFF_KC + AX_KW] * lax.rsqrt(kss * (1.0 / HEAD_DIM) + EPS)
              * gk_ref[...])
        kr = _rope_chunk(kn, cos, sin, first_half)
        kb = kr.astype(bf)
        zero_bf = jnp.zeros_like(kb)
        kc_ref[0, 0, rows] = jnp.where(lane < HEAD_DIM, kb, zero_bf)
        kc_ref[0, 1, rows] = jnp.where(lane < HEAD_DIM, pltpu.roll(kr, HEAD_DIM, 1).astype(bf), zero_bf)
        x1 = kr * FP8_K_HI
        x2 = kr * FP8_K_LO
        x3 = kr - x1.astype(F8).astype(f32) * (1.0 / FP8_K_HI)
        ones_f = jnp.where(lane == HEAD_DIM, 1.0, 0.0)
        norms2 = []
        for g in range(AX_KV_HEADS):
            if g == 0:
                c0 = jnp.where(lane < HEAD_DIM, x1, pltpu.roll(x2, HEAD_DIM, 1))
                c1 = jnp.where(lane < HEAD_DIM, x3, ones_f)
            else:
                c0 = jnp.where(lane < HEAD_DIM, pltpu.roll(x1, HEAD_DIM, 1), x2)
                c1 = jnp.where(lane < HEAD_DIM, pltpu.roll(x3, HEAD_DIM, 1), ones_f)
            c0 = c0.astype(F8)
            c1 = c1.astype(F8)
            k8_ref[0, g, rows] = jnp.concatenate([c0, c1], axis=1)
            c0f = c0.astype(f32)
            c1f = jnp.where(lane < HEAD_DIM, c1.astype(f32), 0.0)
            norms2.append(jnp.max(jnp.sum(c0f * c0f + c1f * c1f, axis=1, keepdims=True),
                                  axis=0, keepdims=True))
        first_tk = (j * IN_SUB) // AX_TK
        off = (j * IN_SUB) % AX_TK
        vct_ref[0, first_tk, :, off:off + IN_SUB] = proj[:, OFF_VC:OFF_VC + AX_KW].T.astype(bf)
        return jnp.where(lane[:1] < HEAD_DIM, norms2[0], norms2[1])

    proj = project(normed(0))
    tile_max = None
    for j in range(n_sub):
        if j + 1 < n_sub:
            h_next = normed(j + 1)
        stats = head_stats(proj)
        if j + 1 < n_sub:
            proj_next = project(h_next)
        sub_max = finish(j, proj, stats)
        tile_max = sub_max if tile_max is None else jnp.maximum(tile_max, sub_max)
        if j + 1 < n_sub:
            proj = proj_next

    first = pl.program_id(1) == 0

    @pl.when(first)
    def _():
        kmax_ref[0] = tile_max

    @pl.when(jnp.logical_not(first))
    def _():
        kmax_ref[0] = jnp.maximum(kmax_ref[0], tile_max)


def _in_proj(x, sc, sh, g, w, layer, cos, sin, gq, gk, bd):
    B, S, D = x.shape
    tm = IN_TILE
    bf = jnp.bfloat16
    tok = lambda width: pl.BlockSpec((1, tm, width), lambda b, i: (b, i, 0))
    vec = pl.BlockSpec((1, 1, D), lambda b, i: (b, 0, 0))
    out_shape = (
        jax.ShapeDtypeStruct((B, S, NA_W), bf), jax.ShapeDtypeStruct((B, S, NA_W), bf),
        jax.ShapeDtypeStruct((B, S // LANES, NA_W, LANES), bf),
        jax.ShapeDtypeStruct((B, S, SW_QW), bf), jax.ShapeDtypeStruct((B, S, SW_KW), bf),
        jax.ShapeDtypeStruct((B, S // SW_BLOCK, SW_KW, SW_BLOCK), bf),
        jax.ShapeDtypeStruct((B, AX_QW, S), bf),
        jax.ShapeDtypeStruct((B, AX_KV_HEADS, S, LANES), bf),
        jax.ShapeDtypeStruct((B, S // AX_TK, AX_KW, AX_TK), bf),
        jax.ShapeDtypeStruct((B, 8, S), jnp.float32),
        jax.ShapeDtypeStruct((B, 1, AX_KW), jnp.float32),
        jax.ShapeDtypeStruct((B, 3, AX_QW, S), F8),
        jax.ShapeDtypeStruct((B, AX_KV_HEADS, S, 2 * LANES), F8),
    )
    out_specs = (
        tok(NA_W), tok(NA_W),
        pl.BlockSpec((1, tm // LANES, NA_W, LANES), lambda b, i: (b, i, 0, 0)),
        tok(SW_QW), tok(SW_KW),
        pl.BlockSpec((1, tm // SW_BLOCK, SW_KW, SW_BLOCK), lambda b, i: (b, i, 0, 0)),
        pl.BlockSpec((1, AX_QW, tm), lambda b, i: (b, 0, i)),
        pl.BlockSpec((1, AX_KV_HEADS, tm, LANES), lambda b, i: (b, 0, i, 0)),
        pl.BlockSpec((1, tm // AX_TK, AX_KW, AX_TK), lambda b, i: (b, i, 0, 0)),
        pl.BlockSpec((1, 8, tm), lambda b, i: (b, 0, i)),
        pl.BlockSpec((1, 1, AX_KW), lambda b, i: (b, 0, 0)),
        pl.BlockSpec((1, 3, AX_QW, tm), lambda b, i: (b, 0, 0, i)),
        pl.BlockSpec((1, AX_KV_HEADS, tm, 2 * LANES), lambda b, i: (b, 0, i, 0)),
    )
    return pl.pallas_call(
        _in_kernel,
        out_shape=out_shape,
        grid=(B, S // tm),
        in_specs=[tok(D), vec, vec, _const_spec((1, D)), _layer_spec(w, layer),
                  pl.BlockSpec((tm, LANES), lambda b, i: (i, 0)),
                  pl.BlockSpec((tm, LANES), lambda b, i: (i, 0)),
                  _const_spec((1, LANES)), _const_spec((1, LANES)),
                  _const_spec(bd.shape)],
        out_specs=out_specs,
        compiler_params=_cparams("arbitrary", "arbitrary"),
        name="in_proj",
    )(x, sc, sh, g, w, cos, sin, gq, gk, bd)


def _na_build_bias(rp_ref, bias_ref, rows):
    shape = (GRID_W, LANES)
    kc = lax.broadcasted_iota(jnp.int32, shape, 0)
    lane = lax.broadcasted_iota(jnp.int32, shape, 1)
    c = lane % GRID_W
    cs = jnp.clip(c - NA_WIN_COLS // 2, 0, GRID_W - NA_WIN_COLS)
    col_ok = (kc >= cs) & (kc < cs + NA_WIN_COLS)
    left = lane < GRID_W
    neg = jnp.full(shape, NEG_INF, jnp.float32)
    tq = NA_QROWS * GRID_W
    for h in range(NA_HEADS):
        pair = []
        for a in range(2 * NA_WIN_ROWS):
            x = jnp.broadcast_to(rp_ref[h, a:a + 1, :] * LOG2E, shape)
            t = pltpu.roll(x, LANES - (NA_WIN_COLS - 1), 1, stride=1, stride_axis=0)
            pair.append(jnp.where(col_ok, t, neg))
        for variant, qr0 in enumerate((0, 2 * NA_QROWS, rows - NA_QROWS)):
            bs = int(np.clip(qr0 - NA_WIN_ROWS // 2, 0, rows - NA_BAND))
            for kj in range(NA_BAND):
                kr = bs + kj
                for u in range(NA_QROWS // 2):
                    r0 = qr0 + 2 * u
                    ok = [int(np.clip(r - NA_WIN_ROWS // 2, 0, rows - NA_WIN_ROWS)) <= kr
                          < int(np.clip(r - NA_WIN_ROWS // 2, 0, rows - NA_WIN_ROWS)) + NA_WIN_ROWS
                          for r in (r0, r0 + 1)]
                    a0 = kr - r0 + NA_WIN_ROWS - 1
                    if ok[0] and ok[1]:
                        tile = pair[a0]
                    elif ok[0]:
                        tile = jnp.where(left, pair[a0], neg)
                    elif ok[1]:
                        tile = jnp.where(left, neg, pair[a0])
                    else:
                        tile = neg
                    bias_ref[variant, kj * GRID_W:(kj + 1) * GRID_W,
                             h * tq + u * LANES:h * tq + (u + 1) * LANES] = tile


def _na_kernel(q_ref, k_ref, vt_ref, rp_ref, o_ref, bias_ref, s_ref, *, rows):
    i = pl.program_id(1)
    tq = NA_QROWS * GRID_W
    nblk = NA_TQ // tq
    nblk_total = rows // NA_QROWS
    bf = jnp.bfloat16

    @pl.when((pl.program_id(0) == 0) & (i == 0))
    def _():
        _na_build_bias(rp_ref, bias_ref, rows)

    head_of_lane = lax.broadcasted_iota(jnp.int32, (tq, NA_W), 1) // HEAD_DIM
    ones = jnp.ones((16, NA_BAND * GRID_W), bf)

    def band_row(blk):
        return jnp.clip((i * nblk + blk) * NA_QROWS - NA_WIN_ROWS // 2, 0, rows - NA_BAND)

    def scores(blk):
        q = q_ref[0, pl.ds(pl.multiple_of(blk * tq, tq), tq), :]
        stack = jnp.concatenate([jnp.where(head_of_lane == h, q, jnp.zeros_like(q))
                                 for h in range(NA_HEADS)], axis=0)
        start = pl.multiple_of(band_row(blk) * GRID_W, NA_QROWS * GRID_W)
        kb = k_ref[0, pl.ds(start, NA_BAND * GRID_W), :]
        return lax.dot_general(kb, stack, (((1,), (1,)), ((), ())),
                               preferred_element_type=jnp.float32)

    for t in range(NA_LOOKAHEAD):
        s_ref[t] = scores(t)

    def body(trip, carry):
        for t in range(NA_LOOKAHEAD):
            blk = trip * NA_LOOKAHEAD + t
            s = s_ref[t]
            s_ref[t] = scores(jnp.minimum(blk + NA_LOOKAHEAD, nblk - 1))
            n = i * nblk + blk
            variant = jnp.where(n == 0, 0, jnp.where(n == nblk_total - 1, 2, 1))
            s = s + bias_ref[variant]
            m = jnp.max(s, axis=0, keepdims=True)
            p = jnp.exp2(s - m).astype(bf)
            first = band_row(blk) * GRID_W // LANES
            vt = jnp.concatenate([vt_ref[0, first + u] for u in range(NA_BAND * GRID_W // LANES)],
                                 axis=1)
            outs = []
            for h in range(NA_HEADS):
                vt_ext = jnp.concatenate([vt[h * HEAD_DIM:(h + 1) * HEAD_DIM], ones], axis=0)
                o = jnp.dot(vt_ext, p[:, h * tq:(h + 1) * tq], preferred_element_type=jnp.float32)
                outs.append(o[:HEAD_DIM] / o[HEAD_DIM:HEAD_DIM + 1])
            o_ref[0, pl.ds(pl.multiple_of(blk * tq, tq), tq), :] = jnp.concatenate(outs, axis=0).T
        return carry

    lax.fori_loop(0, nblk // NA_LOOKAHEAD, body, 0, unroll=NA_UNROLL)


def _na_attention(q, k, vt, rp):
    B, S, W = q.shape
    rows = S // GRID_W
    tq = NA_QROWS * GRID_W
    return pl.pallas_call(
        functools.partial(_na_kernel, rows=rows),
        out_shape=jax.ShapeDtypeStruct((B, S, W), jnp.float32),
        grid=(B, S // NA_TQ),
        in_specs=[pl.BlockSpec((1, NA_TQ, W), lambda b, j: (b, j, 0)),
                  pl.BlockSpec((1, S, W), lambda b, j: (b, 0, 0), pipeline_mode=pl.Buffered(1)),
                  pl.BlockSpec((1, S // LANES, W, LANES), lambda b, j: (b, 0, 0, 0),
                               pipeline_mode=pl.Buffered(1)),
                  _const_spec(rp.shape)],
        out_specs=pl.BlockSpec((1, NA_TQ, W), lambda b, j: (b, j, 0)),
        scratch_shapes=[pltpu.VMEM((3, NA_BAND * GRID_W, NA_HEADS * tq), jnp.float32),
                        pltpu.VMEM((NA_LOOKAHEAD, NA_BAND * GRID_W, NA_HEADS * tq), jnp.float32)],
        compiler_params=_cparams("arbitrary", "arbitrary"),
        name="na_attn",
    )(q, k, vt, rp)


def _na_rpb_rows(rpb):
    n_col = rpb.shape[-1]
    p = jnp.pad(rpb.astype(jnp.float32)[:, :, ::-1], ((0, 0), (1, 1), (0, GRID_W - n_col)))
    return jnp.concatenate([p[:, 1:], p[:, :-1]], axis=-1)


def _sw_kernel(sink_ref, t5_ref, q_ref, k_ref, vt_ref, bucket_ref, o_ref, bias_ref, s_ref, *, seq):
    i = pl.program_id(1)
    nblk_total = seq // SW_BLOCK

    @pl.when((pl.program_id(0) == 0) & (i == 0))
    def _():
        bucket = bucket_ref[...]
        key_row = lax.broadcasted_iota(jnp.int32, bucket.shape, 0)
        for h in range(SW_HEADS):
            acc = jnp.full(bucket.shape, NEG_INF, jnp.float32)
            for b in range(T5_BUCKETS):
                acc = jnp.where(bucket == b, t5_ref[b * SW_HEADS + h] * LOG2E, acc)
            cols = slice(h * SW_BLOCK, (h + 1) * SW_BLOCK)
            bias_ref[0, :, cols] = acc
            bias_ref[1, :, cols] = jnp.where(key_row < SW_BLOCK, NEG_INF, acc)
            bias_ref[2, :, cols] = jnp.where(key_row >= 2 * SW_BLOCK, NEG_INF, acc)

    nblk = SW_TQ // SW_BLOCK
    rep = SW_HEADS // SW_KV_HEADS
    bf = jnp.bfloat16
    lane_group = lax.broadcasted_iota(jnp.int32, (SW_BLOCK, LANES), 1) // HEAD_DIM
    row_group = lax.broadcasted_iota(jnp.int32, (SW_BLOCK, LANES), 0) // HEAD_DIM
    sink = jnp.concatenate([jnp.full((1, SW_BLOCK), sink_ref[h] * LOG2E, jnp.float32)
                            for h in range(SW_HEADS)], axis=1)
    ones = jnp.ones((16, 3 * SW_BLOCK), bf)

    def neighbours(blk):
        n = i * nblk + blk
        return jnp.maximum(n - 1, 0), n, jnp.minimum(n + 1, nblk_total - 1)

    def scores(blk):
        rows = pl.ds(pl.multiple_of(blk * SW_BLOCK, SW_BLOCK), SW_BLOCK)
        stack = []
        for g in range(SW_KV_HEADS):
            for r in range(rep):
                qcol = q_ref[0, rows, r * LANES:(r + 1) * LANES]
                stack.append(jnp.where(lane_group == g, qcol, jnp.zeros_like(qcol)))
        kw = jnp.concatenate([k_ref[0, pl.ds(pl.multiple_of(nb * SW_BLOCK, SW_BLOCK), SW_BLOCK), :]
                              for nb in neighbours(blk)], axis=0)
        return lax.dot_general(kw, jnp.concatenate(stack, axis=0), (((1,), (1,)), ((), ())),
                               preferred_element_type=jnp.float32)

    for t in range(SW_LOOKAHEAD):
        s_ref[t] = scores(t)

    def body(trip, carry):
        for t in range(SW_LOOKAHEAD):
            blk = trip * SW_LOOKAHEAD + t
            s = s_ref[t]
            s_ref[t] = scores(jnp.minimum(blk + SW_LOOKAHEAD, nblk - 1))
            left, n, right = neighbours(blk)
            edge = jnp.where(n == 0, 1, jnp.where(n == nblk_total - 1, 2, 0))
            s = s + bias_ref[edge]
            m = jnp.maximum(jnp.max(s, axis=0, keepdims=True), sink)
            p = jnp.exp2(s - m).astype(bf)
            vt_ext = jnp.concatenate(
                [jnp.concatenate([vt_ref[0, left], vt_ref[0, n], vt_ref[0, right]], axis=1), ones],
                axis=0)
            o = jnp.dot(vt_ext, p, preferred_element_type=jnp.float32)
            res = o[:LANES] / (o[LANES:LANES + 1] + jnp.exp2(sink - m))
            rows = pl.ds(pl.multiple_of(blk * SW_BLOCK, SW_BLOCK), SW_BLOCK)
            for r in range(rep):
                yt = jnp.where(row_group == 0, res[:, r * SW_BLOCK:(r + 1) * SW_BLOCK],
                               res[:, (rep + r) * SW_BLOCK:(rep + r + 1) * SW_BLOCK])
                o_ref[0, rows, r * LANES:(r + 1) * LANES] = yt.T
        return carry

    lax.fori_loop(0, nblk // SW_LOOKAHEAD, body, 0, unroll=SW_UNROLL)


def _sw_attention(q, k, vt, bucket, t5_flat, sink):
    B, S, _ = q.shape
    nb = S // SW_BLOCK
    return pl.pallas_call(
        functools.partial(_sw_kernel, seq=S),
        out_shape=jax.ShapeDtypeStruct((B, S, SW_QW), jnp.float32),
        grid_spec=pltpu.PrefetchScalarGridSpec(
            num_scalar_prefetch=2,
            grid=(B, S // SW_TQ),
            in_specs=[pl.BlockSpec((1, SW_TQ, SW_QW), lambda b, i, *_: (b, i, 0)),
                      pl.BlockSpec((1, S, SW_KW), lambda b, i, *_: (b, 0, 0),
                                   pipeline_mode=pl.Buffered(1)),
                      pl.BlockSpec((1, nb, SW_KW, SW_BLOCK), lambda b, i, *_: (b, 0, 0, 0),
                                   pipeline_mode=pl.Buffered(1)),
                      pl.BlockSpec(bucket.shape, lambda b, i, *_: (0, 0),
                                   pipeline_mode=pl.Buffered(1))],
            out_specs=pl.BlockSpec((1, SW_TQ, SW_QW), lambda b, i, *_: (b, i, 0)),
            scratch_shapes=[pltpu.VMEM((3, 3 * SW_BLOCK, SW_HEADS * SW_BLOCK), jnp.float32),
                            pltpu.VMEM((SW_LOOKAHEAD, 3 * SW_BLOCK, SW_HEADS * SW_BLOCK),
                                       jnp.float32)],
        ),
        compiler_params=_cparams("arbitrary", "arbitrary"),
        name="sw_attn",
    )(sink, t5_flat, q, k, vt, bucket)


def _t5_bucket(rel):
    nb = T5_BUCKETS // 2
    ret = (rel > 0).astype(jnp.int32) * nb
    n = jnp.abs(rel)
    max_exact = nb // 2
    nf = jnp.maximum(n, max_exact).astype(jnp.float32)
    large = max_exact + (jnp.log(nf / max_exact) / math.log(T5_MAX_DIST / max_exact)
                         * (nb - max_exact)).astype(jnp.int32)
    large = jnp.minimum(large, nb - 1)
    return ret + jnp.where(n < max_exact, n, large)


def _sw_bucket_tile():
    qpos = jnp.arange(SW_BLOCK)
    kpos = jnp.arange(3 * SW_BLOCK) - SW_BLOCK
    rel = kpos[:, None] - qpos[None, :]
    return jnp.where(jnp.abs(rel) <= SW_RADIUS, _t5_bucket(rel), -1).astype(jnp.int32)


def _ax_kernel(q_ref, qnorm_ref, kmax_ref, k_ref, vt_ref, o_ref, qpad_ref, m_ref, l_ref,
               acc_ref, s0_ref, *, nkt, bounded):
    tq = q_ref.shape[-1]
    tk = AX_TK
    rep = AX_HEADS // AX_KV_HEADS
    bf = jnp.bfloat16
    for h in range(AX_HEADS):
        g = h // rep
        head = slice(h * HEAD_DIM, (h + 1) * HEAD_DIM)
        if bounded:
            row = lax.broadcasted_iota(jnp.int32, (32, tq), 0)
            shift = qnorm_ref[0, h:h + 1, :] * jnp.sqrt(kmax_ref[0, :, g * HEAD_DIM:g * HEAD_DIM + 1])
            extra = jnp.where(row == 0, -shift, 0.0).astype(F8)
            qpad_ref[h] = jnp.concatenate(
                [q_ref[0, 0, head, :], q_ref[0, 1, head, :], q_ref[0, 2, head, :], extra,
                 jnp.zeros((32, tq), F8)], axis=0)
        else:
            qpad_ref[h] = jnp.concatenate(
                [q_ref[0, head, :], jnp.zeros((LANES - HEAD_DIM, tq), bf)], axis=0)
    if not bounded:
        m_ref[...] = jnp.full(m_ref.shape, -jnp.inf, jnp.float32)
        l_ref[...] = jnp.zeros(l_ref.shape, jnp.float32)
    acc_ref[...] = jnp.zeros(acc_ref.shape, jnp.float32)

    def score(kt, h):
        ks = pl.multiple_of(kt * tk, tk)
        return jnp.dot(k_ref[0, h // rep, pl.ds(ks, tk), :], qpad_ref[h],
                       preferred_element_type=jnp.float32)

    for h in range(AX_LOOKAHEAD):
        s0_ref[h] = score(0, h)

    def body(kt, carry):
        pending = [s0_ref[h] for h in range(AX_LOOKAHEAD)]
        for h in range(AX_HEADS):
            g = h // rep
            s = pending.pop(0)
            ahead = h + AX_LOOKAHEAD
            if ahead < AX_HEADS:
                pending.append(score(kt, ahead))
            else:
                s0_ref[ahead - AX_HEADS] = score(jnp.minimum(kt + 1, nkt - 1), ahead - AX_HEADS)
            vt = vt_ref[0, kt, g * HEAD_DIM:(g + 1) * HEAD_DIM, :]
            if bounded:
                p = jnp.exp2(s)
                vt_ext = jnp.concatenate([vt, jnp.ones((AX_ONES_ROWS, tk), bf)], axis=0)
                acc_ref[h] += jnp.dot(vt_ext, p.astype(bf), preferred_element_type=jnp.float32)
            else:
                m_old = m_ref[h]
                m_new = jnp.maximum(m_old, jnp.max(s, axis=0, keepdims=True))
                alpha = jnp.exp2(m_old - m_new)
                p = jnp.exp2(s - m_new)
                l_ref[h] = alpha * l_ref[h] + jnp.sum(p.reshape(tk // 8, 8, tq), axis=0)
                m_ref[h] = m_new
                pv = jnp.dot(vt, p.astype(bf), preferred_element_type=jnp.float32)
                acc_ref[h, :HEAD_DIM] = alpha * acc_ref[h, :HEAD_DIM] + pv
        return carry

    lax.fori_loop(0, nkt, body, 0, unroll=AX_UNROLL)
    if bounded:
        denom = [acc_ref[h, HEAD_DIM:HEAD_DIM + 1] for h in range(AX_HEADS)]
    else:
        denom = [jnp.sum(l_ref[h], axis=0, keepdims=True) for h in range(AX_HEADS)]
    out_t = jnp.concatenate([acc_ref[h, :HEAD_DIM] / denom[h]
                             for h in range(AX_HEADS)], axis=0)
    o_ref[0] = out_t.T


def _ax_attention(qt, k, q8, k8, qnorm, kmax, vt, bounded):
    B, W, S = qt.shape
    nkt = vt.shape[1]
    if bounded:
        q, keys = q8, k8
        q_spec = pl.BlockSpec((1, 3, W, AX_TQ), lambda b, i: (b, 0, 0, i))
        qpad = pltpu.VMEM((AX_HEADS, 2 * LANES, AX_TQ), F8)
    else:
        q, keys = qt, k
        q_spec = pl.BlockSpec((1, W, AX_TQ), lambda b, i: (b, 0, i))
        qpad = pltpu.VMEM((AX_HEADS, LANES, AX_TQ), jnp.bfloat16)
    return pl.pallas_call(
        functools.partial(_ax_kernel, nkt=nkt, bounded=bounded),
        out_shape=jax.ShapeDtypeStruct((B, S, W), jnp.float32),
        grid=(B, S // AX_TQ),
        in_specs=[q_spec,
                  pl.BlockSpec((1, 8, AX_TQ), lambda b, i: (b, 0, i)),
                  pl.BlockSpec((1, 1, AX_KW), lambda b, i: (b, 0, 0)),
                  pl.BlockSpec((1,) + keys.shape[1:], lambda b, i: (b, 0, 0, 0),
                               pipeline_mode=pl.Buffered(1)),
                  pl.BlockSpec((1, nkt, AX_KW, AX_TK), lambda b, i: (b, 0, 0, 0),
                               pipeline_mode=pl.Buffered(1))],
        out_specs=pl.BlockSpec((1, AX_TQ, W), lambda b, i: (b, i, 0)),
        scratch_shapes=[qpad,
                        pltpu.VMEM((AX_HEADS, 1, AX_TQ), jnp.float32),
                        pltpu.VMEM((AX_HEADS, 8, AX_TQ), jnp.float32),
                        pltpu.VMEM((AX_HEADS, HEAD_DIM + AX_ONES_ROWS, AX_TQ), jnp.float32),
                        pltpu.VMEM((AX_LOOKAHEAD, AX_TK, AX_TQ), jnp.float32)],
        compiler_params=_cparams("arbitrary", "arbitrary"),
        name="ax_attn_bounded" if bounded else "ax_attn_online",
    )(q, qnorm, kmax, keys, vt)


def _ax_dispatch(qt, k, q8, k8, qnorm, kmax, vt, gain_max):
    rep = AX_HEADS // AX_KV_HEADS
    knorm = jnp.sqrt(kmax[:, 0, ::HEAD_DIM])
    shift_max = jnp.max(jnp.max(qnorm[:, :AX_HEADS], axis=-1) * jnp.repeat(knorm, rep, axis=1))
    entry_max = HEAD_DIM ** 0.5 * gain_max * jnp.array([FP8_Q_HI * Q_SCALE_EXP2, 1.0])
    safe = (shift_max <= AX_MAX_SHIFT) & jnp.all(entry_max <= F8_MAX / 2)
    return lax.cond(safe,
                    functools.partial(_ax_attention, bounded=True),
                    functools.partial(_ax_attention, bounded=False),
                    qt, k, q8, k8, qnorm, kmax, vt)


def _post_kernel(x_ref, ya_ref, yb_ref, yc_ref, gg_ref, wo_ref, gta_ref, gf_ref, scf_ref, shf_ref,
                 gtf_ref, wgu_ref, wd_ref, gfin_ref, o_ref, *, final):
    bf = jnp.bfloat16
    gg = gg_ref[...]
    y = jnp.concatenate([
        _rms(ya_ref[0], gg[:, 0:NA_W]).astype(bf),
        _rms(yb_ref[0], gg[:, NA_W:NA_W + SW_QW]).astype(bf),
        _rms(yc_ref[0], gg[:, NA_W + SW_QW:]).astype(bf)], axis=1)
    x1 = x_ref[0] + gta_ref[0] * jnp.dot(y, wo_ref[...], preferred_element_type=jnp.float32)

    h = (_rms(x1, gf_ref[...]) * (1.0 + scf_ref[0]) + shf_ref[0]).astype(bf)
    acc = jnp.zeros(x1.shape, jnp.float32)
    for c0 in range(0, FFN_HIDDEN, FFN_CHUNK):
        c1 = min(c0 + FFN_CHUNK, FFN_HIDDEN)
        gate = jnp.dot(h, wgu_ref[:, c0:c1], preferred_element_type=jnp.float32)
        up = jnp.dot(h, wgu_ref[:, FFN_HIDDEN + c0:FFN_HIDDEN + c1],
                     preferred_element_type=jnp.float32)
        act = (gate * (1.0 / (1.0 + jnp.exp(-gate))) * up).astype(bf)
        acc = acc + jnp.dot(act, wd_ref[c0:c1, :], preferred_element_type=jnp.float32)
    x2 = x1 + gtf_ref[0] * acc
    if final:
        x2 = _rms(x2, gfin_ref[...])
    o_ref[0] = x2


def _post(x, ya, yb, yc, gg, wo, gta, gf, scf, shf, gtf, wgu, wd, gfin, layer, final):
    B, S, D = x.shape
    tm = TOK_TILE
    tok = lambda width: pl.BlockSpec((1, tm, width), lambda b, i: (b, i, 0))
    vec = pl.BlockSpec((1, 1, D), lambda b, i: (b, 0, 0))
    return pl.pallas_call(
        functools.partial(_post_kernel, final=final),
        out_shape=jax.ShapeDtypeStruct((B, S, D), jnp.float32),
        grid=(B, S // tm),
        in_specs=[tok(D), tok(NA_W), tok(SW_QW), tok(AX_QW), _const_spec((1, D)),
                  _layer_spec(wo, layer), vec, _const_spec((1, D)), vec, vec, vec,
                  _layer_spec(wgu, layer), _layer_spec(wd, layer), _const_spec((1, D))],
        out_specs=tok(D),
        compiler_params=_cparams("arbitrary", "arbitrary"),
        name="out_proj_ffn",
    )(x, ya, yb, yc, gg, wo, gta, gf, scf, shf, gtf, wgu, wd, gfin)


def _rope_tables(S):
    t = jnp.arange(S)
    row = (t // GRID_W).astype(jnp.float32)
    col = (t % GRID_W).astype(jnp.float32)
    axis_dim = HEAD_DIM // 2
    freqs = ROPE_THETA ** (-jnp.arange(0, axis_dim, 2, dtype=jnp.float32) / axis_dim)
    ang = jnp.stack([row[:, None] * freqs, col[:, None] * freqs], axis=1)
    cos = jnp.cos(ang)
    sin = jnp.sin(ang)
    cos_h = jnp.concatenate([cos, cos], axis=-1).reshape(S, HEAD_DIM)
    sin_h = jnp.concatenate([-sin, sin], axis=-1).reshape(S, HEAD_DIM)
    return jnp.tile(cos_h, (1, LANES // HEAD_DIM)), jnp.tile(sin_h, (1, LANES // HEAD_DIM))


def _pow2_balance(gq, gk):
    log_rms = lambda g: 0.5 * jnp.log2(jnp.maximum(jnp.mean(g.astype(jnp.float32) ** 2), 1e-30))
    return jnp.exp2(jnp.round(0.5 * (log_rms(gk) - log_rms(gq))))


def _block_diag_ones(width):
    idx = np.arange(width) // HEAD_DIM
    return jnp.asarray(idx[:, None] == idx[None, :], dtype=jnp.bfloat16)


def _sw_relabel(t, start, axis):
    rep = SW_HEADS // SW_KV_HEADS
    seg = lax.slice_in_dim(t, start, start + SW_QW, axis=axis)
    shp = seg.shape
    seg = seg.reshape(shp[:axis] + (SW_KV_HEADS, rep, HEAD_DIM) + shp[axis + 1:])
    seg = jnp.swapaxes(seg, axis, axis + 1).reshape(shp)
    return jnp.concatenate([lax.slice_in_dim(t, 0, start, axis=axis), seg,
                            lax.slice_in_dim(t, start + SW_QW, t.shape[axis], axis=axis)], axis=axis)


def kernel(x, c, w_mod, b_mod, g_attn, w_in, rpb_na, sink_sw, t5_table, gq_ax, gk_ax, g_group,
           w_o, g_ffn, w_gu, w_down, g_final):
    B, S, D = x.shape
    L = w_mod.shape[0]
    bf = jnp.bfloat16

    c_pad = jnp.pad(c, ((0, 8 - B), (0, 0)))
    mod = _modulation(c_pad, w_mod, b_mod)[:, :B]
    mod = mod.reshape(L, B, 6, 1, D)

    cos, sin = _rope_tables(S)
    bd = _block_diag_ones(LANES)
    sw_bucket = _sw_bucket_tile()
    t5_flat = t5_table.astype(jnp.float32).reshape(-1)

    w_in_p = _sw_relabel(w_in, OFF_QB, axis=2).astype(bf)
    w_o_p = _sw_relabel(w_o, NA_W, axis=1).astype(bf)
    g_group_p = _sw_relabel(g_group, NA_W, axis=1)
    w_gu_b = w_gu.astype(bf)
    w_down_b = w_down.astype(bf)

    for l in range(L):
        sh_a, sc_a, gt_a, sh_f, sc_f, gt_f = [mod[l, :, i] for i in range(6)]
        balance = _pow2_balance(gq_ax[l], gk_ax[l])
        gq = gq_ax[l] * balance
        gk = gk_ax[l] / balance
        gain_max = jnp.stack([jnp.max(jnp.abs(gq)), jnp.max(jnp.abs(gk))])
        qa, ka, va, qb, kb, vb, qct, kc, vct, qnorm, kmax, q8, k8 = _in_proj(
            x, sc_a, sh_a, g_attn[l].reshape(1, D), w_in_p, l, cos, sin,
            jnp.tile(gq, LANES // HEAD_DIM).reshape(1, LANES),
            jnp.tile(gk, LANES // HEAD_DIM).reshape(1, LANES), bd)
        ya = _na_attention(qa, ka, va, _na_rpb_rows(rpb_na[l]))
        yb = _sw_attention(qb, kb, vb, sw_bucket, t5_flat, sink_sw[l])
        yc = _ax_dispatch(qct, kc, q8, k8, qnorm, kmax, vct, gain_max)
        x = _post(x, ya, yb, yc, g_group_p[l].reshape(1, D), w_o_p, gt_a, g_ffn[l].reshape(1, D),
                  sc_f, sh_f, gt_f, w_gu_b, w_down_b, g_final.reshape(1, D), layer=l,
                  final=(l == L - 1))
    return x
```

```python
import functools
import math

import jax
import jax.numpy as jnp
import numpy as np
from jax import lax
from jax.experimental import pallas as pl
from jax.experimental.pallas import tpu as pltpu

D_MODEL = 1024
HEAD_DIM = 64
GRID_W = 64
NA_HEADS = 4
SW_HEADS = 6
SW_KV_HEADS = 2
AX_HEADS = 6
AX_KV_HEADS = 2
NA_WIN_ROWS = 8
NA_WIN_COLS = 16
SW_RADIUS = 128
SW_BLOCK = 128
T5_BUCKETS = 32
T5_MAX_DIST = 128
ROPE_THETA = 10000.0
FFN_HIDDEN = 2816
EPS = 1e-6
NEG_INF = -1e30

NA_W = NA_HEADS * HEAD_DIM
SW_QW = SW_HEADS * HEAD_DIM
SW_KW = SW_KV_HEADS * HEAD_DIM
AX_QW = AX_HEADS * HEAD_DIM
AX_KW = AX_KV_HEADS * HEAD_DIM
IN_WIDTH = 3 * NA_W + SW_QW + 2 * SW_KW + AX_QW + 2 * AX_KW
OFF_QA, OFF_KA, OFF_VA = 0, NA_W, 2 * NA_W
OFF_QB = 3 * NA_W
OFF_KB = OFF_QB + SW_QW
OFF_VB = OFF_KB + SW_KW
OFF_QC = OFF_VB + SW_KW
OFF_KC = OFF_QC + AX_QW
OFF_VC = OFF_KC + AX_KW

LANES = 128
VMEM_LIMIT = 56 * 1024 * 1024

MOD_TILE = 1536
TOK_TILE = 512
IN_TILE = 2048
IN_SUB = 256
AX_TQ = 256
AX_TK = 512
AX_LOOKAHEAD = 2
AX_UNROLL = 8
AX_ONES_ROWS = 16
NA_QROWS = 4
NA_BAND = NA_QROWS + NA_WIN_ROWS
NA_TQ = 2048
NA_LOOKAHEAD = 2
NA_UNROLL = 2
SW_TQ = 2048
SW_LOOKAHEAD = 2
SW_UNROLL = 4
FFN_CHUNK = 512

QK_SCALE = HEAD_DIM ** -0.5
F8 = jnp.float8_e4m3fn
F8_MAX = 448.0
FP8_Q_HI, FP8_K_HI = 2.0, 0.5
FP8_Q_LO, FP8_K_LO = 16.0, 1.0 / 16.0
LOG2E = math.log2(math.e)
Q_SCALE_EXP2 = QK_SCALE * LOG2E
AX_MAX_SHIFT = 60.0


def _cparams(*sem):
    return pltpu.CompilerParams(dimension_semantics=sem, vmem_limit_bytes=VMEM_LIMIT)


def _const_spec(shape):
    n = len(shape)
    return pl.BlockSpec(shape, lambda *_: (0,) * n, pipeline_mode=pl.Buffered(1))


def _layer_spec(stacked, layer):
    n = stacked.ndim - 1
    return pl.BlockSpec((None,) + stacked.shape[1:], lambda *_: (layer,) + (0,) * n,
                        pipeline_mode=pl.Buffered(1))


def _rms(x, g):
    return x * lax.rsqrt(jnp.mean(x * x, axis=-1, keepdims=True) + EPS) * g


def _mod_kernel(c_ref, w_ref, b_ref, o_ref):
    c = c_ref[...]
    cond = c * (1.0 / (1.0 + jnp.exp(-c)))
    o_ref[0] = jnp.dot(cond, w_ref[0], preferred_element_type=jnp.float32,
                       precision=lax.Precision.HIGHEST) + b_ref[0]


def _modulation(c_pad, w_mod, b_mod):
    L, D, N = w_mod.shape
    tn = MOD_TILE
    return pl.pallas_call(
        _mod_kernel,
        out_shape=jax.ShapeDtypeStruct((L, c_pad.shape[0], N), jnp.float32),
        grid=(L, N // tn),
        in_specs=[pl.BlockSpec(c_pad.shape, lambda l, j: (0, 0)),
                  pl.BlockSpec((1, D, tn), lambda l, j: (l, 0, j)),
                  pl.BlockSpec((1, 1, tn), lambda l, j: (l, 0, j))],
        out_specs=pl.BlockSpec((1, c_pad.shape[0], tn), lambda l, j: (l, 0, j)),
        compiler_params=_cparams("arbitrary", "arbitrary"),
        name="adaln_mod",
    )(c_pad, w_mod, b_mod.reshape(L, 1, N))


def _head_sumsq(t, bd):
    t2 = (t * t).astype(jnp.bfloat16)
    return jnp.concatenate(
        [jnp.dot(t2[:, c:c + LANES], bd, preferred_element_type=jnp.float32)
         for c in range(0, t.shape[1], LANES)], axis=1)


def _rope_chunk(t, cos, sin_signed, first_half):
    swapped = jnp.where(first_half, pltpu.roll(t, LANES - 16, 1), pltpu.roll(t, 16, 1))
    return t * cos + swapped * sin_signed


def _in_kernel(x_ref, sc_ref, sh_ref, g_ref, w_ref, cos_ref, sin_ref, gq_ref, gk_ref,
               bd_ref,
               qa_ref, ka_ref, vat_ref, qb_ref, kb_ref, vbt_ref, qct_ref, kc_ref, vct_ref,
               qnorm_ref, kmax_ref, q8_ref, k8_ref):
    bf = jnp.bfloat16
    n_sub = x_ref.shape[1] // IN_SUB
    lane = lax.broadcasted_iota(jnp.int32, (IN_SUB, LANES), 1)
    first_half = (lane % 32) < 16

    def normed(j):
        x = x_ref[0, j * IN_SUB:(j + 1) * IN_SUB, :]
        return (_rms(x, g_ref[...]) * (1.0 + sc_ref[0]) + sh_ref[0]).astype(bf)

    def project(h):
        return jnp.dot(h, w_ref[...], preferred_element_type=jnp.float32)

    def head_stats(proj):
        return (_head_sumsq(proj[:, OFF_QC:OFF_QC + AX_QW], bd_ref[...]),
                _head_sumsq(proj[:, OFF_KC:OFF_KC + AX_KW], bd_ref[...]))

    def finish(j, proj, stats):
        rows = slice(j * IN_SUB, (j + 1) * IN_SUB)
        qa_ref[0, rows] = (proj[:, OFF_QA:OFF_QA + NA_W] * Q_SCALE_EXP2).astype(bf)
        ka_ref[0, rows] = proj[:, OFF_KA:OFF_KA + NA_W].astype(bf)
        qb_ref[0, rows] = (proj[:, OFF_QB:OFF_QB + SW_QW] * Q_SCALE_EXP2).astype(bf)
        kb_ref[0, rows] = proj[:, OFF_KB:OFF_KB + SW_KW].astype(bf)
        per_sub = IN_SUB // LANES
        for t in range(per_sub):
            blk = slice(t * LANES, (t + 1) * LANES)
            vat_ref[0, j * per_sub + t] = proj[blk, OFF_VA:OFF_VA + NA_W].T.astype(bf)
            vbt_ref[0, j * per_sub + t] = proj[blk, OFF_VB:OFF_VB + SW_KW].T.astype(bf)

        cos = cos_ref[rows, :]
        sin = sin_ref[rows, :]
        qss, kss = stats
        qn = proj[:, OFF_QC:OFF_QC + AX_QW] * lax.rsqrt(qss * (1.0 / HEAD_DIM) + EPS)
        chunks = []
        for c in range(AX_QW // LANES):
            t = qn[:, c * LANES:(c + 1) * LANES] * gq_ref[...]
            chunks.append(_rope_chunk(t, cos, sin, first_half) * Q_SCALE_EXP2)
        f32 = jnp.float32
        qt = jnp.concatenate(chunks, axis=1).T
        qct_ref[0, :, rows] = qt.astype(bf)
        a1 = (qt * FP8_Q_HI).astype(F8).astype(f32)
        a2 = ((qt - a1 * (1.0 / FP8_Q_HI)) * FP8_Q_LO).astype(F8).astype(f32)
        a3 = qt.astype(F8).astype(f32)
        q8_ref[0, 0, :, rows] = a1.astype(F8)
        q8_ref[0, 1, :, rows] = a2.astype(F8)
        q8_ref[0, 2, :, rows] = a3.astype(F8)
        qsq = a1 * a1 + a2 * a2 + a3 * a3
        norms = [jnp.sqrt(jnp.sum(qsq[hh * HEAD_DIM:(hh + 1) * HEAD_DIM], axis=0, keepdims=True))
                 for hh in range(AX_HEADS)]
        qnorm_ref[0, :, rows] = jnp.concatenate(
            norms + [jnp.zeros_like(norms[0])] * (8 - AX_HEADS), axis=0)

        kn = (proj[:, OFF_KC:OFF_KC + AX_KW] * lax.rsqrt(kss * (1.0 / HEAD_DIM) + EPS)
              * gk_ref[...])
        kr = _rope_chunk(kn, cos, sin, first_half)
        kb = kr.astype(bf)
        zero_bf = jnp.zeros_like(kb)
        kc_ref[0, 0, rows] = jnp.where(lane < HEAD_DIM, kb, zero_bf)
        kc_ref[0, 1, rows] = jnp.where(lane < HEAD_DIM, pltpu.roll(kr, HEAD_DIM, 1).astype(bf), zero_bf)
        x1 = kr * FP8_K_HI
        x2 = kr * FP8_K_LO
        x3 = kr - x1.astype(F8).astype(f32) * (1.0 / FP8_K_HI)
        ones_f = jnp.where(lane == HEAD_DIM, 1.0, 0.0)
        norms2 = []
        for g in range(AX_KV_HEADS):
            if g == 0:
                c0 = jnp.where(lane < HEAD_DIM, x1, pltpu.roll(x2, HEAD_DIM, 1))
                c1 = jnp.where(lane < HEAD_DIM, x3, ones_f)
            else:
                c0 = jnp.where(lane < HEAD_DIM, pltpu.roll(x1, HEAD_DIM, 1), x2)
                c1 = jnp.where(lane < HEAD_DIM, pltpu.roll(x3, HEAD_DIM, 1), ones_f)
            c0 = c0.astype(F8)
            c1 = c1.astype(F8)
            k8_ref[0, g, rows] = jnp.concatenate([c0, c1], axis=1)
            c0f = c0.astype(f32)
            c1f = jnp.where(lane < HEAD_DIM, c1.astype(f32), 0.0)
            norms2.append(jnp.max(jnp.sum(c0f * c0f + c1f * c1f, axis=1, keepdims=True),
                                  axis=0, keepdims=True))
        first_tk = (j * IN_SUB) // AX_TK
        off = (j * IN_SUB) % AX_TK
        vct_ref[0, first_tk, :, off:off + IN_SUB] = proj[:, OFF_VC:OFF_VC + AX_KW].T.astype(bf)
        return jnp.where(lane[:1] < HEAD_DIM, norms2[0], norms2[1])

    proj = project(normed(0))
    tile_max = None
    for j in range(n_sub):
        if j + 1 < n_sub:
            h_next = normed(j + 1)
        stats = head_stats(proj)
        if j + 1 < n_sub:
            proj_next = project(h_next)
        sub_max = finish(j, proj, stats)
        tile_max = sub_max if tile_max is None else jnp.maximum(tile_max, sub_max)
        if j + 1 < n_sub:
            proj = proj_next

    first = pl.program_id(1) == 0

    @pl.when(first)
    def _():
        kmax_ref[0] = tile_max

    @pl.when(jnp.logical_not(first))
    def _():
        kmax_ref[0] = jnp.maximum(kmax_ref[0], tile_max)


def _in_proj(x, sc, sh, g, w, layer, cos, sin, gq, gk, bd):
    B, S, D = x.shape
    tm = IN_TILE
    bf = jnp.bfloat16
    tok = lambda width: pl.BlockSpec((1, tm, width), lambda b, i: (b, i, 0))
    vec = pl.BlockSpec((1, 1, D), lambda b, i: (b, 0, 0))
    out_shape = (
        jax.ShapeDtypeStruct((B, S, NA_W), bf), jax.ShapeDtypeStruct((B, S, NA_W), bf),
        jax.ShapeDtypeStruct((B, S // LANES, NA_W, LANES), bf),
        jax.ShapeDtypeStruct((B, S, SW_QW), bf), jax.ShapeDtypeStruct((B, S, SW_KW), bf),
        jax.ShapeDtypeStruct((B, S // SW_BLOCK, SW_KW, SW_BLOCK), bf),
        jax.ShapeDtypeStruct((B, AX_QW, S), bf),
        jax.ShapeDtypeStruct((B, AX_KV_HEADS, S, LANES), bf),
        jax.ShapeDtypeStruct((B, S // AX_TK, AX_KW, AX_TK), bf),
        jax.ShapeDtypeStruct((B, 8, S), jnp.float32),
        jax.ShapeDtypeStruct((B, 1, AX_KW), jnp.float32),
        jax.ShapeDtypeStruct((B, 3, AX_QW, S), F8),
        jax.ShapeDtypeStruct((B, AX_KV_HEADS, S, 2 * LANES), F8),
    )
    out_specs = (
        tok(NA_W), tok(NA_W),
        pl.BlockSpec((1, tm // LANES, NA_W, LANES), lambda b, i: (b, i, 0, 0)),
        tok(SW_QW), tok(SW_KW),
        pl.BlockSpec((1, tm // SW_BLOCK, SW_KW, SW_BLOCK), lambda b, i: (b, i, 0, 0)),
        pl.BlockSpec((1, AX_QW, tm), lambda b, i: (b, 0, i)),
        pl.BlockSpec((1, AX_KV_HEADS, tm, LANES), lambda b, i: (b, 0, i, 0)),
        pl.BlockSpec((1, tm // AX_TK, AX_KW, AX_TK), lambda b, i: (b, i, 0, 0)),
        pl.BlockSpec((1, 8, tm), lambda b, i: (b, 0, i)),
        pl.BlockSpec((1, 1, AX_KW), lambda b, i: (b, 0, 0)),
        pl.BlockSpec((1, 3, AX_QW, tm), lambda b, i: (b, 0, 0, i)),
        pl.BlockSpec((1, AX_KV_HEADS, tm, 2 * LANES), lambda b, i: (b, 0, i, 0)),
    )
    return pl.pallas_call(
        _in_kernel,
        out_shape=out_shape,
        grid=(B, S // tm),
        in_specs=[tok(D), vec, vec, _const_spec((1, D)), _layer_spec(w, layer),
                  pl.BlockSpec((tm, LANES), lambda b, i: (i, 0)),
                  pl.BlockSpec((tm, LANES), lambda b, i: (i, 0)),
                  _const_spec((1, LANES)), _const_spec((1, LANES)),
                  _const_spec(bd.shape)],
        out_specs=out_specs,
        compiler_params=_cparams("arbitrary", "arbitrary"),
        name="in_proj",
    )(x, sc, sh, g, w, cos, sin, gq, gk, bd)


def _na_build_bias(rp_ref, bias_ref, rows):
    shape = (GRID_W, LANES)
    kc = lax.broadcasted_iota(jnp.int32, shape, 0)
    lane = lax.broadcasted_iota(jnp.int32, shape, 1)
    c = lane % GRID_W
    cs = jnp.clip(c - NA_WIN_COLS // 2, 0, GRID_W - NA_WIN_COLS)
    col_ok = (kc >= cs) & (kc < cs + NA_WIN_COLS)
    left = lane < GRID_W
    neg = jnp.full(shape, NEG_INF, jnp.float32)
    tq = NA_QROWS * GRID_W
    for h in range(NA_HEADS):
        pair = []
        for a in range(2 * NA_WIN_ROWS):
            x = jnp.broadcast_to(rp_ref[h, a:a + 1, :] * LOG2E, shape)
            t = pltpu.roll(x, LANES - (NA_WIN_COLS - 1), 1, stride=1, stride_axis=0)
            pair.append(jnp.where(col_ok, t, neg))
        for variant, qr0 in enumerate((0, 2 * NA_QROWS, rows - NA_QROWS)):
            bs = int(np.clip(qr0 - NA_WIN_ROWS // 2, 0, rows - NA_BAND))
            for kj in range(NA_BAND):
                kr = bs + kj
                for u in range(NA_QROWS // 2):
                    r0 = qr0 + 2 * u
                    ok = [int(np.clip(r - NA_WIN_ROWS // 2, 0, rows - NA_WIN_ROWS)) <= kr
                          < int(np.clip(r - NA_WIN_ROWS // 2, 0, rows - NA_WIN_ROWS)) + NA_WIN_ROWS
                          for r in (r0, r0 + 1)]
                    a0 = kr - r0 + NA_WIN_ROWS - 1
                    if ok[0] and ok[1]:
                        tile = pair[a0]
                    elif ok[0]:
                        tile = jnp.where(left, pair[a0], neg)
                    elif ok[1]:
                        tile = jnp.where(left, neg, pair[a0])
                    else:
                        tile = neg
                    bias_ref[variant, kj * GRID_W:(kj + 1) * GRID_W,
                             h * tq + u * LANES:h * tq + (u + 1) * LANES] = tile


def _na_kernel(q_ref, k_ref, vt_ref, rp_ref, o_ref, bias_ref, s_ref, *, rows):
    i = pl.program_id(1)
    tq = NA_QROWS * GRID_W
    nblk = NA_TQ // tq
    nblk_total = rows // NA_QROWS
    bf = jnp.bfloat16

    @pl.when((pl.program_id(0) == 0) & (i == 0))
    def _():
        _na_build_bias(rp_ref, bias_ref, rows)

    head_of_lane = lax.broadcasted_iota(jnp.int32, (tq, NA_W), 1) // HEAD_DIM
    ones = jnp.ones((16, NA_BAND * GRID_W), bf)

    def band_row(blk):
        return jnp.clip((i * nblk + blk) * NA_QROWS - NA_WIN_ROWS // 2, 0, rows - NA_BAND)

    def scores(blk):
        q = q_ref[0, pl.ds(pl.multiple_of(blk * tq, tq), tq), :]
        stack = jnp.concatenate([jnp.where(head_of_lane == h, q, jnp.zeros_like(q))
                                 for h in range(NA_HEADS)], axis=0)
        start = pl.multiple_of(band_row(blk) * GRID_W, NA_QROWS * GRID_W)
        kb = k_ref[0, pl.ds(start, NA_BAND * GRID_W), :]
        return lax.dot_general(kb, stack, (((1,), (1,)), ((), ())),
                               preferred_element_type=jnp.float32)

    for t in range(NA_LOOKAHEAD):
        s_ref[t] = scores(t)

    def body(trip, carry):
        for t in range(NA_LOOKAHEAD):
            blk = trip * NA_LOOKAHEAD + t
            s = s_ref[t]
            s_ref[t] = scores(jnp.minimum(blk + NA_LOOKAHEAD, nblk - 1))
            n = i * nblk + blk
            variant = jnp.where(n == 0, 0, jnp.where(n == nblk_total - 1, 2, 1))
            s = s + bias_ref[variant]
            m = jnp.max(s, axis=0, keepdims=True)
            p = jnp.exp2(s - m).astype(bf)
            first = band_row(blk) * GRID_W // LANES
            vt = jnp.concatenate([vt_ref[0, first + u] for u in range(NA_BAND * GRID_W // LANES)],
                                 axis=1)
            outs = []
            for h in range(NA_HEADS):
                vt_ext = jnp.concatenate([vt[h * HEAD_DIM:(h + 1) * HEAD_DIM], ones], axis=0)
                o = jnp.dot(vt_ext, p[:, h * tq:(h + 1) * tq], preferred_element_type=jnp.float32)
                outs.append(o[:HEAD_DIM] / o[HEAD_DIM:HEAD_DIM + 1])
            o_ref[0, pl.ds(pl.multiple_of(blk * tq, tq), tq), :] = jnp.concatenate(outs, axis=0).T
        return carry

    lax.fori_loop(0, nblk // NA_LOOKAHEAD, body, 0, unroll=NA_UNROLL)


def _na_attention(q, k, vt, rp):
    B, S, W = q.shape
    rows = S // GRID_W
    tq = NA_QROWS * GRID_W
    return pl.pallas_call(
        functools.partial(_na_kernel, rows=rows),
        out_shape=jax.ShapeDtypeStruct((B, S, W), jnp.float32),
        grid=(B, S // NA_TQ),
        in_specs=[pl.BlockSpec((1, NA_TQ, W), lambda b, j: (b, j, 0)),
                  pl.BlockSpec((1, S, W), lambda b, j: (b, 0, 0), pipeline_mode=pl.Buffered(1)),
                  pl.BlockSpec((1, S // LANES, W, LANES), lambda b, j: (b, 0, 0, 0),
                               pipeline_mode=pl.Buffered(1)),
                  _const_spec(rp.shape)],
        out_specs=pl.BlockSpec((1, NA_TQ, W), lambda b, j: (b, j, 0)),
        scratch_shapes=[pltpu.VMEM((3, NA_BAND * GRID_W, NA_HEADS * tq), jnp.float32),
                        pltpu.VMEM((NA_LOOKAHEAD, NA_BAND * GRID_W, NA_HEADS * tq), jnp.float32)],
        compiler_params=_cparams("arbitrary", "arbitrary"),
        name="na_attn",
    )(q, k, vt, rp)


def _na_rpb_rows(rpb):
    n_col = rpb.shape[-1]
    p = jnp.pad(rpb.astype(jnp.float32)[:, :, ::-1], ((0, 0), (1, 1), (0, GRID_W - n_col)))
    return jnp.concatenate([p[:, 1:], p[:, :-1]], axis=-1)


def _sw_kernel(sink_ref, t5_ref, q_ref, k_ref, vt_ref, bucket_ref, o_ref, bias_ref, s_ref, *, seq):
    i = pl.program_id(1)
    nblk_total = seq // SW_BLOCK

    @pl.when((pl.program_id(0) == 0) & (i == 0))
    def _():
        bucket = bucket_ref[...]
        key_row = lax.broadcasted_iota(jnp.int32, bucket.shape, 0)
        for h in range(SW_HEADS):
            acc = jnp.full(bucket.shape, NEG_INF, jnp.float32)
            for b in range(T5_BUCKETS):
                acc = jnp.where(bucket == b, t5_ref[b * SW_HEADS + h] * LOG2E, acc)
            cols = slice(h * SW_BLOCK, (h + 1) * SW_BLOCK)
            bias_ref[0, :, cols] = acc
            bias_ref[1, :, cols] = jnp.where(key_row < SW_BLOCK, NEG_INF, acc)
            bias_ref[2, :, cols] = jnp.where(key_row >= 2 * SW_BLOCK, NEG_INF, acc)

    nblk = SW_TQ // SW_BLOCK
    rep = SW_HEADS // SW_KV_HEADS
    bf = jnp.bfloat16
    lane_group = lax.broadcasted_iota(jnp.int32, (SW_BLOCK, LANES), 1) // HEAD_DIM
    row_group = lax.broadcasted_iota(jnp.int32, (SW_BLOCK, LANES), 0) // HEAD_DIM
    sink = jnp.concatenate([jnp.full((1, SW_BLOCK), sink_ref[h] * LOG2E, jnp.float32)
                            for h in range(SW_HEADS)], axis=1)
    ones = jnp.ones((16, 3 * SW_BLOCK), bf)

    def neighbours(blk):
        n = i * nblk + blk
        return jnp.maximum(n - 1, 0), n, jnp.minimum(n + 1, nblk_total - 1)

    def scores(blk):
        rows = pl.ds(pl.multiple_of(blk * SW_BLOCK, SW_BLOCK), SW_BLOCK)
        stack = []
        for g in range(SW_KV_HEADS):
            for r in range(rep):
                qcol = q_ref[0, rows, r * LANES:(r + 1) * LANES]
                stack.append(jnp.where(lane_group == g, qcol, jnp.zeros_like(qcol)))
        kw = jnp.concatenate([k_ref[0, pl.ds(pl.multiple_of(nb * SW_BLOCK, SW_BLOCK), SW_BLOCK), :]
                              for nb in neighbours(blk)], axis=0)
        return lax.dot_general(kw, jnp.concatenate(stack, axis=0), (((1,), (1,)), ((), ())),
                               preferred_element_type=jnp.float32)

    for t in range(SW_LOOKAHEAD):
        s_ref[t] = scores(t)

    def body(trip, carry):
        for t in range(SW_LOOKAHEAD):
            blk = trip * SW_LOOKAHEAD + t
            s = s_ref[t]
            s_ref[t] = scores(jnp.minimum(blk + SW_LOOKAHEAD, nblk - 1))
            left, n, right = neighbours(blk)
            edge = jnp.where(n == 0, 1, jnp.where(n == nblk_total - 1, 2, 0))
            s = s + bias_ref[edge]
            m = jnp.maximum(jnp.max(s, axis=0, keepdims=True), sink)
            p = jnp.exp2(s - m).astype(bf)
            vt_ext = jnp.concatenate(
                [jnp.concatenate([vt_ref[0, left], vt_ref[0, n], vt_ref[0, right]], axis=1), ones],
                axis=0)
            o = jnp.dot(vt_ext, p, preferred_element_type=jnp.float32)
            res = o[:LANES] / (o[LANES:LANES + 1] + jnp.exp2(sink - m))
            rows = pl.ds(pl.multiple_of(blk * SW_BLOCK, SW_BLOCK), SW_BLOCK)
            for r in range(rep):
                yt = jnp.where(row_group == 0, res[:, r * SW_BLOCK:(r + 1) * SW_BLOCK],
                               res[:, (rep + r) * SW_BLOCK:(rep + r + 1) * SW_BLOCK])
                o_ref[0, rows, r * LANES:(r + 1) * LANES] = yt.T
        return carry

    lax.fori_loop(0, nblk // SW_LOOKAHEAD, body, 0, unroll=SW_UNROLL)


def _sw_attention(q, k, vt, bucket, t5_flat, sink):
    B, S, _ = q.shape
    nb = S // SW_BLOCK
    return pl.pallas_call(
        functools.partial(_sw_kernel, seq=S),
        out_shape=jax.ShapeDtypeStruct((B, S, SW_QW), jnp.float32),
        grid_spec=pltpu.PrefetchScalarGridSpec(
            num_scalar_prefetch=2,
            grid=(B, S // SW_TQ),
            in_specs=[pl.BlockSpec((1, SW_TQ, SW_QW), lambda b, i, *_: (b, i, 0)),
                      pl.BlockSpec((1, S, SW_KW), lambda b, i, *_: (b, 0, 0),
                                   pipeline_mode=pl.Buffered(1)),
                      pl.BlockSpec((1, nb, SW_KW, SW_BLOCK), lambda b, i, *_: (b, 0, 0, 0),
                                   pipeline_mode=pl.Buffered(1)),
                      pl.BlockSpec(bucket.shape, lambda b, i, *_: (0, 0),
                                   pipeline_mode=pl.Buffered(1))],
            out_specs=pl.BlockSpec((1, SW_TQ, SW_QW), lambda b, i, *_: (b, i, 0)),
            scratch_shapes=[pltpu.VMEM((3, 3 * SW_BLOCK, SW_HEADS * SW_BLOCK), jnp.float32),
                            pltpu.VMEM((SW_LOOKAHEAD, 3 * SW_BLOCK, SW_HEADS * SW_BLOCK),
                                       jnp.float32)],
        ),
        compiler_params=_cparams("arbitrary", "arbitrary"),
        name="sw_attn",
    )(sink, t5_flat, q, k, vt, bucket)


def _t5_bucket(rel):
    nb = T5_BUCKETS // 2
    ret = (rel > 0).astype(jnp.int32) * nb
    n = jnp.abs(rel)
    max_exact = nb // 2
    nf = jnp.maximum(n, max_exact).astype(jnp.float32)
    large = max_exact + (jnp.log(nf / max_exact) / math.log(T5_MAX_DIST / max_exact)
                         * (nb - max_exact)).astype(jnp.int32)
    large = jnp.minimum(large, nb - 1)
    return ret + jnp.where(n < max_exact, n, large)


def _sw_bucket_tile():
    qpos = jnp.arange(SW_BLOCK)
    kpos = jnp.arange(3 * SW_BLOCK) - SW_BLOCK
    rel = kpos[:, None] - qpos[None, :]
    return jnp.where(jnp.abs(rel) <= SW_RADIUS, _t5_bucket(rel), -1).astype(jnp.int32)


def _ax_kernel(q_ref, qnorm_ref, kmax_ref, k_ref, vt_ref, o_ref, qpad_ref, m_ref, l_ref,
               acc_ref, s0_ref, *, nkt, bounded):
    tq = q_ref.shape[-1]
    tk = AX_TK
    rep = AX_HEADS // AX_KV_HEADS
    bf = jnp.bfloat16
    for h in range(AX_HEADS):
        g = h // rep
        head = slice(h * HEAD_DIM, (h + 1) * HEAD_DIM)
        if bounded:
            row = lax.broadcasted_iota(jnp.int32, (32, tq), 0)
            shift = qnorm_ref[0, h:h + 1, :] * jnp.sqrt(kmax_ref[0, :, g * HEAD_DIM:g * HEAD_DIM + 1])
            extra = jnp.where(row == 0, -shift, 0.0).astype(F8)
            qpad_ref[h] = jnp.concatenate(
                [q_ref[0, 0, head, :], q_ref[0, 1, head, :], q_ref[0, 2, head, :], extra,
                 jnp.zeros((32, tq), F8)], axis=0)
        else:
            qpad_ref[h] = jnp.concatenate(
                [q_ref[0, head, :], jnp.zeros((LANES - HEAD_DIM, tq), bf)], axis=0)
    if not bounded:
        m_ref[...] = jnp.full(m_ref.shape, -jnp.inf, jnp.float32)
        l_ref[...] = jnp.zeros(l_ref.shape, jnp.float32)
    acc_ref[...] = jnp.zeros(acc_ref.shape, jnp.float32)

    def score(kt, h):
        ks = pl.multiple_of(kt * tk, tk)
        return jnp.dot(k_ref[0, h // rep, pl.ds(ks, tk), :], qpad_ref[h],
                       preferred_element_type=jnp.float32)

    for h in range(AX_LOOKAHEAD):
        s0_ref[h] = score(0, h)

    def body(kt, carry):
        pending = [s0_ref[h] for h in range(AX_LOOKAHEAD)]
        for h in range(AX_HEADS):
            g = h // rep
            s = pending.pop(0)
            ahead = h + AX_LOOKAHEAD
            if ahead < AX_HEADS:
                pending.append(score(kt, ahead))
            else:
                s0_ref[ahead - AX_HEADS] = score(jnp.minimum(kt + 1, nkt - 1), ahead - AX_HEADS)
            vt = vt_ref[0, kt, g * HEAD_DIM:(g + 1) * HEAD_DIM, :]
            if bounded:
                p = jnp.exp2(s)
                vt_ext = jnp.concatenate([vt, jnp.ones((AX_ONES_ROWS, tk), bf)], axis=0)
                acc_ref[h] += jnp.dot(vt_ext, p.astype(bf), preferred_element_type=jnp.float32)
            else:
                m_old = m_ref[h]
                m_new = jnp.maximum(m_old, jnp.max(s, axis=0, keepdims=True))
                alpha = jnp.exp2(m_old - m_new)
                p = jnp.exp2(s - m_new)
                l_ref[h] = alpha * l_ref[h] + jnp.sum(p.reshape(tk // 8, 8, tq), axis=0)
                m_ref[h] = m_new
                pv = jnp.dot(vt, p.astype(bf), preferred_element_type=jnp.float32)
                acc_ref[h, :HEAD_DIM] = alpha * acc_ref[h, :HEAD_DIM] + pv
        return carry

    lax.fori_loop(0, nkt, body, 0, unroll=AX_UNROLL)
    if bounded:
        denom = [acc_ref[h, HEAD_DIM:HEAD_DIM + 1] for h in range(AX_HEADS)]
    else:
        denom = [jnp.sum(l_ref[h], axis=0, keepdims=True) for h in range(AX_HEADS)]
    out_t = jnp.concatenate([acc_ref[h, :HEAD_DIM] / denom[h]
                             for h in range(AX_HEADS)], axis=0)
    o_ref[0] = out_t.T


def _ax_attention(qt, k, q8, k8, qnorm, kmax, vt, bounded):
    B, W, S = qt.shape
    nkt = vt.shape[1]
    if bounded:
        q, keys = q8, k8
        q_spec = pl.BlockSpec((1, 3, W, AX_TQ), lambda b, i: (b, 0, 0, i))
        qpad = pltpu.VMEM((AX_HEADS, 2 * LANES, AX_TQ), F8)
    else:
        q, keys = qt, k
        q_spec = pl.BlockSpec((1, W, AX_TQ), lambda b, i: (b, 0, i))
        qpad = pltpu.VMEM((AX_HEADS, LANES, AX_TQ), jnp.bfloat16)
    return pl.pallas_call(
        functools.partial(_ax_kernel, nkt=nkt, bounded=bounded),
        out_shape=jax.ShapeDtypeStruct((B, S, W), jnp.float32),
        grid=(B, S // AX_TQ),
        in_specs=[q_spec,
                  pl.BlockSpec((1, 8, AX_TQ), lambda b, i: (b, 0, i)),
                  pl.BlockSpec((1, 1, AX_KW), lambda b, i: (b, 0, 0)),
                  pl.BlockSpec((1,) + keys.shape[1:], lambda b, i: (b, 0, 0, 0),
                               pipeline_mode=pl.Buffered(1)),
                  pl.BlockSpec((1, nkt, AX_KW, AX_TK), lambda b, i: (b, 0, 0, 0),
                               pipeline_mode=pl.Buffered(1))],
        out_specs=pl.BlockSpec((1, AX_TQ, W), lambda b, i: (b, i, 0)),
        scratch_shapes=[qpad,
                        pltpu.VMEM((AX_HEADS, 1, AX_TQ), jnp.float32),
                        pltpu.VMEM((AX_HEADS, 8, AX_TQ), jnp.float32),
                        pltpu.VMEM((AX_HEADS, HEAD_DIM + AX_ONES_ROWS, AX_TQ), jnp.float32),
                        pltpu.VMEM((AX_LOOKAHEAD, AX_TK, AX_TQ), jnp.float32)],
        compiler_params=_cparams("arbitrary", "arbitrary"),
        name="ax_attn_bounded" if bounded else "ax_attn_online",
    )(q, qnorm, kmax, keys, vt)


def _ax_dispatch(qt, k, q8, k8, qnorm, kmax, vt, gain_max):
    rep = AX_HEADS // AX_KV_HEADS
    knorm = jnp.sqrt(kmax[:, 0, ::HEAD_DIM])
    shift_max = jnp.max(jnp.max(qnorm[:, :AX_HEADS], axis=-1) * jnp.repeat(knorm, rep, axis=1))
    entry_max = HEAD_DIM ** 0.5 * gain_max * jnp.array([FP8_Q_HI * Q_SCALE_EXP2, 1.0])
    safe = (shift_max <= AX_MAX_SHIFT) & jnp.all(entry_max <= F8_MAX / 2)
    return lax.cond(safe,
                    functools.partial(_ax_attention, bounded=True),
                    functools.partial(_ax_attention, bounded=False),
                    qt, k, q8, k8, qnorm, kmax, vt)


def _post_kernel(x_ref, ya_ref, yb_ref, yc_ref, gg_ref, wo_ref, gta_ref, gf_ref, scf_ref, shf_ref,
                 gtf_ref, wgu_ref, wd_ref, gfin_ref, o_ref, *, final):
    bf = jnp.bfloat16
    gg = gg_ref[...]
    y = jnp.concatenate([
        _rms(ya_ref[0], gg[:, 0:NA_W]).astype(bf),
        _rms(yb_ref[0], gg[:, NA_W:NA_W + SW_QW]).astype(bf),
        _rms(yc_ref[0], gg[:, NA_W + SW_QW:]).astype(bf)], axis=1)
    x1 = x_ref[0] + gta_ref[0] * jnp.dot(y, wo_ref[...], preferred_element_type=jnp.float32)

    h = (_rms(x1, gf_ref[...]) * (1.0 + scf_ref[0]) + shf_ref[0]).astype(bf)
    acc = jnp.zeros(x1.shape, jnp.float32)
    for c0 in range(0, FFN_HIDDEN, FFN_CHUNK):
        c1 = min(c0 + FFN_CHUNK, FFN_HIDDEN)
        gate = jnp.dot(h, wgu_ref[:, c0:c1], preferred_element_type=jnp.float32)
        up = jnp.dot(h, wgu_ref[:, FFN_HIDDEN + c0:FFN_HIDDEN + c1],
                     preferred_element_type=jnp.float32)
        act = (gate * (1.0 / (1.0 + jnp.exp(-gate))) * up).astype(bf)
        acc = acc + jnp.dot(act, wd_ref[c0:c1, :], preferred_element_type=jnp.float32)
    x2 = x1 + gtf_ref[0] * acc
    if final:
        x2 = _rms(x2, gfin_ref[...])
    o_ref[0] = x2


def _post(x, ya, yb, yc, gg, wo, gta, gf, scf, shf, gtf, wgu, wd, gfin, layer, final):
    B, S, D = x.shape
    tm = TOK_TILE
    tok = lambda width: pl.BlockSpec((1, tm, width), lambda b, i: (b, i, 0))
    vec = pl.BlockSpec((1, 1, D), lambda b, i: (b, 0, 0))
    return pl.pallas_call(
        functools.partial(_post_kernel, final=final),
        out_shape=jax.ShapeDtypeStruct((B, S, D), jnp.float32),
        grid=(B, S // tm),
        in_specs=[tok(D), tok(NA_W), tok(SW_QW), tok(AX_QW), _const_spec((1, D)),
                  _layer_spec(wo, layer), vec, _const_spec((1, D)), vec, vec, vec,
                  _layer_spec(wgu, layer), _layer_spec(wd, layer), _const_spec((1, D))],
        out_specs=tok(D),
        compiler_params=_cparams("arbitrary", "arbitrary"),
        name="out_proj_ffn",
    )(x, ya, yb, yc, gg, wo, gta, gf, scf, shf, gtf, wgu, wd, gfin)


def _rope_tables(S):
    t = jnp.arange(S)
    row = (t // GRID_W).astype(jnp.float32)
    col = (t % GRID_W).astype(jnp.float32)
    axis_dim = HEAD_DIM // 2
    freqs = ROPE_THETA ** (-jnp.arange(0, axis_dim, 2, dtype=jnp.float32) / axis_dim)
    ang = jnp.stack([row[:, None] * freqs, col[:, None] * freqs], axis=1)
    cos = jnp.cos(ang)
    sin = jnp.sin(ang)
    cos_h = jnp.concatenate([cos, cos], axis=-1).reshape(S, HEAD_DIM)
    sin_h = jnp.concatenate([-sin, sin], axis=-1).reshape(S, HEAD_DIM)
    return jnp.tile(cos_h, (1, LANES // HEAD_DIM)), jnp.tile(sin_h, (1, LANES // HEAD_DIM))


def _pow2_balance(gq, gk):
    log_rms = lambda g: 0.5 * jnp.log2(jnp.maximum(jnp.mean(g.astype(jnp.float32) ** 2), 1e-30))
    return jnp.exp2(jnp.round(0.5 * (log_rms(gk) - log_rms(gq))))


def _block_diag_ones(width):
    idx = np.arange(width) // HEAD_DIM
    return jnp.asarray(idx[:, None] == idx[None, :], dtype=jnp.bfloat16)


def _sw_relabel(t, start, axis):
    rep = SW_HEADS // SW_KV_HEADS
    seg = lax.slice_in_dim(t, start, start + SW_QW, axis=axis)
    shp = seg.shape
    seg = seg.reshape(shp[:axis] + (SW_KV_HEADS, rep, HEAD_DIM) + shp[axis + 1:])
    seg = jnp.swapaxes(seg, axis, axis + 1).reshape(shp)
    return jnp.concatenate([lax.slice_in_dim(t, 0, start, axis=axis), seg,
                            lax.slice_in_dim(t, start + SW_QW, t.shape[axis], axis=axis)], axis=axis)


def kernel(x, c, w_mod, b_mod, g_attn, w_in, rpb_na, sink_sw, t5_table, gq_ax, gk_ax, g_group,
           w_o, g_ffn, w_gu, w_down, g_final):
    B, S, D = x.shape
    L = w_mod.shape[0]
    bf = jnp.bfloat16

    c_pad = jnp.pad(c, ((0, 8 - B), (0, 0)))
    mod = _modulation(c_pad, w_mod, b_mod)[:, :B]
    mod = mod.reshape(L, B, 6, 1, D)

    cos, sin = _rope_tables(S)
    bd = _block_diag_ones(LANES)
    sw_bucket = _sw_bucket_tile()
    t5_flat = t5_table.astype(jnp.float32).reshape(-1)

    w_in_p = _sw_relabel(w_in, OFF_QB, axis=2).astype(bf)
    w_o_p = _sw_relabel(w_o, NA_W, axis=1).astype(bf)
    g_group_p = _sw_relabel(g_group, NA_W, axis=1)
    w_gu_b = w_gu.astype(bf)
    w_down_b = w_down.astype(bf)

    for l in range(L):
        sh_a, sc_a, gt_a, sh_f, sc_f, gt_f = [mod[l, :, i] for i in range(6)]
        balance = _pow2_balance(gq_ax[l], gk_ax[l])
        gq = gq_ax[l] * balance
        gk = gk_ax[l] / balance
        gain_max = jnp.stack([jnp.max(jnp.abs(gq)), jnp.max(jnp.abs(gk))])
        qa, ka, va, qb, kb, vb, qct, kc, vct, qnorm, kmax, q8, k8 = _in_proj(
            x, sc_a, sh_a, g_attn[l].reshape(1, D), w_in_p, l, cos, sin,
            jnp.tile(gq, LANES // HEAD_DIM).reshape(1, LANES),
            jnp.tile(gk, LANES // HEAD_DIM).reshape(1, LANES), bd)
        ya = _na_attention(qa, ka, va, _na_rpb_rows(rpb_na[l]))
        yb = _sw_attention(qb, kb, vb, sw_bucket, t5_flat, sink_sw[l])
        yc = _ax_dispatch(qct, kc, q8, k8, qnorm, kmax, vct, gain_max)
        x = _post(x, ya, yb, yc, g_group_p[l].reshape(1, D), w_o_p, gt_a, g_ffn[l].reshape(1, D),
                  sc_f, sh_f, gt_f, w_gu_b, w_down_b, g_final.reshape(1, D), layer=l,
                  final=(l == L - 1))
    return x
```

```python
import functools
import math

import jax
import jax.numpy as jnp
import numpy as np
from jax import lax
from jax.experimental import pallas as pl
from jax.experimental.pallas import tpu as pltpu

D_MODEL = 1024
HEAD_DIM = 64
GRID_W = 64
NA_HEADS = 4
SW_HEADS = 6
SW_KV_HEADS = 2
AX_HEADS = 6
AX_KV_HEADS = 2
NA_WIN_ROWS = 8
NA_WIN_COLS = 16
SW_RADIUS = 128
SW_BLOCK = 128
T5_BUCKETS = 32
T5_MAX_DIST = 128
ROPE_THETA = 10000.0
FFN_HIDDEN = 2816
EPS = 1e-6
NEG_INF = -1e30

NA_W = NA_HEADS * HEAD_DIM
SW_QW = SW_HEADS * HEAD_DIM
SW_KW = SW_KV_HEADS * HEAD_DIM
AX_QW = AX_HEADS * HEAD_DIM
AX_KW = AX_KV_HEADS * HEAD_DIM
IN_WIDTH = 3 * NA_W + SW_QW + 2 * SW_KW + AX_QW + 2 * AX_KW
OFF_QA, OFF_KA, OFF_VA = 0, NA_W, 2 * NA_W
OFF_QB = 3 * NA_W
OFF_KB = OFF_QB + SW_QW
OFF_VB = OFF_KB + SW_KW
OFF_QC = OFF_VB + SW_KW
OFF_KC = OFF_QC + AX_QW
OFF_VC = OFF_KC + AX_KW

LANES = 128
VMEM_LIMIT = 56 * 1024 * 1024

MOD_TILE = 1536
TOK_TILE = 512
IN_TILE = 2048
IN_SUB = 256
AX_TQ = 256
AX_TK = 512
AX_LOOKAHEAD = 2
AX_UNROLL = 8
AX_ONES_ROWS = 16
NA_QROWS = 4
NA_BAND = NA_QROWS + NA_WIN_ROWS
NA_TQ = 2048
NA_LOOKAHEAD = 2
NA_UNROLL = 2
SW_TQ = 2048
SW_LOOKAHEAD = 2
SW_UNROLL = 4
FFN_CHUNK = 512

QK_SCALE = HEAD_DIM ** -0.5
F8 = jnp.float8_e4m3fn
F8_MAX = 448.0
FP8_Q_HI, FP8_K_HI = 2.0, 0.5
FP8_Q_LO, FP8_K_LO = 16.0, 1.0 / 16.0
LOG2E = math.log2(math.e)
Q_SCALE_EXP2 = QK_SCALE * LOG2E
AX_MAX_SHIFT = 60.0


def _cparams(*sem):
    return pltpu.CompilerParams(dimension_semantics=sem, vmem_limit_bytes=VMEM_LIMIT)


def _const_spec(shape):
    n = len(shape)
    return pl.BlockSpec(shape, lambda *_: (0,) * n, pipeline_mode=pl.Buffered(1))


def _layer_spec(stacked, layer):
    n = stacked.ndim - 1
    return pl.BlockSpec((None,) + stacked.shape[1:], lambda *_: (layer,) + (0,) * n,
                        pipeline_mode=pl.Buffered(1))


def _rms(x, g):
    return x * lax.rsqrt(jnp.mean(x * x, axis=-1, keepdims=True) + EPS) * g


def _mod_kernel(c_ref, w_ref, b_ref, o_ref):
    c = c_ref[...]
    cond = c * (1.0 / (1.0 + jnp.exp(-c)))
    o_ref[0] = jnp.dot(cond, w_ref[0], preferred_element_type=jnp.float32,
                       precision=lax.Precision.HIGHEST) + b_ref[0]


def _modulation(c_pad, w_mod, b_mod):
    L, D, N = w_mod.shape
    tn = MOD_TILE
    return pl.pallas_call(
        _mod_kernel,
        out_shape=jax.ShapeDtypeStruct((L, c_pad.shape[0], N), jnp.float32),
        grid=(L, N // tn),
        in_specs=[pl.BlockSpec(c_pad.shape, lambda l, j: (0, 0)),
                  pl.BlockSpec((1, D, tn), lambda l, j: (l, 0, j)),
                  pl.BlockSpec((1, 1, tn), lambda l, j: (l, 0, j))],
        out_specs=pl.BlockSpec((1, c_pad.shape[0], tn), lambda l, j: (l, 0, j)),
        compiler_params=_cparams("arbitrary", "arbitrary"),
        name="adaln_mod",
    )(c_pad, w_mod, b_mod.reshape(L, 1, N))


def _head_sumsq(t, bd):
    t2 = (t * t).astype(jnp.bfloat16)
    return jnp.concatenate(
        [jnp.dot(t2[:, c:c + LANES], bd, preferred_element_type=jnp.float32)
         for c in range(0, t.shape[1], LANES)], axis=1)


def _rope_chunk(t, cos, sin_signed, first_half):
    swapped = jnp.where(first_half, pltpu.roll(t, LANES - 16, 1), pltpu.roll(t, 16, 1))
    return t * cos + swapped * sin_signed


def _in_kernel(x_ref, sc_ref, sh_ref, g_ref, w_ref, cos_ref, sin_ref, gq_ref, gk_ref,
               bd_ref,
               qa_ref, ka_ref, vat_ref, qb_ref, kb_ref, vbt_ref, qct_ref, kc_ref, vct_ref,
               qnorm_ref, kmax_ref, q8_ref, k8_ref):
    bf = jnp.bfloat16
    n_sub = x_ref.shape[1] // IN_SUB
    lane = lax.broadcasted_iota(jnp.int32, (IN_SUB, LANES), 1)
    first_half = (lane % 32) < 16

    gain = g_ref[...] * (1.0 + sc_ref[0])

    def normed(j):
        x = x_ref[0, j * IN_SUB:(j + 1) * IN_SUB, :]
        return (_rms(x, gain) + sh_ref[0]).astype(bf)

    def project(h):
        return jnp.dot(h, w_ref[...], preferred_element_type=jnp.float32)

    def head_stats(proj):
        return (_head_sumsq(proj[:, OFF_QC:OFF_QC + AX_QW], bd_ref[...]),
                _head_sumsq(proj[:, OFF_KC:OFF_KC + AX_KW], bd_ref[...]))

    def finish(j, proj, stats):
        rows = slice(j * IN_SUB, (j + 1) * IN_SUB)
        qa_ref[0, rows] = (proj[:, OFF_QA:OFF_QA + NA_W] * Q_SCALE_EXP2).astype(bf)
        ka_ref[0, rows] = proj[:, OFF_KA:OFF_KA + NA_W].astype(bf)
        qb_ref[0, rows] = (proj[:, OFF_QB:OFF_QB + SW_QW] * Q_SCALE_EXP2).astype(bf)
        kb_ref[0, rows] = proj[:, OFF_KB:OFF_KB + SW_KW].astype(bf)
        per_sub = IN_SUB // LANES
        for t in range(per_sub):
            blk = slice(t * LANES, (t + 1) * LANES)
            vat_ref[0, j * per_sub + t] = proj[blk, OFF_VA:OFF_VA + NA_W].T.astype(bf)
            vbt_ref[0, j * per_sub + t] = proj[blk, OFF_VB:OFF_VB + SW_KW].T.astype(bf)

        cos = cos_ref[rows, :]
        sin = sin_ref[rows, :]
        qss, kss = stats
        qn = proj[:, OFF_QC:OFF_QC + AX_QW] * lax.rsqrt(qss * (1.0 / HEAD_DIM) + EPS)
        chunks = []
        for c in range(AX_QW // LANES):
            t = qn[:, c * LANES:(c + 1) * LANES] * gq_ref[...]
            chunks.append(_rope_chunk(t, cos, sin, first_half) * Q_SCALE_EXP2)
        f32 = jnp.float32
        qt = jnp.concatenate(chunks, axis=1).T
        qct_ref[0, :, rows] = qt.astype(bf)
        a1 = (qt * FP8_Q_HI).astype(F8).astype(f32)
        a2 = ((qt - a1 * (1.0 / FP8_Q_HI)) * FP8_Q_LO).astype(F8).astype(f32)
        a3 = qt.astype(F8).astype(f32)
        q8_ref[0, 0, :, rows] = a1.astype(F8)
        q8_ref[0, 1, :, rows] = a2.astype(F8)
        q8_ref[0, 2, :, rows] = a3.astype(F8)
        qsq = a1 * a1 + a2 * a2 + a3 * a3
        norms = [jnp.sqrt(jnp.sum(qsq[hh * HEAD_DIM:(hh + 1) * HEAD_DIM], axis=0, keepdims=True))
                 for hh in range(AX_HEADS)]
        qnorm_ref[0, :, rows] = jnp.concatenate(
            norms + [jnp.zeros_like(norms[0])] * (8 - AX_HEADS), axis=0)

        kn = (proj[:, OFF_KC:OFF_KC + AX_KW] * lax.rsqrt(kss * (1.0 / HEAD_DIM) + EPS)
              * gk_ref[...])
        kr = _rope_chunk(kn, cos, sin, first_half)
        kb = kr.astype(bf)
        zero_bf = jnp.zeros_like(kb)
        kc_ref[0, 0, rows] = jnp.where(lane < HEAD_DIM, kb, zero_bf)
        kc_ref[0, 1, rows] = jnp.where(lane < HEAD_DIM, pltpu.roll(kr, HEAD_DIM, 1).astype(bf), zero_bf)
        x1 = kr * FP8_K_HI
        x2 = kr * FP8_K_LO
        x3 = kr - x1.astype(F8).astype(f32) * (1.0 / FP8_K_HI)
        ones_f = jnp.where(lane == HEAD_DIM, 1.0, 0.0)
        norms2 = []
        for g in range(AX_KV_HEADS):
            if g == 0:
                c0 = jnp.where(lane < HEAD_DIM, x1, pltpu.roll(x2, HEAD_DIM, 1))
                c1 = jnp.where(lane < HEAD_DIM, x3, ones_f)
            else:
                c0 = jnp.where(lane < HEAD_DIM, pltpu.roll(x1, HEAD_DIM, 1), x2)
                c1 = jnp.where(lane < HEAD_DIM, pltpu.roll(x3, HEAD_DIM, 1), ones_f)
            c0 = c0.astype(F8)
            c1 = c1.astype(F8)
            k8_ref[0, g, rows] = jnp.concatenate([c0, c1], axis=1)
            c0f = c0.astype(f32)
            c1f = jnp.where(lane < HEAD_DIM, c1.astype(f32), 0.0)
            norms2.append(jnp.max(jnp.sum(c0f * c0f + c1f * c1f, axis=1, keepdims=True),
                                  axis=0, keepdims=True))
        first_tk = (j * IN_SUB) // AX_TK
        off = (j * IN_SUB) % AX_TK
        vct_ref[0, first_tk, :, off:off + IN_SUB] = proj[:, OFF_VC:OFF_VC + AX_KW].T.astype(bf)
        return jnp.where(lane[:1] < HEAD_DIM, norms2[0], norms2[1])

    proj = project(normed(0))
    tile_max = None
    for j in range(n_sub):
        if j + 1 < n_sub:
            h_next = normed(j + 1)
        stats = head_stats(proj)
        if j + 1 < n_sub:
            proj_next = project(h_next)
        sub_max = finish(j, proj, stats)
        tile_max = sub_max if tile_max is None else jnp.maximum(tile_max, sub_max)
        if j + 1 < n_sub:
            proj = proj_next

    first = pl.program_id(1) == 0

    @pl.when(first)
    def _():
        kmax_ref[0] = tile_max

    @pl.when(jnp.logical_not(first))
    def _():
        kmax_ref[0] = jnp.maximum(kmax_ref[0], tile_max)


def _in_proj(x, sc, sh, g, w, layer, cos, sin, gq, gk, bd):
    B, S, D = x.shape
    tm = IN_TILE
    bf = jnp.bfloat16
    tok = lambda width: pl.BlockSpec((1, tm, width), lambda b, i: (b, i, 0))
    vec = pl.BlockSpec((1, 1, D), lambda b, i: (b, 0, 0))
    out_shape = (
        jax.ShapeDtypeStruct((B, S, NA_W), bf), jax.ShapeDtypeStruct((B, S, NA_W), bf),
        jax.ShapeDtypeStruct((B, S // LANES, NA_W, LANES), bf),
        jax.ShapeDtypeStruct((B, S, SW_QW), bf), jax.ShapeDtypeStruct((B, S, SW_KW), bf),
        jax.ShapeDtypeStruct((B, S // SW_BLOCK, SW_KW, SW_BLOCK), bf),
        jax.ShapeDtypeStruct((B, AX_QW, S), bf),
        jax.ShapeDtypeStruct((B, AX_KV_HEADS, S, LANES), bf),
        jax.ShapeDtypeStruct((B, S // AX_TK, AX_KW, AX_TK), bf),
        jax.ShapeDtypeStruct((B, 8, S), jnp.float32),
        jax.ShapeDtypeStruct((B, 1, AX_KW), jnp.float32),
        jax.ShapeDtypeStruct((B, 3, AX_QW, S), F8),
        jax.ShapeDtypeStruct((B, AX_KV_HEADS, S, 2 * LANES), F8),
    )
    out_specs = (
        tok(NA_W), tok(NA_W),
        pl.BlockSpec((1, tm // LANES, NA_W, LANES), lambda b, i: (b, i, 0, 0)),
        tok(SW_QW), tok(SW_KW),
        pl.BlockSpec((1, tm // SW_BLOCK, SW_KW, SW_BLOCK), lambda b, i: (b, i, 0, 0)),
        pl.BlockSpec((1, AX_QW, tm), lambda b, i: (b, 0, i)),
        pl.BlockSpec((1, AX_KV_HEADS, tm, LANES), lambda b, i: (b, 0, i, 0)),
        pl.BlockSpec((1, tm // AX_TK, AX_KW, AX_TK), lambda b, i: (b, i, 0, 0)),
        pl.BlockSpec((1, 8, tm), lambda b, i: (b, 0, i)),
        pl.BlockSpec((1, 1, AX_KW), lambda b, i: (b, 0, 0)),
        pl.BlockSpec((1, 3, AX_QW, tm), lambda b, i: (b, 0, 0, i)),
        pl.BlockSpec((1, AX_KV_HEADS, tm, 2 * LANES), lambda b, i: (b, 0, i, 0)),
    )
    return pl.pallas_call(
        _in_kernel,
        out_shape=out_shape,
        grid=(B, S // tm),
        in_specs=[tok(D), vec, vec, _const_spec((1, D)), _layer_spec(w, layer),
                  pl.BlockSpec((tm, LANES), lambda b, i: (i, 0)),
                  pl.BlockSpec((tm, LANES), lambda b, i: (i, 0)),
                  _const_spec((1, LANES)), _const_spec((1, LANES)),
                  _const_spec(bd.shape)],
        out_specs=out_specs,
        compiler_params=_cparams("arbitrary", "arbitrary"),
        name="in_proj",
    )(x, sc, sh, g, w, cos, sin, gq, gk, bd)


def _na_build_bias(rp_ref, bias_ref, rows):
    shape = (GRID_W, LANES)
    kc = lax.broadcasted_iota(jnp.int32, shape, 0)
    lane = lax.broadcasted_iota(jnp.int32, shape, 1)
    c = lane % GRID_W
    cs = jnp.clip(c - NA_WIN_COLS // 2, 0, GRID_W - NA_WIN_COLS)
    col_ok = (kc >= cs) & (kc < cs + NA_WIN_COLS)
    left = lane < GRID_W
    neg = jnp.full(shape, NEG_INF, jnp.float32)
    tq = NA_QROWS * GRID_W
    for h in range(NA_HEADS):
        pair = []
        for a in range(2 * NA_WIN_ROWS):
            x = jnp.broadcast_to(rp_ref[h, a:a + 1, :] * LOG2E, shape)
            t = pltpu.roll(x, LANES - (NA_WIN_COLS - 1), 1, stride=1, stride_axis=0)
            pair.append(jnp.where(col_ok, t, neg))
        for variant, qr0 in enumerate((0, 2 * NA_QROWS, rows - NA_QROWS)):
            bs = int(np.clip(qr0 - NA_WIN_ROWS // 2, 0, rows - NA_BAND))
            for kj in range(NA_BAND):
                kr = bs + kj
                for u in range(NA_QROWS // 2):
                    r0 = qr0 + 2 * u
                    ok = [int(np.clip(r - NA_WIN_ROWS // 2, 0, rows - NA_WIN_ROWS)) <= kr
                          < int(np.clip(r - NA_WIN_ROWS // 2, 0, rows - NA_WIN_ROWS)) + NA_WIN_ROWS
                          for r in (r0, r0 + 1)]
                    a0 = kr - r0 + NA_WIN_ROWS - 1
                    if ok[0] and ok[1]:
                        tile = pair[a0]
                    elif ok[0]:
                        tile = jnp.where(left, pair[a0], neg)
                    elif ok[1]:
                        tile = jnp.where(left, neg, pair[a0])
                    else:
                        tile = neg
                    bias_ref[variant, kj * GRID_W:(kj + 1) * GRID_W,
                             h * tq + u * LANES:h * tq + (u + 1) * LANES] = tile


def _na_kernel(q_ref, k_ref, vt_ref, rp_ref, o_ref, bias_ref, s_ref, *, rows):
    i = pl.program_id(1)
    tq = NA_QROWS * GRID_W
    nblk = NA_TQ // tq
    nblk_total = rows // NA_QROWS
    bf = jnp.bfloat16

    @pl.when((pl.program_id(0) == 0) & (i == 0))
    def _():
        _na_build_bias(rp_ref, bias_ref, rows)

    head_of_lane = lax.broadcasted_iota(jnp.int32, (tq, NA_W), 1) // HEAD_DIM
    ones = jnp.ones((16, NA_BAND * GRID_W), bf)

    def band_row(blk):
        return jnp.clip((i * nblk + blk) * NA_QROWS - NA_WIN_ROWS // 2, 0, rows - NA_BAND)

    def scores(blk):
        q = q_ref[0, pl.ds(pl.multiple_of(blk * tq, tq), tq), :]
        stack = jnp.concatenate([jnp.where(head_of_lane == h, q, jnp.zeros_like(q))
                                 for h in range(NA_HEADS)], axis=0)
        start = pl.multiple_of(band_row(blk) * GRID_W, NA_QROWS * GRID_W)
        kb = k_ref[0, pl.ds(start, NA_BAND * GRID_W), :]
        return lax.dot_general(kb, stack, (((1,), (1,)), ((), ())),
                               preferred_element_type=jnp.float32)

    for t in range(NA_LOOKAHEAD):
        s_ref[t] = scores(t)

    def body(trip, carry):
        for t in range(NA_LOOKAHEAD):
            blk = trip * NA_LOOKAHEAD + t
            s = s_ref[t]
            s_ref[t] = scores(jnp.minimum(blk + NA_LOOKAHEAD, nblk - 1))
            n = i * nblk + blk
            variant = jnp.where(n == 0, 0, jnp.where(n == nblk_total - 1, 2, 1))
            s = s + bias_ref[variant]
            m = jnp.max(s, axis=0, keepdims=True)
            p = jnp.exp2(s - m).astype(bf)
            first = band_row(blk) * GRID_W // LANES
            vt = jnp.concatenate([vt_ref[0, first + u] for u in range(NA_BAND * GRID_W // LANES)],
                                 axis=1)
            outs = []
            for h in range(NA_HEADS):
                vt_ext = jnp.concatenate([vt[h * HEAD_DIM:(h + 1) * HEAD_DIM], ones], axis=0)
                o = jnp.dot(vt_ext, p[:, h * tq:(h + 1) * tq], preferred_element_type=jnp.float32)
                outs.append(o[:HEAD_DIM] / o[HEAD_DIM:HEAD_DIM + 1])
            o_ref[0, pl.ds(pl.multiple_of(blk * tq, tq), tq), :] = jnp.concatenate(outs, axis=0).T
        return carry

    lax.fori_loop(0, nblk // NA_LOOKAHEAD, body, 0, unroll=NA_UNROLL)


def _na_attention(q, k, vt, rp):
    B, S, W = q.shape
    rows = S // GRID_W
    tq = NA_QROWS * GRID_W
    return pl.pallas_call(
        functools.partial(_na_kernel, rows=rows),
        out_shape=jax.ShapeDtypeStruct((B, S, W), jnp.float32),
        grid=(B, S // NA_TQ),
        in_specs=[pl.BlockSpec((1, NA_TQ, W), lambda b, j: (b, j, 0)),
                  pl.BlockSpec((1, S, W), lambda b, j: (b, 0, 0), pipeline_mode=pl.Buffered(1)),
                  pl.BlockSpec((1, S // LANES, W, LANES), lambda b, j: (b, 0, 0, 0),
                               pipeline_mode=pl.Buffered(1)),
                  _const_spec(rp.shape)],
        out_specs=pl.BlockSpec((1, NA_TQ, W), lambda b, j: (b, j, 0)),
        scratch_shapes=[pltpu.VMEM((3, NA_BAND * GRID_W, NA_HEADS * tq), jnp.float32),
                        pltpu.VMEM((NA_LOOKAHEAD, NA_BAND * GRID_W, NA_HEADS * tq), jnp.float32)],
        compiler_params=_cparams("arbitrary", "arbitrary"),
        name="na_attn",
    )(q, k, vt, rp)


def _na_rpb_rows(rpb):
    n_col = rpb.shape[-1]
    p = jnp.pad(rpb.astype(jnp.float32)[:, :, ::-1], ((0, 0), (1, 1), (0, GRID_W - n_col)))
    return jnp.concatenate([p[:, 1:], p[:, :-1]], axis=-1)


def _sw_kernel(sink_ref, t5_ref, q_ref, k_ref, vt_ref, bucket_ref, o_ref, bias_ref, s_ref, *, seq):
    i = pl.program_id(1)
    nblk_total = seq // SW_BLOCK

    @pl.when((pl.program_id(0) == 0) & (i == 0))
    def _():
        bucket = bucket_ref[...]
        key_row = lax.broadcasted_iota(jnp.int32, bucket.shape, 0)
        for h in range(SW_HEADS):
            acc = jnp.full(bucket.shape, NEG_INF, jnp.float32)
            for b in range(T5_BUCKETS):
                acc = jnp.where(bucket == b, t5_ref[b * SW_HEADS + h] * LOG2E, acc)
            cols = slice(h * SW_BLOCK, (h + 1) * SW_BLOCK)
            bias_ref[0, :, cols] = acc
            bias_ref[1, :, cols] = jnp.where(key_row < SW_BLOCK, NEG_INF, acc)
            bias_ref[2, :, cols] = jnp.where(key_row >= 2 * SW_BLOCK, NEG_INF, acc)

    nblk = SW_TQ // SW_BLOCK
    rep = SW_HEADS // SW_KV_HEADS
    bf = jnp.bfloat16
    lane_group = lax.broadcasted_iota(jnp.int32, (SW_BLOCK, LANES), 1) // HEAD_DIM
    row_group = lax.broadcasted_iota(jnp.int32, (SW_BLOCK, LANES), 0) // HEAD_DIM
    sink = jnp.concatenate([jnp.full((1, SW_BLOCK), sink_ref[h] * LOG2E, jnp.float32)
                            for h in range(SW_HEADS)], axis=1)
    ones = jnp.ones((16, 3 * SW_BLOCK), bf)

    def neighbours(blk):
        n = i * nblk + blk
        return jnp.maximum(n - 1, 0), n, jnp.minimum(n + 1, nblk_total - 1)

    def scores(blk):
        rows = pl.ds(pl.multiple_of(blk * SW_BLOCK, SW_BLOCK), SW_BLOCK)
        stack = []
        for g in range(SW_KV_HEADS):
            for r in range(rep):
                qcol = q_ref[0, rows, r * LANES:(r + 1) * LANES]
                stack.append(jnp.where(lane_group == g, qcol, jnp.zeros_like(qcol)))
        kw = jnp.concatenate([k_ref[0, pl.ds(pl.multiple_of(nb * SW_BLOCK, SW_BLOCK), SW_BLOCK), :]
                              for nb in neighbours(blk)], axis=0)
        return lax.dot_general(kw, jnp.concatenate(stack, axis=0), (((1,), (1,)), ((), ())),
                               preferred_element_type=jnp.float32)

    for t in range(SW_LOOKAHEAD):
        s_ref[t] = scores(t)

    def body(trip, carry):
        for t in range(SW_LOOKAHEAD):
            blk = trip * SW_LOOKAHEAD + t
            s = s_ref[t]
            s_ref[t] = scores(jnp.minimum(blk + SW_LOOKAHEAD, nblk - 1))
            left, n, right = neighbours(blk)
            edge = jnp.where(n == 0, 1, jnp.where(n == nblk_total - 1, 2, 0))
            s = s + bias_ref[edge]
            m = jnp.maximum(jnp.max(s, axis=0, keepdims=True), sink)
            p = jnp.exp2(s - m).astype(bf)
            vt_ext = jnp.concatenate(
                [jnp.concatenate([vt_ref[0, left], vt_ref[0, n], vt_ref[0, right]], axis=1), ones],
                axis=0)
            o = jnp.dot(vt_ext, p, preferred_element_type=jnp.float32)
            res = o[:LANES] / (o[LANES:LANES + 1] + jnp.exp2(sink - m))
            rows = pl.ds(pl.multiple_of(blk * SW_BLOCK, SW_BLOCK), SW_BLOCK)
            for r in range(rep):
                yt = jnp.where(row_group == 0, res[:, r * SW_BLOCK:(r + 1) * SW_BLOCK],
                               res[:, (rep + r) * SW_BLOCK:(rep + r + 1) * SW_BLOCK])
                o_ref[0, rows, r * LANES:(r + 1) * LANES] = yt.T
        return carry

    lax.fori_loop(0, nblk // SW_LOOKAHEAD, body, 0, unroll=SW_UNROLL)


def _sw_attention(q, k, vt, bucket, t5_flat, sink):
    B, S, _ = q.shape
    nb = S // SW_BLOCK
    return pl.pallas_call(
        functools.partial(_sw_kernel, seq=S),
        out_shape=jax.ShapeDtypeStruct((B, S, SW_QW), jnp.float32),
        grid_spec=pltpu.PrefetchScalarGridSpec(
            num_scalar_prefetch=2,
            grid=(B, S // SW_TQ),
            in_specs=[pl.BlockSpec((1, SW_TQ, SW_QW), lambda b, i, *_: (b, i, 0)),
                      pl.BlockSpec((1, S, SW_KW), lambda b, i, *_: (b, 0, 0),
                                   pipeline_mode=pl.Buffered(1)),
                      pl.BlockSpec((1, nb, SW_KW, SW_BLOCK), lambda b, i, *_: (b, 0, 0, 0),
                                   pipeline_mode=pl.Buffered(1)),
                      pl.BlockSpec(bucket.shape, lambda b, i, *_: (0, 0),
                                   pipeline_mode=pl.Buffered(1))],
            out_specs=pl.BlockSpec((1, SW_TQ, SW_QW), lambda b, i, *_: (b, i, 0)),
            scratch_shapes=[pltpu.VMEM((3, 3 * SW_BLOCK, SW_HEADS * SW_BLOCK), jnp.float32),
                            pltpu.VMEM((SW_LOOKAHEAD, 3 * SW_BLOCK, SW_HEADS * SW_BLOCK),
                                       jnp.float32)],
        ),
        compiler_params=_cparams("arbitrary", "arbitrary"),
        name="sw_attn",
    )(sink, t5_flat, q, k, vt, bucket)


def _t5_bucket(rel):
    nb = T5_BUCKETS // 2
    ret = (rel > 0).astype(jnp.int32) * nb
    n = jnp.abs(rel)
    max_exact = nb // 2
    nf = jnp.maximum(n, max_exact).astype(jnp.float32)
    large = max_exact + (jnp.log(nf / max_exact) / math.log(T5_MAX_DIST / max_exact)
                         * (nb - max_exact)).astype(jnp.int32)
    large = jnp.minimum(large, nb - 1)
    return ret + jnp.where(n < max_exact, n, large)


def _sw_bucket_tile():
    qpos = jnp.arange(SW_BLOCK)
    kpos = jnp.arange(3 * SW_BLOCK) - SW_BLOCK
    rel = kpos[:, None] - qpos[None, :]
    return jnp.where(jnp.abs(rel) <= SW_RADIUS, _t5_bucket(rel), -1).astype(jnp.int32)


def _ax_kernel(q_ref, qnorm_ref, kmax_ref, k_ref, vt_ref, o_ref, qpad_ref, m_ref, l_ref,
               acc_ref, s0_ref, *, nkt, bounded):
    tq = q_ref.shape[-1]
    tk = AX_TK
    rep = AX_HEADS // AX_KV_HEADS
    bf = jnp.bfloat16
    for h in range(AX_HEADS):
        g = h // rep
        head = slice(h * HEAD_DIM, (h + 1) * HEAD_DIM)
        if bounded:
            row = lax.broadcasted_iota(jnp.int32, (32, tq), 0)
            shift = qnorm_ref[0, h:h + 1, :] * jnp.sqrt(kmax_ref[0, :, g * HEAD_DIM:g * HEAD_DIM + 1])
            extra = jnp.where(row == 0, -shift, 0.0).astype(F8)
            qpad_ref[h] = jnp.concatenate(
                [q_ref[0, 0, head, :], q_ref[0, 1, head, :], q_ref[0, 2, head, :], extra,
                 jnp.zeros((32, tq), F8)], axis=0)
        else:
            qpad_ref[h] = jnp.concatenate(
                [q_ref[0, head, :], jnp.zeros((LANES - HEAD_DIM, tq), bf)], axis=0)
    if not bounded:
        m_ref[...] = jnp.full(m_ref.shape, -jnp.inf, jnp.float32)
        l_ref[...] = jnp.zeros(l_ref.shape, jnp.float32)
    acc_ref[...] = jnp.zeros(acc_ref.shape, jnp.float32)

    def score(kt, h):
        ks = pl.multiple_of(kt * tk, tk)
        return jnp.dot(k_ref[0, h // rep, pl.ds(ks, tk), :], qpad_ref[h],
                       preferred_element_type=jnp.float32)

    for h in range(AX_LOOKAHEAD):
        s0_ref[h] = score(0, h)

    def body(kt, carry):
        pending = [s0_ref[h] for h in range(AX_LOOKAHEAD)]
        for h in range(AX_HEADS):
            g = h // rep
            s = pending.pop(0)
            ahead = h + AX_LOOKAHEAD
            if ahead < AX_HEADS:
                pending.append(score(kt, ahead))
            else:
                s0_ref[ahead - AX_HEADS] = score(jnp.minimum(kt + 1, nkt - 1), ahead - AX_HEADS)
            vt = vt_ref[0, kt, g * HEAD_DIM:(g + 1) * HEAD_DIM, :]
            if bounded:
                p = jnp.exp2(s)
                vt_ext = jnp.concatenate([vt, jnp.ones((AX_ONES_ROWS, tk), bf)], axis=0)
                acc_ref[h] += jnp.dot(vt_ext, p.astype(bf), preferred_element_type=jnp.float32)
            else:
                m_old = m_ref[h]
                m_new = jnp.maximum(m_old, jnp.max(s, axis=0, keepdims=True))
                alpha = jnp.exp2(m_old - m_new)
                p = jnp.exp2(s - m_new)
                l_ref[h] = alpha * l_ref[h] + jnp.sum(p.reshape(tk // 8, 8, tq), axis=0)
                m_ref[h] = m_new
                pv = jnp.dot(vt, p.astype(bf), preferred_element_type=jnp.float32)
                acc_ref[h, :HEAD_DIM] = alpha * acc_ref[h, :HEAD_DIM] + pv
        return carry

    lax.fori_loop(0, nkt, body, 0, unroll=AX_UNROLL)
    if bounded:
        denom = [acc_ref[h, HEAD_DIM:HEAD_DIM + 1] for h in range(AX_HEADS)]
    else:
        denom = [jnp.sum(l_ref[h], axis=0, keepdims=True) for h in range(AX_HEADS)]
    out_t = jnp.concatenate([acc_ref[h, :HEAD_DIM] / denom[h]
                             for h in range(AX_HEADS)], axis=0)
    o_ref[0] = out_t.T


def _ax_attention(qt, k, q8, k8, qnorm, kmax, vt, bounded):
    B, W, S = qt.shape
    nkt = vt.shape[1]
    if bounded:
        q, keys = q8, k8
        q_spec = pl.BlockSpec((1, 3, W, AX_TQ), lambda b, i: (b, 0, 0, i))
        qpad = pltpu.VMEM((AX_HEADS, 2 * LANES, AX_TQ), F8)
    else:
        q, keys = qt, k
        q_spec = pl.BlockSpec((1, W, AX_TQ), lambda b, i: (b, 0, i))
        qpad = pltpu.VMEM((AX_HEADS, LANES, AX_TQ), jnp.bfloat16)
    return pl.pallas_call(
        functools.partial(_ax_kernel, nkt=nkt, bounded=bounded),
        out_shape=jax.ShapeDtypeStruct((B, S, W), jnp.float32),
        grid=(B, S // AX_TQ),
        in_specs=[q_spec,
                  pl.BlockSpec((1, 8, AX_TQ), lambda b, i: (b, 0, i)),
                  pl.BlockSpec((1, 1, AX_KW), lambda b, i: (b, 0, 0)),
                  pl.BlockSpec((1,) + keys.shape[1:], lambda b, i: (b, 0, 0, 0),
                               pipeline_mode=pl.Buffered(1)),
                  pl.BlockSpec((1, nkt, AX_KW, AX_TK), lambda b, i: (b, 0, 0, 0),
                               pipeline_mode=pl.Buffered(1))],
        out_specs=pl.BlockSpec((1, AX_TQ, W), lambda b, i: (b, i, 0)),
        scratch_shapes=[qpad,
                        pltpu.VMEM((AX_HEADS, 1, AX_TQ), jnp.float32),
                        pltpu.VMEM((AX_HEADS, 8, AX_TQ), jnp.float32),
                        pltpu.VMEM((AX_HEADS, HEAD_DIM + AX_ONES_ROWS, AX_TQ), jnp.float32),
                        pltpu.VMEM((AX_LOOKAHEAD, AX_TK, AX_TQ), jnp.float32)],
        compiler_params=_cparams("arbitrary", "arbitrary"),
        name="ax_attn_bounded" if bounded else "ax_attn_online",
    )(q, qnorm, kmax, keys, vt)


def _ax_dispatch(qt, k, q8, k8, qnorm, kmax, vt, gain_max):
    rep = AX_HEADS // AX_KV_HEADS
    knorm = jnp.sqrt(kmax[:, 0, ::HEAD_DIM])
    shift_max = jnp.max(jnp.max(qnorm[:, :AX_HEADS], axis=-1) * jnp.repeat(knorm, rep, axis=1))
    entry_max = HEAD_DIM ** 0.5 * gain_max * jnp.array([FP8_Q_HI * Q_SCALE_EXP2, 1.0])
    safe = (shift_max <= AX_MAX_SHIFT) & jnp.all(entry_max <= F8_MAX / 2)
    return lax.cond(safe,
                    functools.partial(_ax_attention, bounded=True),
                    functools.partial(_ax_attention, bounded=False),
                    qt, k, q8, k8, qnorm, kmax, vt)


def _post_kernel(x_ref, ya_ref, yb_ref, yc_ref, gg_ref, wo_ref, gta_ref, gf_ref, scf_ref, shf_ref,
                 gtf_ref, wgu_ref, wd_ref, gfin_ref, o_ref, *, final):
    bf = jnp.bfloat16
    gg = gg_ref[...]
    y = jnp.concatenate([
        _rms(ya_ref[0], gg[:, 0:NA_W]).astype(bf),
        _rms(yb_ref[0], gg[:, NA_W:NA_W + SW_QW]).astype(bf),
        _rms(yc_ref[0], gg[:, NA_W + SW_QW:]).astype(bf)], axis=1)
    x1 = x_ref[0] + gta_ref[0] * jnp.dot(y, wo_ref[...], preferred_element_type=jnp.float32)

    h = (_rms(x1, gf_ref[...] * (1.0 + scf_ref[0])) + shf_ref[0]).astype(bf)
    acc = jnp.zeros(x1.shape, jnp.float32)
    for c0 in range(0, FFN_HIDDEN, FFN_CHUNK):
        c1 = min(c0 + FFN_CHUNK, FFN_HIDDEN)
        gate = jnp.dot(h, wgu_ref[:, c0:c1], preferred_element_type=jnp.float32)
        up = jnp.dot(h, wgu_ref[:, FFN_HIDDEN + c0:FFN_HIDDEN + c1],
                     preferred_element_type=jnp.float32)
        act = (gate * (1.0 / (1.0 + jnp.exp(-gate))) * up).astype(bf)
        acc = acc + jnp.dot(act, wd_ref[c0:c1, :], preferred_element_type=jnp.float32)
    x2 = x1 + gtf_ref[0] * acc
    if final:
        x2 = _rms(x2, gfin_ref[...])
    o_ref[0] = x2


def _post(x, ya, yb, yc, gg, wo, gta, gf, scf, shf, gtf, wgu, wd, gfin, layer, final):
    B, S, D = x.shape
    tm = TOK_TILE
    tok = lambda width: pl.BlockSpec((1, tm, width), lambda b, i: (b, i, 0))
    vec = pl.BlockSpec((1, 1, D), lambda b, i: (b, 0, 0))
    return pl.pallas_call(
        functools.partial(_post_kernel, final=final),
        out_shape=jax.ShapeDtypeStruct((B, S, D), jnp.float32),
        grid=(B, S // tm),
        in_specs=[tok(D), tok(NA_W), tok(SW_QW), tok(AX_QW), _const_spec((1, D)),
                  _layer_spec(wo, layer), vec, _const_spec((1, D)), vec, vec, vec,
                  _layer_spec(wgu, layer), _layer_spec(wd, layer), _const_spec((1, D))],
        out_specs=tok(D),
        compiler_params=_cparams("arbitrary", "arbitrary"),
        name="out_proj_ffn",
    )(x, ya, yb, yc, gg, wo, gta, gf, scf, shf, gtf, wgu, wd, gfin)


def _rope_tables(S):
    t = jnp.arange(S)
    row = (t // GRID_W).astype(jnp.float32)
    col = (t % GRID_W).astype(jnp.float32)
    axis_dim = HEAD_DIM // 2
    freqs = ROPE_THETA ** (-jnp.arange(0, axis_dim, 2, dtype=jnp.float32) / axis_dim)
    ang = jnp.stack([row[:, None] * freqs, col[:, None] * freqs], axis=1)
    cos = jnp.cos(ang)
    sin = jnp.sin(ang)
    cos_h = jnp.concatenate([cos, cos], axis=-1).reshape(S, HEAD_DIM)
    sin_h = jnp.concatenate([-sin, sin], axis=-1).reshape(S, HEAD_DIM)
    return jnp.tile(cos_h, (1, LANES // HEAD_DIM)), jnp.tile(sin_h, (1, LANES // HEAD_DIM))


def _pow2_balance(gq, gk):
    log_rms = lambda g: 0.5 * jnp.log2(jnp.maximum(jnp.mean(g.astype(jnp.float32) ** 2), 1e-30))
    return jnp.exp2(jnp.round(0.5 * (log_rms(gk) - log_rms(gq))))


def _block_diag_ones(width):
    idx = np.arange(width) // HEAD_DIM
    return jnp.asarray(idx[:, None] == idx[None, :], dtype=jnp.bfloat16)


def _sw_relabel(t, start, axis):
    rep = SW_HEADS // SW_KV_HEADS
    seg = lax.slice_in_dim(t, start, start + SW_QW, axis=axis)
    shp = seg.shape
    seg = seg.reshape(shp[:axis] + (SW_KV_HEADS, rep, HEAD_DIM) + shp[axis + 1:])
    seg = jnp.swapaxes(seg, axis, axis + 1).reshape(shp)
    return jnp.concatenate([lax.slice_in_dim(t, 0, start, axis=axis), seg,
                            lax.slice_in_dim(t, start + SW_QW, t.shape[axis], axis=axis)], axis=axis)


def kernel(x, c, w_mod, b_mod, g_attn, w_in, rpb_na, sink_sw, t5_table, gq_ax, gk_ax, g_group,
           w_o, g_ffn, w_gu, w_down, g_final):
    B, S, D = x.shape
    L = w_mod.shape[0]
    bf = jnp.bfloat16

    c_pad = jnp.pad(c, ((0, 8 - B), (0, 0)))
    mod = _modulation(c_pad, w_mod, b_mod)[:, :B]
    mod = mod.reshape(L, B, 6, 1, D)

    cos, sin = _rope_tables(S)
    bd = _block_diag_ones(LANES)
    sw_bucket = _sw_bucket_tile()
    t5_flat = t5_table.astype(jnp.float32).reshape(-1)

    w_in_p = _sw_relabel(w_in, OFF_QB, axis=2).astype(bf)
    w_o_p = _sw_relabel(w_o, NA_W, axis=1).astype(bf)
    g_group_p = _sw_relabel(g_group, NA_W, axis=1)
    w_gu_b = w_gu.astype(bf)
    w_down_b = w_down.astype(bf)

    for l in range(L):
        sh_a, sc_a, gt_a, sh_f, sc_f, gt_f = [mod[l, :, i] for i in range(6)]
        balance = _pow2_balance(gq_ax[l], gk_ax[l])
        gq = gq_ax[l] * balance
        gk = gk_ax[l] / balance
        gain_max = jnp.stack([jnp.max(jnp.abs(gq)), jnp.max(jnp.abs(gk))])
        qa, ka, va, qb, kb, vb, qct, kc, vct, qnorm, kmax, q8, k8 = _in_proj(
            x, sc_a, sh_a, g_attn[l].reshape(1, D), w_in_p, l, cos, sin,
            jnp.tile(gq, LANES // HEAD_DIM).reshape(1, LANES),
            jnp.tile(gk, LANES // HEAD_DIM).reshape(1, LANES), bd)
        ya = _na_attention(qa, ka, va, _na_rpb_rows(rpb_na[l]))
        yb = _sw_attention(qb, kb, vb, sw_bucket, t5_flat, sink_sw[l])
        yc = _ax_dispatch(qct, kc, q8, k8, qnorm, kmax, vct, gain_max)
        x = _post(x, ya, yb, yc, g_group_p[l].reshape(1, D), w_o_p, gt_a, g_ffn[l].reshape(1, D),
                  sc_f, sh_f, gt_f, w_gu_b, w_down_b, g_final.reshape(1, D), layer=l,
                  final=(l == L - 1))
    return x
```
